```python
import math
import jax, jax.numpy as jnp
from jax import lax
import numpy as np

D_MODEL = 4096
BATCH = 4
SEQ = 4096
DEPTH = 1

RET_HEADS = 8
RET_DK = 256
RET_DV = 256
RET_CHUNK = 128
MLA_HEADS = 16
MLA_Q_RANK = 768
MLA_KV_RANK = 512
MLA_NOPE = 128
MLA_ROPE = 64
MLA_V = 128
MLA_QK = MLA_NOPE + MLA_ROPE
ATTN_BLOCK = 128
ROPE_BASE = 10000.0
RET_W = RET_HEADS * RET_DV
MLA_W = MLA_HEADS * MLA_V
MIX_W = RET_W + MLA_W
IN_SIZES = (RET_HEADS * RET_DK, RET_HEADS * RET_DK, RET_W, RET_W,
            MLA_Q_RANK, MLA_KV_RANK, MLA_ROPE)
IN_W = sum(IN_SIZES)
IN_SPLIT = tuple(int(v) for v in np.cumsum(IN_SIZES)[:-1])
N_EXPERTS = 64
TOP_K = 8
N_GROUPS = 8
TOPK_GROUPS = 4
EXPERT_FF = 512
SHARED_FF = 512
ROUTED_SCALE = 2.5
EPS = 1e-6

kernel_name = "hybrid_retention_mla_moe_adaln"


def _rms(x):
    xf = x.astype(jnp.float32)
    return xf * lax.rsqrt(jnp.mean(xf * xf, axis=-1, keepdims=True) + EPS)


def rms_norm(x, g):
    return (_rms(x) * g.astype(jnp.float32)).astype(x.dtype)


def modulate(h, shift, scale):
    return h * (1.0 + scale[:, None, :]) + shift[:, None, :]


def rope_tables(positions, dim):
    inv = ROPE_BASE ** (-jnp.arange(0, dim, 2, dtype=jnp.float32) / dim)
    ang = positions.astype(jnp.float32)[..., None] * inv
    return jnp.cos(ang)[:, :, None, :], jnp.sin(ang)[:, :, None, :]


def apply_rope(x, cos, sin):
    x1, x2 = jnp.split(x, 2, axis=-1)
    return jnp.concatenate([x1 * cos - x2 * sin, x2 * cos + x1 * sin], axis=-1)


def retention(q, k, v):
    b, s, h, dk = q.shape
    dv = v.shape[-1]
    C = RET_CHUNK
    n = s // C
    log_g = jnp.log(1.0 - 2.0 ** (-5.0 - jnp.arange(h, dtype=jnp.float32)))
    i = jnp.arange(C, dtype=jnp.float32)
    diff = i[:, None] - i[None, :]
    inner_decay = jnp.where(diff[None] >= 0, jnp.exp(diff[None] * log_g[:, None, None]), 0.0)
    q_decay = jnp.exp((i + 1.0)[:, None] * log_g[None, :])
    k_decay = jnp.exp((C - 1.0 - i)[:, None] * log_g[None, :])
    chunk_decay = jnp.exp(C * log_g)
    qc = q.reshape(b, n, C, h, dk)
    kc = k.reshape(b, n, C, h, dk)
    vc = v.reshape(b, n, C, h, dv)
    sc = jnp.einsum('bnihd,bnjhd->bnhij', qc, kc) * inner_decay[None, None]
    inner = jnp.einsum('bnhij,bnjhe->bnihe', sc, vc)

    def step(state, blk):
        qb, kb, vb = blk
        cross = jnp.einsum('bihd,bhde->bihe', qb, state) * q_decay[None, :, :, None]
        kv = jnp.einsum('bjhd,bjhe->bhde', kb * k_decay[None, :, :, None], vb)
        state = state * chunk_decay[None, :, None, None] + kv
        return state, cross

    state0 = jnp.zeros((b, h, dk, dv), jnp.float32)
    _, cross = lax.scan(step, state0, (qc.transpose(1, 0, 2, 3, 4),
                                       kc.transpose(1, 0, 2, 3, 4),
                                       vc.transpose(1, 0, 2, 3, 4)))
    cross = cross.transpose(1, 0, 2, 3, 4)
    return (inner + cross).reshape(b, s, h, dv)


def causal_block_attention(q, k, v):
    b, s, h, d = q.shape
    nb = s // ATTN_BLOCK
    scale = 1.0 / math.sqrt(d)
    qb = q.reshape(b, nb, ATTN_BLOCK, h, d).transpose(1, 0, 2, 3, 4)
    key_idx = jnp.arange(s)

    def one_block(args):
        qblk, bi = args
        sc = jnp.einsum('bqhd,bkhd->bhqk', qblk, k).astype(jnp.float32) * scale
        q_idx = bi * ATTN_BLOCK + jnp.arange(ATTN_BLOCK)
        mask = key_idx[None, :] <= q_idx[:, None]
        sc = jnp.where(mask[None, None], sc, -jnp.inf)
        p = jax.nn.softmax(sc, axis=-1)
        return jnp.einsum('bhqk,bkhd->bqhd', p.astype(v.dtype), v)

    out = lax.map(one_block, (qb, jnp.arange(nb)))
    return out.transpose(1, 0, 2, 3, 4).reshape(b, s, h, v.shape[-1])


def mla(cq, ckv, kr, cos_m, sin_m, q_a_norm_g, w_uq, kv_a_norm_g, w_ukv, q_norm_g, k_norm_g):
    b, s, _ = cq.shape
    q = (rms_norm(cq, q_a_norm_g) @ w_uq).reshape(b, s, MLA_HEADS, MLA_QK)
    kv = (rms_norm(ckv, kv_a_norm_g) @ w_ukv).reshape(b, s, MLA_HEADS, MLA_NOPE + MLA_V)
    k_nope, v = kv[..., :MLA_NOPE], kv[..., MLA_NOPE:]
    k_rope = jnp.broadcast_to(kr[:, :, None, :], (b, s, MLA_HEADS, MLA_ROPE))
    k = jnp.concatenate([k_nope, k_rope], axis=-1)
    q = rms_norm(q, q_norm_g)
    k = rms_norm(k, k_norm_g)
    q = jnp.concatenate([q[..., :MLA_NOPE], apply_rope(q[..., MLA_NOPE:], cos_m, sin_m)], axis=-1)
    k = jnp.concatenate([k[..., :MLA_NOPE], apply_rope(k[..., MLA_NOPE:], cos_m, sin_m)], axis=-1)
    o = causal_block_attention(q, k, v)
    return o.reshape(b, s, MLA_W)


def moe(h, w_router, router_bias, w_exp_gate, w_exp_up, w_exp_down, w_sh_gate, w_sh_up, w_sh_down):
    b, s, d = h.shape
    t = b * s
    hf = h.reshape(t, d)
    scores = jax.nn.sigmoid((hf @ w_router).astype(jnp.float32))
    sel = scores + router_bias.astype(jnp.float32)
    per_group = N_EXPERTS // N_GROUPS
    group_score = jnp.sum(lax.top_k(sel.reshape(t, N_GROUPS, per_group), 2)[0], axis=-1)
    _, g_idx = lax.top_k(group_score, TOPK_GROUPS)
    g_mask = jnp.sum(jax.nn.one_hot(g_idx, N_GROUPS, dtype=jnp.float32), axis=1)
    e_mask = jnp.repeat(g_mask, per_group, axis=-1)
    _, e_idx = lax.top_k(jnp.where(e_mask > 0, sel, -jnp.inf), TOP_K)
    w = jnp.take_along_axis(scores, e_idx, axis=-1)
    w = w / jnp.sum(w, axis=-1, keepdims=True) * ROUTED_SCALE
    gates = jnp.sum(jax.nn.one_hot(e_idx, N_EXPERTS, dtype=jnp.float32) * w[..., None], axis=1)
    routed = jnp.zeros((t, d), jnp.float32)
    for e in range(N_EXPERTS):
        a = jax.nn.silu(hf @ w_exp_gate[e]) * (hf @ w_exp_up[e])
        routed = routed + gates[:, e:e + 1] * (a @ w_exp_down[e])
    shared = (jax.nn.silu(hf @ w_sh_gate) * (hf @ w_sh_up)) @ w_sh_down
    return (routed + shared).reshape(b, s, d).astype(h.dtype)


def setup_inputs(seed: int = 0) -> dict:
    key = jax.random.key(seed)
    ks = jax.random.split(key, 24)
    f32 = jnp.float32
    L = DEPTH

    def nrm(k, shape, fan_in, mult=1.0):
        return jax.random.normal(k, shape, f32) * (mult * fan_in ** -0.5)

    def gain(k, shape):
        return 1.0 + 0.02 * jax.random.normal(k, shape, f32)

    x = jax.random.normal(ks[0], (BATCH, SEQ, D_MODEL), f32)
    c = jax.random.normal(ks[1], (BATCH, D_MODEL), f32)
    offs = jax.random.randint(ks[2], (BATCH, 1), 0, 1024, dtype=jnp.int32)
    positions = (jnp.arange(SEQ, dtype=jnp.int32)[None, :] + offs).astype(jnp.int32)
    return {
        "x": x,
        "c": c,
        "positions": positions,
        "w_ada": nrm(ks[3], (L, D_MODEL, 6 * D_MODEL), D_MODEL, 0.5),
        "b_ada": 0.02 * jax.random.normal(ks[4], (L, 6 * D_MODEL), f32),
        "norm1_g": gain(ks[5], (L, D_MODEL)),
        "w_in": nrm(ks[6], (L, D_MODEL, IN_W), D_MODEL),
        "q_a_norm_g": gain(ks[7], (L, MLA_Q_RANK)),
        "w_uq": nrm(ks[8], (L, MLA_Q_RANK, MLA_HEADS * MLA_QK), MLA_Q_RANK),
        "kv_a_norm_g": gain(ks[9], (L, MLA_KV_RANK)),
        "w_ukv": nrm(ks[10], (L, MLA_KV_RANK, MLA_HEADS * (MLA_NOPE + MLA_V)), MLA_KV_RANK),
        "q_norm_g": gain(ks[11], (L, MLA_QK)),
        "k_norm_g": gain(ks[12], (L, MLA_QK)),
        "w_o": nrm(ks[13], (L, MIX_W, D_MODEL), MIX_W),
        "norm2_g": gain(ks[14], (L, D_MODEL)),
        "w_router": nrm(ks[15], (L, D_MODEL, N_EXPERTS), D_MODEL),
        "router_bias": 0.01 * jax.random.normal(ks[16], (L, N_EXPERTS), f32),
        "w_exp_gate": nrm(ks[17], (L, N_EXPERTS, D_MODEL, EXPERT_FF), D_MODEL),
        "w_exp_up": nrm(ks[18], (L, N_EXPERTS, D_MODEL, EXPERT_FF), D_MODEL),
        "w_exp_down": nrm(ks[19], (L, N_EXPERTS, EXPERT_FF, D_MODEL), EXPERT_FF),
        "w_sh_gate": nrm(ks[20], (L, D_MODEL, SHARED_FF), D_MODEL),
        "w_sh_up": nrm(ks[21], (L, D_MODEL, SHARED_FF), D_MODEL),
        "w_sh_down": nrm(ks[22], (L, SHARED_FF, D_MODEL), SHARED_FF),
    }


def reference(x, c, positions, w_ada, b_ada, norm1_g, w_in, q_a_norm_g, w_uq, kv_a_norm_g,
              w_ukv, q_norm_g, k_norm_g, w_o, norm2_g, w_router, router_bias,
              w_exp_gate, w_exp_up, w_exp_down, w_sh_gate, w_sh_up, w_sh_down):
    b, s, _ = x.shape
    cos_r, sin_r = rope_tables(positions, RET_DK)
    cos_m, sin_m = rope_tables(positions, MLA_ROPE)
    c_act = jax.nn.silu(c)
    for l in range(DEPTH):
        mod = c_act @ w_ada[l] + b_ada[l]
        shift_a, scale_a, gate_a, shift_m, scale_m, gate_m = jnp.split(mod, 6, axis=-1)

        h = modulate(rms_norm(x, norm1_g[l]), shift_a, scale_a)
        proj = h @ w_in[l]
        rq, rk, rv, rg, cq, ckv, kr = jnp.split(proj, IN_SPLIT, axis=-1)
        rq = apply_rope(rq.reshape(b, s, RET_HEADS, RET_DK), cos_r, sin_r)
        rk = apply_rope(rk.reshape(b, s, RET_HEADS, RET_DK), cos_r, sin_r) * (RET_DK ** -0.5)
        rv = rv.reshape(b, s, RET_HEADS, RET_DV)
        ret = retention(rq, rk, rv)
        ret = _rms(ret).reshape(b, s, RET_W).astype(x.dtype) * jax.nn.silu(rg)
        att = mla(cq, ckv, kr, cos_m, sin_m, q_a_norm_g[l], w_uq[l], kv_a_norm_g[l],
                  w_ukv[l], q_norm_g[l], k_norm_g[l])
        mixed = jnp.concatenate([ret, att], axis=-1) @ w_o[l]
        x = x + gate_a[:, None, :] * mixed

        h = modulate(rms_norm(x, norm2_g[l]), shift_m, scale_m)
        y = moe(h, w_router[l], router_bias[l], w_exp_gate[l], w_exp_up[l], w_exp_down[l],
                w_sh_gate[l], w_sh_up[l], w_sh_down[l])
        x = x + gate_m[:, None, :] * y
    return x
```

```python
import functools
import math

import jax
import jax.numpy as jnp
from jax import lax
from jax.experimental import pallas as pl
from jax.experimental.pallas import tpu as pltpu

F32 = jnp.float32
BF16 = jnp.bfloat16

EPS = 1e-6
ROPE_BASE = 10000.0
RET_DK = 256
RET_DV = 256
MLA_NOPE = 128
MLA_ROPE = 64
MLA_V = 128
MLA_QK = MLA_NOPE + MLA_ROPE
N_GROUPS = 8
TOPK_GROUPS = 4
TOP_K = 8
ROUTED_SCALE = 2.5

LANES = 128
HEAD_PAD = 2 * LANES
VMEM_LIMIT = 56 * 1024 * 1024


def _tile(dim, pref):
    t = min(dim, pref)
    while dim % t:
        t //= 2
    return t


def _params(sem):
    return pltpu.CompilerParams(dimension_semantics=sem, vmem_limit_bytes=VMEM_LIMIT)


def _silu(v):
    return v * jax.nn.sigmoid(v)


def _dot(a, b):
    return jnp.dot(a, b, preferred_element_type=F32)


def _dot_nt(a, b):
    return lax.dot_general(a, b, (((1,), (1,)), ((), ())), preferred_element_type=F32)


def _dot_tn(a, b):
    return lax.dot_general(a, b, (((0,), (0,)), ((), ())), preferred_element_type=F32)


def _ada_kernel(c_ref, w_ref, b_ref, o_ref):
    ca = _silu(c_ref[...]).astype(BF16)
    o_ref[...] = _dot(ca, w_ref[...].astype(BF16)) + b_ref[...]


def _ada(c_pad, w, b):
    m, d = c_pad.shape
    n = w.shape[1]
    tn = _tile(n, 512)
    return pl.pallas_call(
        _ada_kernel,
        grid=(n // tn,),
        in_specs=[pl.BlockSpec((m, d), lambda j: (0, 0)),
                  pl.BlockSpec((d, tn), lambda j: (0, j)),
                  pl.BlockSpec((1, tn), lambda j: (0, j))],
        out_specs=pl.BlockSpec((m, tn), lambda j: (0, j)),
        out_shape=jax.ShapeDtypeStruct((m, n), F32),
        compiler_params=_params(("arbitrary",)),
    )(c_pad, w, b)


def _norm_mod(x, g, scale, shift):
    xf = x.astype(F32)
    h = xf * lax.rsqrt(jnp.mean(xf * xf, axis=-1, keepdims=True) + EPS) * g
    return h * (1.0 + scale) + shift


def _norm_mod_kernel(x_ref, g_ref, sc_ref, sh_ref, o_ref):
    o_ref[...] = _norm_mod(x_ref[...], g_ref[...], sc_ref[0], sh_ref[0]).astype(o_ref.dtype)


def _norm_mod_call(x2, g, scale, shift, batch, out_dtype):
    t, d = x2.shape
    s = t // batch
    ts = _tile(s, 512)
    ns = s // ts
    return pl.pallas_call(
        _norm_mod_kernel,
        grid=(batch, ns),
        in_specs=[pl.BlockSpec((ts, d), lambda b, i: (b * ns + i, 0)),
                  pl.BlockSpec((1, d), lambda b, i: (0, 0)),
                  pl.BlockSpec((1, 1, d), lambda b, i: (b, 0, 0)),
                  pl.BlockSpec((1, 1, d), lambda b, i: (b, 0, 0))],
        out_specs=pl.BlockSpec((ts, d), lambda b, i: (b * ns + i, 0)),
        out_shape=jax.ShapeDtypeStruct((t, d), out_dtype),
        compiler_params=_params(("arbitrary", "arbitrary")),
    )(x2, g, scale, shift)


def _mm_kernel(a_ref, w_ref, o_ref):
    o_ref[...] = _dot(a_ref[...], w_ref[...]).astype(o_ref.dtype)


def _mm(a, w, out_dtype, tm_pref=1024, tn_pref=512):
    m, k = a.shape
    n = w.shape[1]
    tm, tn = _tile(m, tm_pref), _tile(n, tn_pref)
    return pl.pallas_call(
        _mm_kernel,
        grid=(m // tm, n // tn),
        in_specs=[pl.BlockSpec((tm, k), lambda i, j: (i, 0)),
                  pl.BlockSpec((k, tn), lambda i, j: (0, j))],
        out_specs=pl.BlockSpec((tm, tn), lambda i, j: (i, j)),
        out_shape=jax.ShapeDtypeStruct((m, n), out_dtype),
        compiler_params=_params(("arbitrary", "arbitrary")),
    )(a, w)


def _ret_kernel(lg_ref, q_ref, k_ref, v_ref, g_ref, cos_ref, sin_ref, o_ref, state_ref):
    h = pl.program_id(1)

    @pl.when(pl.program_id(2) == 0)
    def _():
        state_ref[...] = jnp.zeros_like(state_ref)

    lg = lg_ref[h]
    c = q_ref.shape[0]
    cos, sin = cos_ref[...], sin_ref[...]
    half = RET_DK // 2

    def rope(v):
        v1, v2 = v[:, :half], v[:, half:]
        return jnp.concatenate([v1 * cos - v2 * sin, v2 * cos + v1 * sin], axis=-1)

    q = rope(q_ref[...].astype(F32))
    k = rope(k_ref[...].astype(F32)) * (RET_DK ** -0.5)
    v = v_ref[...]
    ii = lax.broadcasted_iota(jnp.int32, (c, c), 0)
    jj = lax.broadcasted_iota(jnp.int32, (c, c), 1)
    diff = (ii - jj).astype(F32)
    inner_decay = jnp.where(diff >= 0, jnp.exp(diff * lg), 0.0)
    ri = lax.broadcasted_iota(jnp.int32, (c, 1), 0).astype(F32)
    q_decay = jnp.exp((ri + 1.0) * lg)
    k_decay = jnp.exp((c - 1.0 - ri) * lg)
    chunk_decay = jnp.exp(jnp.full((1, 1), c, F32) * lg)

    qb = q.astype(BF16)
    sc = _dot_nt(qb, k.astype(BF16)) * inner_decay
    inner = _dot(sc.astype(BF16), v)
    state = state_ref[...]
    cross = _dot(qb, state.astype(BF16)) * q_decay
    kv = _dot_tn((k * k_decay).astype(BF16), v)
    state_ref[...] = state * chunk_decay + kv
    out = inner + cross
    r = out * lax.rsqrt(jnp.mean(out * out, axis=-1, keepdims=True) + EPS)
    o_ref[...] = (r * _silu(g_ref[...].astype(F32))).astype(o_ref.dtype)


def _retention(proj, cos_r, sin_r, log_g, batch, n_heads, col0):
    t = proj.shape[0]
    s = t // batch
    c = _tile(s, 256)
    n = s // c
    hh = n_heads

    def blk(off):
        return pl.BlockSpec((c, RET_DK), lambda b, h, i, lg: (b * n + i, col0 + off + h))

    tab = pl.BlockSpec((c, RET_DK // 2), lambda b, h, i, lg: (b * n + i, 0))
    return pl.pallas_call(
        _ret_kernel,
        grid_spec=pltpu.PrefetchScalarGridSpec(
            num_scalar_prefetch=1,
            grid=(batch, hh, n),
            in_specs=[blk(0), blk(hh), blk(2 * hh), blk(3 * hh), tab, tab],
            out_specs=pl.BlockSpec((c, RET_DV), lambda b, h, i, lg: (b * n + i, h)),
            scratch_shapes=[pltpu.VMEM((RET_DK, RET_DV), F32)]),
        out_shape=jax.ShapeDtypeStruct((t, hh * RET_DV), BF16),
        compiler_params=_params(("arbitrary", "arbitrary", "arbitrary")),
    )(log_g, proj, proj, proj, proj, cos_r, sin_r)


def _mla_prep_kernel(p_ref, ct_ref, sa_ref, sb_ref, gqa_ref, gkva_ref, gq_ref, gk_ref,
                     wq_ref, wk_ref, wv_ref, q_ref, k_ref, v_ref, *, qr, kvr, nh):
    p = p_ref[...].astype(F32)
    cq = p[:, :qr]
    krp = p[:, qr:qr + LANES]
    ckv = p[:, qr + LANES:qr + LANES + kvr]

    def rms(v, n):
        return v * lax.rsqrt(jnp.sum(v * v, axis=-1, keepdims=True) * (1.0 / n) + EPS)

    cqn = (rms(cq, qr) * gqa_ref[...]).astype(BF16)
    ckvn = (rms(ckv, kvr) * gkva_ref[...]).astype(BF16)
    qf = _dot(cqn, wq_ref[...])
    kn = _dot(ckvn, wk_ref[...])
    v_ref[...] = _dot(ckvn, wv_ref[...]).astype(v_ref.dtype)

    ct, sa, sb = ct_ref[...], sa_ref[...], sb_ref[...]

    def rope(v):
        return v * ct + pltpu.roll(v, LANES - MLA_ROPE // 2, 1) * sa + pltpu.roll(v, MLA_ROPE // 2, 1) * sb

    gq, gk = gq_ref[...], gk_ref[...]
    kr_rot = rope(krp * gk[:, LANES:])
    kr_ss = jnp.sum(krp * krp, axis=-1, keepdims=True)
    qscale = 1.0 / math.sqrt(MLA_QK)
    for h in range(nh):
        qh = qf[:, h * HEAD_PAD:(h + 1) * HEAD_PAD]
        rq = lax.rsqrt(jnp.sum(qh * qh, axis=-1, keepdims=True) * (1.0 / MLA_QK) + EPS) * qscale
        qn = qh * rq * gq
        q_ref[:, h * HEAD_PAD:h * HEAD_PAD + LANES] = qn[:, :LANES].astype(q_ref.dtype)
        q_ref[:, h * HEAD_PAD + LANES:(h + 1) * HEAD_PAD] = rope(qn[:, LANES:]).astype(q_ref.dtype)
        knh = kn[:, h * MLA_NOPE:(h + 1) * MLA_NOPE]
        rk = lax.rsqrt((jnp.sum(knh * knh, axis=-1, keepdims=True) + kr_ss) * (1.0 / MLA_QK) + EPS)
        k_ref[:, h * HEAD_PAD:h * HEAD_PAD + LANES] = (knh * rk * gk[:, :LANES]).astype(k_ref.dtype)
        k_ref[:, h * HEAD_PAD + LANES:(h + 1) * HEAD_PAD] = (kr_rot * rk).astype(k_ref.dtype)


def _mla_prep(proj, wm, tabs, gqa, gkva, gq, gk, wq, wk, wv, qr, kvr, nh):
    t = proj.shape[0]
    tm = _tile(t, 256)
    full = lambda a: pl.BlockSpec(a.shape, lambda i: (0, 0))
    tab = pl.BlockSpec((tm, LANES), lambda i: (i, 0))
    return pl.pallas_call(
        functools.partial(_mla_prep_kernel, qr=qr, kvr=kvr, nh=nh),
        grid=(t // tm,),
        in_specs=[pl.BlockSpec((tm, wm), lambda i: (i, 0)), tab, tab, tab,
                  full(gqa), full(gkva), full(gq), full(gk), full(wq), full(wk), full(wv)],
        out_specs=[pl.BlockSpec((tm, nh * HEAD_PAD), lambda i: (i, 0)),
                   pl.BlockSpec((tm, nh * HEAD_PAD), lambda i: (i, 0)),
                   pl.BlockSpec((tm, nh * MLA_V), lambda i: (i, 0))],
        out_shape=[jax.ShapeDtypeStruct((t, nh * HEAD_PAD), BF16),
                   jax.ShapeDtypeStruct((t, nh * HEAD_PAD), BF16),
                   jax.ShapeDtypeStruct((t, nh * MLA_V), BF16)],
        compiler_params=_params(("arbitrary",)),
    )(proj, *tabs, gqa, gkva, gq, gk, wq, wk, wv)


def _flash_kernel(qi_ref, ki_ref, q_ref, k_ref, v_ref, o_ref, m_ref, l_ref, acc_ref):
    p = pl.program_id(2)
    qi, ki = qi_ref[p], ki_ref[p]

    @pl.when(ki == 0)
    def _():
        m_ref[...] = jnp.full_like(m_ref, -jnp.inf)
        l_ref[...] = jnp.zeros_like(l_ref)
        acc_ref[...] = jnp.zeros_like(acc_ref)

    def update(masked):
        s = _dot_nt(q_ref[...], k_ref[...])
        if masked:
            row = lax.broadcasted_iota(jnp.int32, s.shape, 0)
            col = lax.broadcasted_iota(jnp.int32, s.shape, 1)
            s = jnp.where(col <= row, s, -jnp.inf)
        m_prev = m_ref[...]
        m_new = jnp.maximum(m_prev, jnp.max(s, axis=-1, keepdims=True))
        alpha = jnp.exp(m_prev - m_new)
        pe = jnp.exp(s - m_new)
        l_ref[...] = alpha * l_ref[...] + jnp.sum(pe, axis=-1, keepdims=True)
        acc_ref[...] = alpha * acc_ref[...] + _dot(pe.astype(BF16), v_ref[...])
        m_ref[...] = m_new

    @pl.when(ki < qi)
    def _():
        update(False)

    @pl.when(ki == qi)
    def _():
        update(True)
        o_ref[...] = (acc_ref[...] / l_ref[...]).astype(o_ref.dtype)


def _flash(q, k, v, batch, nh):
    t = q.shape[0]
    s = t // batch
    blk = _tile(s, 1024)
    nq = s // blk
    pairs = [(a, b) for a in range(nq) for b in range(a + 1)]
    qi = jnp.asarray([a for a, _ in pairs], jnp.int32)
    ki = jnp.asarray([b for _, b in pairs], jnp.int32)
    return pl.pallas_call(
        _flash_kernel,
        grid_spec=pltpu.PrefetchScalarGridSpec(
            num_scalar_prefetch=2,
            grid=(batch, nh, len(pairs)),
            in_specs=[pl.BlockSpec((blk, HEAD_PAD), lambda b, h, p, qi, ki: (b * nq + qi[p], h)),
                      pl.BlockSpec((blk, HEAD_PAD), lambda b, h, p, qi, ki: (b * nq + ki[p], h)),
                      pl.BlockSpec((blk, MLA_V), lambda b, h, p, qi, ki: (b * nq + ki[p], h))],
            out_specs=pl.BlockSpec((blk, MLA_V), lambda b, h, p, qi, ki: (b * nq + qi[p], h)),
            scratch_shapes=[pltpu.VMEM((blk, 1), F32), pltpu.VMEM((blk, 1), F32),
                            pltpu.VMEM((blk, MLA_V), F32)]),
        out_shape=jax.ShapeDtypeStruct((t, nh * MLA_V), BF16),
        compiler_params=_params(("arbitrary", "arbitrary", "arbitrary")),
    )(qi, ki, q, k, v)


def _wo_kernel(ret_ref, att_ref, w1_ref, w2_ref, x_ref, gate_ref, o_ref):
    mix = _dot(ret_ref[...], w1_ref[...]) + _dot(att_ref[...], w2_ref[...])
    o_ref[...] = x_ref[...] + gate_ref[0] * mix


def _wo(ret, att, w_o, x2, gate, batch):
    t, wr = ret.shape
    wa = att.shape[1]
    assert wr == wa
    d = w_o.shape[1]
    s = t // batch
    tm = _tile(s, 1024)
    tn = _tile(d, 512)
    ns = s // tm
    return pl.pallas_call(
        _wo_kernel,
        grid=(t // tm, d // tn),
        in_specs=[pl.BlockSpec((tm, wr), lambda i, j: (i, 0)),
                  pl.BlockSpec((tm, wa), lambda i, j: (i, 0)),
                  pl.BlockSpec((wr, tn), lambda i, j: (0, j)),
                  pl.BlockSpec((wa, tn), lambda i, j: (1, j)),
                  pl.BlockSpec((tm, tn), lambda i, j: (i, j)),
                  pl.BlockSpec((1, 1, tn), lambda i, j: (i // ns, 0, j))],
        out_specs=pl.BlockSpec((tm, tn), lambda i, j: (i, j)),
        out_shape=jax.ShapeDtypeStruct((t, d), F32),
        compiler_params=_params(("arbitrary", "arbitrary")),
    )(ret, att, w_o, w_o, x2, gate)


def _router_kernel(x_ref, g_ref, sc_ref, sh_ref, wrh_ref, wrl_ref, bias_ref, wsg_ref, wsu_ref,
                   h_ref, a_ref, e_ref, w_ref):
    h = _norm_mod(x_ref[...], g_ref[...], sc_ref[0], sh_ref[0])
    h_ref[...] = h
    hb = h.astype(BF16)
    a_ref[...] = (_silu(_dot(hb, wsg_ref[...])) * _dot(hb, wsu_ref[...])).astype(a_ref.dtype)

    hl = (h - hb.astype(F32)).astype(BF16)
    wrh = wrh_ref[...]
    logits = _dot_nt(wrh, hb) + _dot_nt(wrh, hl) + _dot_nt(wrl_ref[...], hb)
    scores = jax.nn.sigmoid(logits)
    sel = scores + bias_ref[...]
    ne, tm = sel.shape
    per_group = ne // N_GROUPS
    neg = -jnp.inf

    def first_argmax(v, iota, size):
        m = jnp.max(v, axis=0, keepdims=True)
        idx = jnp.min(jnp.where(v == m, iota, size), axis=0, keepdims=True)
        return m, idx

    iota_pg = lax.broadcasted_iota(jnp.int32, (per_group, tm), 0)
    gs = []
    for g in range(N_GROUPS):
        sg = sel[g * per_group:(g + 1) * per_group, :]
        m1, i1 = first_argmax(sg, iota_pg, per_group)
        m2 = jnp.max(jnp.where(iota_pg == i1, neg, sg), axis=0, keepdims=True)
        gs.append(m1 + m2)
    gsc = jnp.concatenate(gs, axis=0)
    iota_g = lax.broadcasted_iota(jnp.int32, (N_GROUPS, tm), 0)
    gmask = jnp.zeros((N_GROUPS, tm), jnp.bool_)
    for _ in range(TOPK_GROUPS):
        _, ig = first_argmax(gsc, iota_g, N_GROUPS)
        hit = iota_g == ig
        gmask = jnp.logical_or(gmask, hit)
        gsc = jnp.where(hit, neg, gsc)
    emask = jnp.concatenate(
        [jnp.broadcast_to(gmask[g:g + 1, :], (per_group, tm)) for g in range(N_GROUPS)], axis=0)
    cand = jnp.where(emask, sel, neg)
    iota_e = lax.broadcasted_iota(jnp.int32, (ne, tm), 0)
    es, ws = [], []
    for _ in range(TOP_K):
        _, ie = first_argmax(cand, iota_e, ne)
        hit = iota_e == ie
        es.append(ie)
        ws.append(jnp.sum(jnp.where(hit, scores, 0.0), axis=0, keepdims=True))
        cand = jnp.where(hit, neg, cand)
    wsum = ws[0]
    for wk in ws[1:]:
        wsum = wsum + wk
    e_ref[...] = jnp.concatenate(es, axis=0)
    w_ref[...] = jnp.concatenate(ws, axis=0) / wsum * ROUTED_SCALE


def _router(x2, g, scale, shift, wrh, wrl, bias, wsg, wsu, batch):
    t, d = x2.shape
    s = t // batch
    tm = _tile(s, 256)
    ns = s // tm
    ne = wrh.shape[0]
    f = wsg.shape[1]
    full = lambda a: pl.BlockSpec(a.shape, lambda i: (0,) * a.ndim)
    bat = pl.BlockSpec((1, 1, d), lambda i: (i // ns, 0, 0))
    return pl.pallas_call(
        _router_kernel,
        grid=(t // tm,),
        in_specs=[pl.BlockSpec((tm, d), lambda i: (i, 0)), full(g), bat, bat,
                  full(wrh), full(wrl), full(bias), full(wsg), full(wsu)],
        out_specs=[pl.BlockSpec((tm, d), lambda i: (i, 0)),
                   pl.BlockSpec((tm, f), lambda i: (i, 0)),
                   pl.BlockSpec((TOP_K, tm), lambda i: (0, i)),
                   pl.BlockSpec((TOP_K, tm), lambda i: (0, i))],
        out_shape=[jax.ShapeDtypeStruct((t, d), F32),
                   jax.ShapeDtypeStruct((t, f), BF16),
                   jax.ShapeDtypeStruct((TOP_K, t), jnp.int32),
                   jax.ShapeDtypeStruct((TOP_K, t), F32)],
        compiler_params=_params(("arbitrary",)),
    )(x2, g, scale, shift, wrh, wrl, bias, wsg, wsu)


def _moe_kernel(order_ref, te_ref, tr_ref, tn_ref, h_hbm, wg_ref, wu_ref, wd_ref, y_hbm,
                xbuf, ybuf, gsem, ssem, *, n_tok):
    g = pl.program_id(0)
    nv = tn_ref[g]
    row0 = tr_ref[g]

    @pl.when(g == 0)
    def _():
        xbuf[...] = jnp.zeros_like(xbuf)

    def row_gather(r):
        tok = order_ref[row0 + r] & (n_tok - 1)
        return pltpu.make_async_copy(h_hbm.at[pl.ds(tok, 1), :], xbuf.at[pl.ds(r, 1), :], gsem)

    def row_scatter(r):
        dst = order_ref[row0 + r]
        return pltpu.make_async_copy(ybuf.at[pl.ds(r, 1), :], y_hbm.at[pl.ds(dst, 1), :], ssem)

    @pl.when(nv > 0)
    def _():
        def g_start(r, carry):
            row_gather(r).start()
            return carry

        def g_wait(r, carry):
            row_gather(r).wait()
            return carry

        lax.fori_loop(0, nv, g_start, 0)
        lax.fori_loop(0, nv, g_wait, 0)
        xb = xbuf[...].astype(BF16)
        act = _silu(_dot(xb, wg_ref[...])) * _dot(xb, wu_ref[...])
        ybuf[...] = _dot(act.astype(BF16), wd_ref[...])

        def s_start(r, carry):
            row_scatter(r).start()
            return carry

        def s_wait(r, carry):
            row_scatter(r).wait()
            return carry

        lax.fori_loop(0, nv, s_start, 0)
        lax.fori_loop(0, nv, s_wait, 0)


def _moe(order, tile_e, tile_row0, tile_nv, h2, wg, wu, wd, tm):
    t, d = h2.shape
    assert t & (t - 1) == 0, "token count must be a power of two (slot index packing)"
    ne, _, f = wg.shape
    n_tiles = tile_e.shape[0]
    return pl.pallas_call(
        functools.partial(_moe_kernel, n_tok=t),
        grid_spec=pltpu.PrefetchScalarGridSpec(
            num_scalar_prefetch=4,
            grid=(n_tiles,),
            in_specs=[pl.BlockSpec(memory_space=pl.ANY),
                      pl.BlockSpec((None, d, f), lambda g, o, te, tr, tn: (te[g], 0, 0)),
                      pl.BlockSpec((None, d, f), lambda g, o, te, tr, tn: (te[g], 0, 0)),
                      pl.BlockSpec((None, f, d), lambda g, o, te, tr, tn: (te[g], 0, 0))],
            out_specs=pl.BlockSpec(memory_space=pl.ANY),
            scratch_shapes=[pltpu.VMEM((tm, d), F32), pltpu.VMEM((tm, d), F32),
                            pltpu.SemaphoreType.DMA, pltpu.SemaphoreType.DMA]),
        out_shape=jax.ShapeDtypeStruct((TOP_K * t, d), F32),
        compiler_params=_params(("arbitrary",)),
    )(order, tile_e, tile_row0, tile_nv, h2, wg, wu, wd)


def _moe_schedule(e_idx, ne, tm):
    k, t = e_idx.shape
    p = k * t
    keys = e_idx.reshape(p)
    skeys, order = lax.sort((keys, jnp.arange(p, dtype=jnp.int32)), num_keys=1, is_stable=True)
    starts = jnp.searchsorted(skeys, jnp.arange(ne + 1, dtype=jnp.int32), side="left").astype(jnp.int32)
    counts = starts[1:] - starts[:-1]
    ntile = (counts + tm - 1) // tm
    cum = jnp.cumsum(ntile)
    n_tiles = p // tm + ne
    gidx = jnp.arange(n_tiles, dtype=jnp.int32)
    te = jnp.minimum(jnp.searchsorted(cum, gidx, side="right").astype(jnp.int32), ne - 1)
    local = gidx - (cum[te] - ntile[te])
    row0 = starts[te] + local * tm
    nv = jnp.where(gidx < cum[-1], jnp.clip(counts[te] - local * tm, 0, tm), 0)
    row0 = jnp.where(nv > 0, row0, 0)
    return order, te, row0.astype(jnp.int32), nv.astype(jnp.int32)


def _combine_kernel(y_ref, w_ref, a_ref, wsd_ref, x_ref, gate_ref, o_ref):
    acc = _dot(a_ref[...], wsd_ref[...])
    w = w_ref[...]
    for k in range(y_ref.shape[0]):
        acc = acc + w[:, k:k + 1] * y_ref[k]
    o_ref[...] = x_ref[...] + gate_ref[0] * acc


def _combine(y3, w_tk, a_sh, wsd, x2, gate, batch):
    kk, t, d = y3.shape
    f = a_sh.shape[1]
    s = t // batch
    tm = _tile(s, 256)
    tn = _tile(d, 1024)
    ns = s // tm
    return pl.pallas_call(
        _combine_kernel,
        grid=(t // tm, d // tn),
        in_specs=[pl.BlockSpec((kk, tm, tn), lambda i, j: (0, i, j)),
                  pl.BlockSpec((tm, kk), lambda i, j: (i, 0)),
                  pl.BlockSpec((tm, f), lambda i, j: (i, 0)),
                  pl.BlockSpec((f, tn), lambda i, j: (0, j)),
                  pl.BlockSpec((tm, tn), lambda i, j: (i, j)),
                  pl.BlockSpec((1, 1, tn), lambda i, j: (i // ns, 0, j))],
        out_specs=pl.BlockSpec((tm, tn), lambda i, j: (i, j)),
        out_shape=jax.ShapeDtypeStruct((t, d), F32),
        compiler_params=_params(("arbitrary", "arbitrary")),
    )(y3, w_tk, a_sh, wsd, x2, gate)


def _round_up(v, m):
    return (v + m - 1) // m * m


def kernel(x, c, positions, w_ada, b_ada, norm1_g, w_in, q_a_norm_g, w_uq, kv_a_norm_g, w_ukv, q_norm_g, k_norm_g, w_o, norm2_g, w_router, router_bias, w_exp_gate, w_exp_up, w_exp_down, w_sh_gate, w_sh_up, w_sh_down):
    b, s, d = x.shape
    t = b * s
    depth = w_ada.shape[0]
    qr = q_a_norm_g.shape[1]
    kvr = kv_a_norm_g.shape[1]
    nh = w_uq.shape[2] // MLA_QK
    in_w = w_in.shape[2]
    ret_heads = (in_w - qr - kvr - MLA_ROPE) // (2 * RET_DK + 2 * RET_DV)
    ret_w = ret_heads * RET_DK
    ne = w_router.shape[2]
    moe_tm = 256

    pos = positions.reshape(t).astype(F32)
    inv_r = ROPE_BASE ** (-jnp.arange(0, RET_DK, 2, dtype=F32) / RET_DK)
    ang_r = pos[:, None] * inv_r
    cos_r, sin_r = jnp.cos(ang_r), jnp.sin(ang_r)
    inv_m = ROPE_BASE ** (-jnp.arange(0, MLA_ROPE, 2, dtype=F32) / MLA_ROPE)
    ang_m = pos[:, None] * inv_m
    cm, sm = jnp.cos(ang_m), jnp.sin(ang_m)
    z = jnp.zeros_like(cm)
    mla_tabs = (jnp.concatenate([cm, cm, z, z], axis=-1),
                jnp.concatenate([-sm, z, z, z], axis=-1),
                jnp.concatenate([z, sm, z, z], axis=-1))
    log_g = jnp.log(1.0 - 2.0 ** (-5.0 - jnp.arange(ret_heads, dtype=F32)))

    c_pad = jnp.zeros((8, d), F32).at[:b].set(c)
    x2 = x.reshape(t, d)
    wm = _round_up(qr + LANES + kvr, HEAD_PAD)
    np_ = _round_up(wm + 4 * ret_w, 512)
    split = [2 * ret_w // 2, 2 * ret_w, 3 * ret_w, 4 * ret_w, 4 * ret_w + qr, 4 * ret_w + qr + kvr]

    for l in range(depth):
        mod = _ada(c_pad, w_ada[l], b_ada[l][None, :])[:b]
        shift_a, scale_a, gate_a, shift_m, scale_m, gate_m = (
            m.reshape(b, 1, d) for m in jnp.split(mod, 6, axis=-1))

        wi = w_in[l]
        w_ret, w_cq = wi[:, :split[3]], wi[:, split[3]:split[4]]
        w_ckv, w_kr = wi[:, split[4]:split[5]], wi[:, split[5]:]
        w_in_p = jnp.concatenate(
            [w_cq, w_kr, jnp.zeros((d, LANES - MLA_ROPE), F32), w_ckv,
             jnp.zeros((d, wm - qr - LANES - kvr), F32), w_ret,
             jnp.zeros((d, np_ - wm - 4 * ret_w), F32)], axis=-1).astype(BF16)

        h1 = _norm_mod_call(x2, norm1_g[l][None, :], scale_a, shift_a, b, BF16)
        proj = _mm(h1, w_in_p, BF16)

        ret = _retention(proj, cos_r, sin_r, log_g, b, ret_heads, wm // RET_DK)

        wq = jnp.pad(w_uq[l].reshape(qr, nh, MLA_QK), ((0, 0), (0, 0), (0, HEAD_PAD - MLA_QK)))
        wq = wq.reshape(qr, nh * HEAD_PAD).astype(BF16)
        wkv = w_ukv[l].reshape(kvr, nh, MLA_NOPE + MLA_V)
        wk = wkv[:, :, :MLA_NOPE].reshape(kvr, nh * MLA_NOPE).astype(BF16)
        wv = wkv[:, :, MLA_NOPE:].reshape(kvr, nh * MLA_V).astype(BF16)
        gq = jnp.pad(q_norm_g[l], (0, HEAD_PAD - MLA_QK))[None, :]
        gk = jnp.pad(k_norm_g[l], (0, HEAD_PAD - MLA_QK))[None, :]
        q, k, v = _mla_prep(proj, wm, mla_tabs, q_a_norm_g[l][None, :], kv_a_norm_g[l][None, :],
                            gq, gk, wq, wk, wv, qr, kvr, nh)
        att = _flash(q, k, v, b, nh)

        x2 = _wo(ret, att, w_o[l].astype(BF16), x2, gate_a, b)

        wr_t = w_router[l].T
        wrh = wr_t.astype(BF16)
        wrl = (wr_t - wrh.astype(F32)).astype(BF16)
        h2, a_sh, e_idx, w_kt = _router(x2, norm2_g[l][None, :], scale_m, shift_m, wrh, wrl,
                                        router_bias[l][:, None], w_sh_gate[l].astype(BF16),
                                        w_sh_up[l].astype(BF16), b)
        order, tile_e, tile_row0, tile_nv = _moe_schedule(e_idx, ne, moe_tm)
        y = _moe(order, tile_e, tile_row0, tile_nv, h2, w_exp_gate[l].astype(BF16),
                 w_exp_up[l].astype(BF16), w_exp_down[l].astype(BF16), moe_tm)
        x2 = _combine(y.reshape(TOP_K, t, d), w_kt.T, a_sh, w_sh_down[l].astype(BF16), x2, gate_m, b)

    return x2.reshape(b, s, d)
```

```python
import functools
import math

import jax
import jax.numpy as jnp
from jax import lax
from jax.experimental import pallas as pl
from jax.experimental.pallas import tpu as pltpu

F32 = jnp.float32
BF16 = jnp.bfloat16

EPS = 1e-6
ROPE_BASE = 10000.0
RET_DK = 256
RET_DV = 256
MLA_NOPE = 128
MLA_ROPE = 64
MLA_V = 128
MLA_QK = MLA_NOPE + MLA_ROPE
N_GROUPS = 8
TOPK_GROUPS = 4
TOP_K = 8
ROUTED_SCALE = 2.5

LANES = 128
SUBLANES = 8
HEAD_PAD = 2 * LANES
VMEM_LIMIT = 56 * 1024 * 1024
ROW_UNROLL = 8


def _tile(dim, pref):
    t = min(dim, pref)
    while dim % t:
        t //= 2
    return t


def _params(sem):
    return pltpu.CompilerParams(dimension_semantics=sem, vmem_limit_bytes=VMEM_LIMIT)


def _silu(v):
    return v * jax.nn.sigmoid(v)


def _dot(a, b):
    return jnp.dot(a, b, preferred_element_type=F32)


def _dot_nt(a, b):
    return lax.dot_general(a, b, (((1,), (1,)), ((), ())), preferred_element_type=F32)


def _dot_tn(a, b):
    return lax.dot_general(a, b, (((0,), (0,)), ((), ())), preferred_element_type=F32)


HI_HALF = 0xFFFF0000


def _pack_pairs(v):
    half = v.shape[1] // 2
    bits = lax.bitcast_convert_type(v, jnp.uint32)
    return (bits[:, :half] >> 16) | (bits[:, half:] & jnp.uint32(HI_HALF))


def _unpack_pairs(w):
    lo = lax.bitcast_convert_type(w << 16, F32)
    hi = lax.bitcast_convert_type(w & jnp.uint32(HI_HALF), F32)
    return lo, hi


def _round_bf16(v):
    return v.astype(BF16).astype(F32)


def _ada_kernel(c_ref, w_ref, b_ref, o_ref):
    ca = _silu(c_ref[...]).astype(BF16)
    o_ref[...] = _dot(ca, w_ref[...].astype(BF16)) + b_ref[...]


def _ada(c_pad, w, b):
    m, d = c_pad.shape
    n = w.shape[1]
    tn = _tile(n, 512)
    return pl.pallas_call(
        _ada_kernel,
        grid=(n // tn,),
        in_specs=[pl.BlockSpec((m, d), lambda j: (0, 0)),
                  pl.BlockSpec((d, tn), lambda j: (0, j)),
                  pl.BlockSpec((1, tn), lambda j: (0, j))],
        out_specs=pl.BlockSpec((m, tn), lambda j: (0, j)),
        out_shape=jax.ShapeDtypeStruct((m, n), F32),
        compiler_params=_params(("arbitrary",)),
    )(c_pad, w, b)


def _norm_mod(x, g, scale, shift):
    xf = x.astype(F32)
    h = xf * lax.rsqrt(jnp.mean(xf * xf, axis=-1, keepdims=True) + EPS) * g
    return h * (1.0 + scale) + shift


def _norm_mod_kernel(x_ref, g_ref, sc_ref, sh_ref, o_ref):
    o_ref[...] = _norm_mod(x_ref[...], g_ref[...], sc_ref[0], sh_ref[0]).astype(o_ref.dtype)


def _norm_mod_call(x2, g, scale, shift, batch, out_dtype):
    t, d = x2.shape
    s = t // batch
    ts = _tile(s, 512)
    ns = s // ts
    return pl.pallas_call(
        _norm_mod_kernel,
        grid=(batch, ns),
        in_specs=[pl.BlockSpec((ts, d), lambda b, i: (b * ns + i, 0)),
                  pl.BlockSpec((1, d), lambda b, i: (0, 0)),
                  pl.BlockSpec((1, 1, d), lambda b, i: (b, 0, 0)),
                  pl.BlockSpec((1, 1, d), lambda b, i: (b, 0, 0))],
        out_specs=pl.BlockSpec((ts, d), lambda b, i: (b * ns + i, 0)),
        out_shape=jax.ShapeDtypeStruct((t, d), out_dtype),
        compiler_params=_params(("arbitrary", "arbitrary")),
    )(x2, g, scale, shift)


def _mm_kernel(a_ref, w_ref, o_ref):
    o_ref[...] = _dot(a_ref[...], w_ref[...]).astype(o_ref.dtype)


def _mm(a, w, out_dtype, tm_pref=1024, tn_pref=512):
    m, k = a.shape
    n = w.shape[1]
    tm, tn = _tile(m, tm_pref), _tile(n, tn_pref)
    return pl.pallas_call(
        _mm_kernel,
        grid=(m // tm, n // tn),
        in_specs=[pl.BlockSpec((tm, k), lambda i, j: (i, 0)),
                  pl.BlockSpec((k, tn), lambda i, j: (0, j))],
        out_specs=pl.BlockSpec((tm, tn), lambda i, j: (i, j)),
        out_shape=jax.ShapeDtypeStruct((m, n), out_dtype),
        compiler_params=_params(("arbitrary", "arbitrary")),
    )(a, w)


def _mm_wcast_kernel(a_ref, w_ref, o_ref, wb_ref):
    @pl.when(pl.program_id(1) == 0)
    def _():
        wb_ref[...] = w_ref[...].astype(wb_ref.dtype)

    o_ref[...] = _dot(a_ref[...], wb_ref[...]).astype(o_ref.dtype)


def _mm_wcast(a, w3, layer, n, out_dtype, tm_pref=1024, tn_pref=512):
    m, k = a.shape
    tm, tn = _tile(m, tm_pref), _tile(n, tn_pref)
    return pl.pallas_call(
        _mm_wcast_kernel,
        grid=(n // tn, m // tm),
        in_specs=[pl.BlockSpec((tm, k), lambda j, i: (i, 0)),
                  pl.BlockSpec((None, k, tn), lambda j, i: (layer, 0, j))],
        out_specs=pl.BlockSpec((tm, tn), lambda j, i: (i, j)),
        out_shape=jax.ShapeDtypeStruct((m, n), out_dtype),
        scratch_shapes=[pltpu.VMEM((k, tn), BF16)],
        compiler_params=_params(("arbitrary", "arbitrary")),
    )(a, w3)


def _ret_kernel(lg_ref, q_ref, k_ref, v_ref, g_ref, cos_ref, sin_ref, o_ref, state_ref):
    h = pl.program_id(1)

    @pl.when(pl.program_id(2) == 0)
    def _():
        state_ref[...] = jnp.zeros_like(state_ref)

    lg = lg_ref[h]
    c = q_ref.shape[0]
    cos, sin = cos_ref[...], sin_ref[...]
    half = RET_DK // 2

    def rope(v):
        v1, v2 = v[:, :half], v[:, half:]
        return jnp.concatenate([v1 * cos - v2 * sin, v2 * cos + v1 * sin], axis=-1)

    q = rope(q_ref[...].astype(F32))
    k = rope(k_ref[...].astype(F32)) * (RET_DK ** -0.5)
    v = v_ref[...]
    ii = lax.broadcasted_iota(jnp.int32, (c, c), 0)
    jj = lax.broadcasted_iota(jnp.int32, (c, c), 1)
    diff = (ii - jj).astype(F32)
    inner_decay = jnp.where(diff >= 0, jnp.exp(diff * lg), 0.0)
    ri = lax.broadcasted_iota(jnp.int32, (c, 1), 0).astype(F32)
    q_decay = jnp.exp((ri + 1.0) * lg)
    k_decay = jnp.exp((c - 1.0 - ri) * lg)
    chunk_decay = jnp.exp(jnp.full((1, 1), c, F32) * lg)

    qb = q.astype(BF16)
    sc = _dot_nt(qb, k.astype(BF16)) * inner_decay
    inner = _dot(sc.astype(BF16), v)
    state = state_ref[...]
    cross = _dot(qb, state.astype(BF16)) * q_decay
    kv = _dot_tn((k * k_decay).astype(BF16), v)
    state_ref[...] = state * chunk_decay + kv
    out = inner + cross
    r = out * lax.rsqrt(jnp.mean(out * out, axis=-1, keepdims=True) + EPS)
    o_ref[...] = (r * _silu(g_ref[...].astype(F32))).astype(o_ref.dtype)


def _retention(proj, cos_r, sin_r, log_g, batch, n_heads, col0):
    t = proj.shape[0]
    s = t // batch
    c = _tile(s, 256)
    n = s // c
    hh = n_heads

    def blk(off):
        return pl.BlockSpec((c, RET_DK), lambda b, h, i, lg: (b * n + i, col0 + off + h))

    tab = pl.BlockSpec((c, RET_DK // 2), lambda b, h, i, lg: (b * n + i, 0))
    return pl.pallas_call(
        _ret_kernel,
        grid_spec=pltpu.PrefetchScalarGridSpec(
            num_scalar_prefetch=1,
            grid=(batch, hh, n),
            in_specs=[blk(0), blk(hh), blk(2 * hh), blk(3 * hh), tab, tab],
            out_specs=pl.BlockSpec((c, RET_DV), lambda b, h, i, lg: (b * n + i, h)),
            scratch_shapes=[pltpu.VMEM((RET_DK, RET_DV), F32)]),
        out_shape=jax.ShapeDtypeStruct((t, hh * RET_DV), BF16),
        compiler_params=_params(("arbitrary", "arbitrary", "arbitrary")),
    )(log_g, proj, proj, proj, proj, cos_r, sin_r)


def _mla_prep_kernel(p_ref, ct_ref, sa_ref, sb_ref, gqa_ref, gkva_ref, gq_ref, gk_ref,
                     wq_ref, wk_ref, wv_ref, q_ref, k_ref, v_ref, *, qr, kvr, nh):
    p = p_ref[...].astype(F32)
    cq = p[:, :qr]
    krp = p[:, qr:qr + LANES]
    ckv = p[:, qr + LANES:qr + LANES + kvr]

    def rms(v, n):
        return v * lax.rsqrt(jnp.sum(v * v, axis=-1, keepdims=True) * (1.0 / n) + EPS)

    cqn = (rms(cq, qr) * gqa_ref[...]).astype(BF16)
    ckvn = (rms(ckv, kvr) * gkva_ref[...]).astype(BF16)
    qf = _dot(cqn, wq_ref[...])
    kn = _dot(ckvn, wk_ref[...])
    vf = _dot(ckvn, wv_ref[...])
    ones_col = (lax.broadcasted_iota(jnp.int32, (p.shape[0], LANES), 1) == 0).astype(v_ref.dtype)
    for h in range(nh):
        v_ref[:, h * HEAD_PAD:h * HEAD_PAD + MLA_V] = vf[:, h * MLA_V:(h + 1) * MLA_V].astype(v_ref.dtype)
        v_ref[:, h * HEAD_PAD + MLA_V:(h + 1) * HEAD_PAD] = ones_col

    ct, sa, sb = ct_ref[...], sa_ref[...], sb_ref[...]

    def rope(v):
        return v * ct + pltpu.roll(v, LANES - MLA_ROPE // 2, 1) * sa + pltpu.roll(v, MLA_ROPE // 2, 1) * sb

    gq, gk = gq_ref[...], gk_ref[...]
    kr_rot = rope(krp * gk[:, LANES:])
    kr_ss = jnp.sum(krp * krp, axis=-1, keepdims=True)
    qscale = 1.0 / math.sqrt(MLA_QK)
    for h in range(nh):
        qh = qf[:, h * HEAD_PAD:(h + 1) * HEAD_PAD]
        rq = lax.rsqrt(jnp.sum(qh * qh, axis=-1, keepdims=True) * (1.0 / MLA_QK) + EPS) * qscale
        qn = qh * rq * gq
        q_ref[:, h * HEAD_PAD:h * HEAD_PAD + LANES] = qn[:, :LANES].astype(q_ref.dtype)
        q_ref[:, h * HEAD_PAD + LANES:(h + 1) * HEAD_PAD] = rope(qn[:, LANES:]).astype(q_ref.dtype)
        knh = kn[:, h * MLA_NOPE:(h + 1) * MLA_NOPE]
        rk = lax.rsqrt((jnp.sum(knh * knh, axis=-1, keepdims=True) + kr_ss) * (1.0 / MLA_QK) + EPS)
        k_ref[:, h * HEAD_PAD:h * HEAD_PAD + LANES] = (knh * rk * gk[:, :LANES]).astype(k_ref.dtype)
        k_ref[:, h * HEAD_PAD + LANES:(h + 1) * HEAD_PAD] = (kr_rot * rk).astype(k_ref.dtype)


def _mla_prep(proj, wm, tabs, gqa, gkva, gq, gk, wq, wk, wv, qr, kvr, nh):
    t = proj.shape[0]
    tm = _tile(t, 256)
    full = lambda a: pl.BlockSpec(a.shape, lambda i: (0, 0))
    tab = pl.BlockSpec((tm, LANES), lambda i: (i, 0))
    return pl.pallas_call(
        functools.partial(_mla_prep_kernel, qr=qr, kvr=kvr, nh=nh),
        grid=(t // tm,),
        in_specs=[pl.BlockSpec((tm, wm), lambda i: (i, 0)), tab, tab, tab,
                  full(gqa), full(gkva), full(gq), full(gk), full(wq), full(wk), full(wv)],
        out_specs=[pl.BlockSpec((tm, nh * HEAD_PAD), lambda i: (i, 0)),
                   pl.BlockSpec((tm, nh * HEAD_PAD), lambda i: (i, 0)),
                   pl.BlockSpec((tm, nh * HEAD_PAD), lambda i: (i, 0))],
        out_shape=[jax.ShapeDtypeStruct((t, nh * HEAD_PAD), BF16),
                   jax.ShapeDtypeStruct((t, nh * HEAD_PAD), BF16),
                   jax.ShapeDtypeStruct((t, nh * HEAD_PAD), BF16)],
        compiler_params=_params(("arbitrary",)),
    )(proj, *tabs, gqa, gkva, gq, gk, wq, wk, wv)


def _flash_kernel(qi_ref, ki_ref, q_ref, k_ref, v_ref, o_ref, m_ref, acc_ref):
    p = pl.program_id(2)
    qi, ki = qi_ref[p], ki_ref[p]

    @pl.when(ki == 0)
    def _():
        m_ref[...] = jnp.full_like(m_ref, -jnp.inf)
        acc_ref[...] = jnp.zeros_like(acc_ref)

    def update(masked):
        s = _dot_nt(q_ref[...], k_ref[...])
        if masked:
            row = lax.broadcasted_iota(jnp.int32, s.shape, 0)
            col = lax.broadcasted_iota(jnp.int32, s.shape, 1)
            s = jnp.where(col <= row, s, -jnp.inf)
        m_prev = m_ref[...]
        m_new = jnp.maximum(m_prev, jnp.max(s, axis=-1, keepdims=True))
        alpha = jnp.exp(m_prev - m_new)
        pe = jnp.exp((s - m_new).astype(BF16))
        acc_ref[...] = alpha * acc_ref[...] + _dot(pe, v_ref[...])
        m_ref[...] = m_new

    @pl.when(ki < qi)
    def _():
        update(False)

    @pl.when(ki == qi)
    def _():
        update(True)
        acc = acc_ref[...]
        o_ref[...] = (acc[:, :MLA_V] / acc[:, MLA_V:MLA_V + 1]).astype(o_ref.dtype)


def _flash(q, k, v, batch, nh):
    t = q.shape[0]
    s = t // batch
    blk = _tile(s, 1024)
    nq = s // blk
    pairs = [(a, b) for a in range(nq) for b in range(a + 1)]
    qi = jnp.asarray([a for a, _ in pairs], jnp.int32)
    ki = jnp.asarray([b for _, b in pairs], jnp.int32)
    return pl.pallas_call(
        _flash_kernel,
        grid_spec=pltpu.PrefetchScalarGridSpec(
            num_scalar_prefetch=2,
            grid=(batch, nh, len(pairs)),
            in_specs=[pl.BlockSpec((blk, HEAD_PAD), lambda b, h, p, qi, ki: (b * nq + qi[p], h)),
                      pl.BlockSpec((blk, HEAD_PAD), lambda b, h, p, qi, ki: (b * nq + ki[p], h)),
                      pl.BlockSpec((blk, HEAD_PAD), lambda b, h, p, qi, ki: (b * nq + ki[p], h))],
            out_specs=pl.BlockSpec((blk, MLA_V), lambda b, h, p, qi, ki: (b * nq + qi[p], h)),
            scratch_shapes=[pltpu.VMEM((blk, 1), F32), pltpu.VMEM((blk, HEAD_PAD), F32)]),
        out_shape=jax.ShapeDtypeStruct((t, nh * MLA_V), BF16),
        compiler_params=_params(("arbitrary", "arbitrary", "arbitrary")),
    )(qi, ki, q, k, v)


def _wo_kernel(ret_ref, att_ref, w1_ref, w2_ref, x_ref, gate_ref, o_ref):
    mix = _dot(ret_ref[...], w1_ref[...]) + _dot(att_ref[...], w2_ref[...])
    o_ref[...] = x_ref[...] + gate_ref[0] * mix


def _wo(ret, att, w_o, x2, gate, batch):
    t, wr = ret.shape
    wa = att.shape[1]
    assert wr == wa
    d = w_o.shape[1]
    s = t // batch
    tm = _tile(s, 1024)
    tn = _tile(d, 512)
    ns = s // tm
    return pl.pallas_call(
        _wo_kernel,
        grid=(t // tm, d // tn),
        in_specs=[pl.BlockSpec((tm, wr), lambda i, j: (i, 0)),
                  pl.BlockSpec((tm, wa), lambda i, j: (i, 0)),
                  pl.BlockSpec((wr, tn), lambda i, j: (0, j)),
                  pl.BlockSpec((wa, tn), lambda i, j: (1, j)),
                  pl.BlockSpec((tm, tn), lambda i, j: (i, j)),
                  pl.BlockSpec((1, 1, tn), lambda i, j: (i // ns, 0, j))],
        out_specs=pl.BlockSpec((tm, tn), lambda i, j: (i, j)),
        out_shape=jax.ShapeDtypeStruct((t, d), F32),
        compiler_params=_params(("arbitrary", "arbitrary")),
    )(ret, att, w_o, w_o, x2, gate)


def _router_kernel(x_ref, g_ref, sc_ref, sh_ref, wrh_ref, wrl_ref, bias_ref, wsg_ref, wsu_ref,
                   h_ref, a_ref, e_ref, w_ref):
    h = _norm_mod(x_ref[...], g_ref[...], sc_ref[0], sh_ref[0])
    hb = h.astype(BF16)
    hbf = hb.astype(F32)
    h_ref[...] = _pack_pairs(hbf)
    a_ref[...] = (_silu(_dot(hb, wsg_ref[...])) * _dot(hb, wsu_ref[...])).astype(a_ref.dtype)

    hl = (h - hbf).astype(BF16)
    wrh = wrh_ref[...]
    logits = _dot_nt(wrh, hb) + _dot_nt(wrh, hl) + _dot_nt(wrl_ref[...], hb)
    scores = jax.nn.sigmoid(logits)
    sel = scores + bias_ref[...]
    ne, tm = sel.shape
    per_group = ne // N_GROUPS
    neg = -jnp.inf

    def first_argmax(v, iota, size):
        m = jnp.max(v, axis=0, keepdims=True)
        idx = jnp.min(jnp.where(v == m, iota, size), axis=0, keepdims=True)
        return m, idx

    iota_pg = lax.broadcasted_iota(jnp.int32, (per_group, tm), 0)
    gs = []
    for g in range(N_GROUPS):
        sg = sel[g * per_group:(g + 1) * per_group, :]
        m1, i1 = first_argmax(sg, iota_pg, per_group)
        m2 = jnp.max(jnp.where(iota_pg == i1, neg, sg), axis=0, keepdims=True)
        gs.append(m1 + m2)
    gsc = jnp.concatenate(gs, axis=0)
    iota_g = lax.broadcasted_iota(jnp.int32, (N_GROUPS, tm), 0)
    gmask = jnp.zeros((N_GROUPS, tm), jnp.bool_)
    for _ in range(TOPK_GROUPS):
        _, ig = first_argmax(gsc, iota_g, N_GROUPS)
        hit = iota_g == ig
        gmask = jnp.logical_or(gmask, hit)
        gsc = jnp.where(hit, neg, gsc)
    emask = jnp.concatenate(
        [jnp.broadcast_to(gmask[g:g + 1, :], (per_group, tm)) for g in range(N_GROUPS)], axis=0)
    cand = jnp.where(emask, sel, neg)
    iota_e = lax.broadcasted_iota(jnp.int32, (ne, tm), 0)
    es, ws = [], []
    for _ in range(TOP_K):
        _, ie = first_argmax(cand, iota_e, ne)
        hit = iota_e == ie
        es.append(ie)
        ws.append(jnp.sum(jnp.where(hit, scores, 0.0), axis=0, keepdims=True))
        cand = jnp.where(hit, neg, cand)
    wsum = ws[0]
    for wk in ws[1:]:
        wsum = wsum + wk
    e_ref[...] = jnp.concatenate(es, axis=0)
    w_ref[...] = jnp.concatenate(ws, axis=0) / wsum * ROUTED_SCALE


def _router(x2, g, scale, shift, wrh, wrl, bias, wsg, wsu, batch):
    t, d = x2.shape
    s = t // batch
    tm = _tile(s, 256)
    ns = s // tm
    ne = wrh.shape[0]
    f = wsg.shape[1]
    full = lambda a: pl.BlockSpec(a.shape, lambda i: (0,) * a.ndim)
    bat = pl.BlockSpec((1, 1, d), lambda i: (i // ns, 0, 0))
    return pl.pallas_call(
        _router_kernel,
        grid=(t // tm,),
        in_specs=[pl.BlockSpec((tm, d), lambda i: (i, 0)), full(g), bat, bat,
                  full(wrh), full(wrl), full(bias), full(wsg), full(wsu)],
        out_specs=[pl.BlockSpec((tm, d // 2), lambda i: (i, 0)),
                   pl.BlockSpec((tm, f), lambda i: (i, 0)),
                   pl.BlockSpec((TOP_K, tm), lambda i: (0, i)),
                   pl.BlockSpec((TOP_K, tm), lambda i: (0, i))],
        out_shape=[jax.ShapeDtypeStruct((t, d // 2), jnp.uint32),
                   jax.ShapeDtypeStruct((t, f), BF16),
                   jax.ShapeDtypeStruct((TOP_K, t), jnp.int32),
                   jax.ShapeDtypeStruct((TOP_K, t), F32)],
        compiler_params=_params(("arbitrary",)),
    )(x2, g, scale, shift, wrh, wrl, bias, wsg, wsu)


def _moe_kernel(order_ref, te_ref, tr_ref, tn_ref, h_hbm, wg_ref, wu_ref, wd_ref, y_hbm,
                xbuf, ybuf, gsem, ssem, *, n_tok):
    g = pl.program_id(0)
    n_grid = pl.num_programs(0)
    slot = lax.rem(g, 2)
    nxt = jnp.minimum(g + 1, n_grid - 1)
    active = tn_ref[g] > 0
    next_active = jnp.logical_and(g + 1 < n_grid, tn_ref[nxt] > 0)

    def for_rows(n, body):
        def chunk(i, carry):
            for u in range(ROW_UNROLL):
                body(i * ROW_UNROLL + u)
            return carry

        def single(r, carry):
            body(r)
            return carry

        n_chunks = n // ROW_UNROLL
        lax.fori_loop(0, n_chunks, chunk, 0)
        lax.fori_loop(n_chunks * ROW_UNROLL, n, single, 0)

    def gather(tile, sl):
        base = tr_ref[tile]

        def body(r):
            tok = order_ref[base + r] & (n_tok - 1)
            pltpu.make_async_copy(h_hbm.at[pl.ds(tok, 1), :], xbuf.at[sl, pl.ds(r, 1), :],
                                  gsem.at[sl]).start()

        for_rows(tn_ref[tile], body)

    def scatter(tile, sl):
        base = tr_ref[tile]

        def body(r):
            dst = order_ref[base + r]
            pltpu.make_async_copy(ybuf.at[sl, pl.ds(r, 1), :], y_hbm.at[pl.ds(dst, 1), :],
                                  ssem.at[sl]).start()

        for_rows(tn_ref[tile], body)

    def wait_rows(n, copy):
        n_al = pl.multiple_of((n // SUBLANES) * SUBLANES, SUBLANES)

        @pl.when(n_al > 0)
        def _():
            copy(0, n_al).wait()

        def single(r, carry):
            copy(r, 1).wait()
            return carry

        lax.fori_loop(n_al, n, single, 0)

    def wait_gather(tile, sl):
        wait_rows(tn_ref[tile], lambda r, n: pltpu.make_async_copy(
            h_hbm.at[pl.ds(r, n), :], xbuf.at[sl, pl.ds(r, n), :], gsem.at[sl]))

    def wait_scatter(tile, sl):
        wait_rows(tn_ref[tile], lambda r, n: pltpu.make_async_copy(
            ybuf.at[sl, pl.ds(r, n), :], y_hbm.at[pl.ds(r, n), :], ssem.at[sl]))

    @pl.when(g == 0)
    def _():
        xbuf[...] = jnp.zeros_like(xbuf)

        @pl.when(active)
        def _():
            gather(0, 0)

    @pl.when(active)
    def _():
        wait_gather(g, slot)

    @pl.when(next_active)
    def _():
        gather(nxt, 1 - slot)

    @pl.when(active)
    def _():
        half = wg_ref.shape[0] // 2
        lo, hi = _unpack_pairs(xbuf[slot])
        lo, hi = lo.astype(BF16), hi.astype(BF16)
        gate = _dot(lo, wg_ref[:half, :]) + _dot(hi, wg_ref[half:, :])
        up = _dot(lo, wu_ref[:half, :]) + _dot(hi, wu_ref[half:, :])
        y = _dot((_silu(gate) * up).astype(BF16), wd_ref[...])

        @pl.when(g >= 2)
        def _():
            wait_scatter(g - 2, slot)

        ybuf[slot] = _pack_pairs(_round_bf16(y))
        scatter(g, slot)

    @pl.when(jnp.logical_and(active, jnp.logical_not(next_active)))
    def _():
        wait_scatter(g, slot)

        @pl.when(g >= 1)
        def _():
            wait_scatter(g - 1, 1 - slot)


def _moe(order, tile_e, tile_row0, tile_nv, h2p, wg, wu, wd, tm):
    t, dh = h2p.shape
    assert t & (t - 1) == 0, "token count must be a power of two (slot index packing)"
    ne, d, f = wg.shape
    n_tiles = tile_e.shape[0]
    return pl.pallas_call(
        functools.partial(_moe_kernel, n_tok=t),
        grid_spec=pltpu.PrefetchScalarGridSpec(
            num_scalar_prefetch=4,
            grid=(n_tiles,),
            in_specs=[pl.BlockSpec(memory_space=pl.ANY),
                      pl.BlockSpec((None, d, f), lambda g, o, te, tr, tn: (te[g], 0, 0)),
                      pl.BlockSpec((None, d, f), lambda g, o, te, tr, tn: (te[g], 0, 0)),
                      pl.BlockSpec((None, f, d), lambda g, o, te, tr, tn: (te[g], 0, 0))],
            out_specs=pl.BlockSpec(memory_space=pl.ANY),
            scratch_shapes=[pltpu.VMEM((2, tm, dh), jnp.uint32), pltpu.VMEM((2, tm, dh), jnp.uint32),
                            pltpu.SemaphoreType.DMA((2,)), pltpu.SemaphoreType.DMA((2,))]),
        out_shape=jax.ShapeDtypeStruct((TOP_K * t, dh), jnp.uint32),
        compiler_params=_params(("arbitrary",)),
    )(order, tile_e, tile_row0, tile_nv, h2p, wg, wu, wd)


def _moe_schedule(e_idx, ne, tm):
    k, t = e_idx.shape
    p = k * t
    keys = e_idx.reshape(p)
    _, order = lax.sort((keys, jnp.arange(p, dtype=jnp.int32)), num_keys=1, is_stable=True)
    experts = jnp.arange(ne, dtype=jnp.int32)
    counts = jnp.sum((keys[None, :] == experts[:, None]).astype(jnp.int32), axis=1)
    starts = jnp.cumsum(counts) - counts
    ntile = (counts + tm - 1) // tm
    cum = jnp.cumsum(ntile)
    n_tiles = p // tm + ne
    gidx = jnp.arange(n_tiles, dtype=jnp.int32)
    te = jnp.minimum(jnp.sum((gidx[:, None] >= cum[None, :]).astype(jnp.int32), axis=1), ne - 1)
    onehot = (te[:, None] == experts[None, :]).astype(jnp.int32)
    pick = lambda v: jnp.sum(onehot * v[None, :], axis=1)
    local = gidx - pick(cum - ntile)
    nv = jnp.where(gidx < cum[-1], jnp.clip(pick(counts) - local * tm, 0, tm), 0)
    row0 = jnp.where(nv > 0, pick(starts) + local * tm, 0)
    return order, te, row0.astype(jnp.int32), nv.astype(jnp.int32)


def _combine_kernel(*refs):
    y_refs = refs[:TOP_K]
    w_ref, a_ref, wsd_ref, x_ref, gate_ref, o_ref = refs[TOP_K:]
    half = y_refs[0].shape[1]
    w = w_ref[...]
    lo, hi = _unpack_pairs(y_refs[0][...])
    acc_lo, acc_hi = w[:, 0:1] * lo, w[:, 0:1] * hi
    for k in range(1, TOP_K):
        lo, hi = _unpack_pairs(y_refs[k][...])
        acc_lo = acc_lo + w[:, k:k + 1] * lo
        acc_hi = acc_hi + w[:, k:k + 1] * hi
    a = a_ref[...]
    gate = gate_ref[0]
    o_ref[:, :half] = x_ref[:, :half] + gate[:, :half] * (acc_lo + _dot(a, wsd_ref[:, :half]))
    o_ref[:, half:] = x_ref[:, half:] + gate[:, half:] * (acc_hi + _dot(a, wsd_ref[:, half:]))


def _combine(y, w_tk, a_sh, wsd, x2, gate, batch):
    dh = y.shape[1]
    t, d = x2.shape
    kk = w_tk.shape[1]
    assert kk == TOP_K
    f = a_sh.shape[1]
    s = t // batch
    tm = _tile(s, 128)
    ns = s // tm
    nt = t // tm
    return pl.pallas_call(
        _combine_kernel,
        grid=(nt,),
        in_specs=[pl.BlockSpec((tm, dh), lambda i, k=k: (k * nt + i, 0)) for k in range(TOP_K)] + [
                  pl.BlockSpec((tm, kk), lambda i: (i, 0)),
                  pl.BlockSpec((tm, f), lambda i: (i, 0)),
                  pl.BlockSpec((f, d), lambda i: (0, 0)),
                  pl.BlockSpec((tm, d), lambda i: (i, 0)),
                  pl.BlockSpec((1, 1, d), lambda i: (i // ns, 0, 0))],
        out_specs=pl.BlockSpec((tm, d), lambda i: (i, 0)),
        out_shape=jax.ShapeDtypeStruct((t, d), F32),
        compiler_params=_params(("arbitrary",)),
    )(*([y] * TOP_K), w_tk, a_sh, wsd, x2, gate)


def _round_up(v, m):
    return (v + m - 1) // m * m


def kernel(x, c, positions, w_ada, b_ada, norm1_g, w_in, q_a_norm_g, w_uq, kv_a_norm_g, w_ukv, q_norm_g, k_norm_g, w_o, norm2_g, w_router, router_bias, w_exp_gate, w_exp_up, w_exp_down, w_sh_gate, w_sh_up, w_sh_down):
    b, s, d = x.shape
    t = b * s
    depth = w_ada.shape[0]
    qr = q_a_norm_g.shape[1]
    kvr = kv_a_norm_g.shape[1]
    nh = w_uq.shape[2] // MLA_QK
    in_w = w_in.shape[2]
    ret_heads = (in_w - qr - kvr - MLA_ROPE) // (2 * RET_DK + 2 * RET_DV)
    ret_w = ret_heads * RET_DK
    ne = w_router.shape[2]
    moe_tm = 256

    pos = positions.reshape(t).astype(F32)
    inv_r = ROPE_BASE ** (-jnp.arange(0, RET_DK, 2, dtype=F32) / RET_DK)
    ang_r = pos[:, None] * inv_r
    cos_r, sin_r = jnp.cos(ang_r), jnp.sin(ang_r)
    inv_m = ROPE_BASE ** (-jnp.arange(0, MLA_ROPE, 2, dtype=F32) / MLA_ROPE)
    ang_m = pos[:, None] * inv_m
    cm, sm = jnp.cos(ang_m), jnp.sin(ang_m)
    z = jnp.zeros_like(cm)
    mla_tabs = (jnp.concatenate([cm, cm, z, z], axis=-1),
                jnp.concatenate([-sm, z, z, z], axis=-1),
                jnp.concatenate([z, sm, z, z], axis=-1))
    log_g = jnp.log(1.0 - 2.0 ** (-5.0 - jnp.arange(ret_heads, dtype=F32)))

    c_pad = jnp.zeros((8, d), F32).at[:b].set(c)
    x2 = x.reshape(t, d)
    wm = _round_up(qr + LANES + kvr, 512)
    ret_cols = 4 * ret_w

    for l in range(depth):
        mod = _ada(c_pad, w_ada[l], b_ada[l][None, :])[:b]
        shift_a, scale_a, gate_a, shift_m, scale_m, gate_m = (
            m.reshape(b, 1, d) for m in jnp.split(mod, 6, axis=-1))

        wi = w_in[l]
        w_cq = wi[:, ret_cols:ret_cols + qr]
        w_ckv = wi[:, ret_cols + qr:ret_cols + qr + kvr]
        w_kr = wi[:, ret_cols + qr + kvr:]
        w_mla = jnp.concatenate(
            [w_cq, w_kr, jnp.zeros((d, LANES - MLA_ROPE), F32), w_ckv,
             jnp.zeros((d, wm - qr - LANES - kvr), F32)], axis=-1).astype(BF16)

        h1 = _norm_mod_call(x2, norm1_g[l][None, :], scale_a, shift_a, b, BF16)
        proj_ret = _mm_wcast(h1, w_in, l, ret_cols, BF16)
        proj = _mm(h1, w_mla, BF16)

        ret = _retention(proj_ret, cos_r, sin_r, log_g, b, ret_heads, 0)

        wq = jnp.pad(w_uq[l].reshape(qr, nh, MLA_QK), ((0, 0), (0, 0), (0, HEAD_PAD - MLA_QK)))
        wq = wq.reshape(qr, nh * HEAD_PAD).astype(BF16)
        wkv = w_ukv[l].reshape(kvr, nh, MLA_NOPE + MLA_V)
        wk = wkv[:, :, :MLA_NOPE].reshape(kvr, nh * MLA_NOPE).astype(BF16)
        wv = wkv[:, :, MLA_NOPE:].reshape(kvr, nh * MLA_V).astype(BF16)
        gq = jnp.pad(q_norm_g[l], (0, HEAD_PAD - MLA_QK))[None, :]
        gk = jnp.pad(k_norm_g[l], (0, HEAD_PAD - MLA_QK))[None, :]
        q, k, v = _mla_prep(proj, wm, mla_tabs, q_a_norm_g[l][None, :], kv_a_norm_g[l][None, :],
                            gq, gk, wq, wk, wv, qr, kvr, nh)
        att = _flash(q, k, v, b, nh)

        x2 = _wo(ret, att, w_o[l].astype(BF16), x2, gate_a, b)

        wr_t = w_router[l].T
        wrh = wr_t.astype(BF16)
        wrl = (wr_t - wrh.astype(F32)).astype(BF16)
        h2p, a_sh, e_idx, w_kt = _router(x2, norm2_g[l][None, :], scale_m, shift_m, wrh, wrl,
                                         router_bias[l][:, None], w_sh_gate[l].astype(BF16),
                                         w_sh_up[l].astype(BF16), b)
        order, tile_e, tile_row0, tile_nv = _moe_schedule(e_idx, ne, moe_tm)
        y = _moe(order, tile_e, tile_row0, tile_nv, h2p, w_exp_gate[l].astype(BF16),
                 w_exp_up[l].astype(BF16), w_exp_down[l].astype(BF16), moe_tm)
        x2 = _combine(y, w_kt.T, a_sh, w_sh_down[l].astype(BF16), x2, gate_m, b)

    return x2.reshape(b, s, d)
```

```python
import functools
import math

import jax
import jax.numpy as jnp
from jax import lax
from jax.experimental import pallas as pl
from jax.experimental.pallas import tpu as pltpu

F32 = jnp.float32
BF16 = jnp.bfloat16

EPS = 1e-6
ROPE_BASE = 10000.0
RET_DK = 256
RET_DV = 256
MLA_NOPE = 128
MLA_ROPE = 64
MLA_V = 128
MLA_QK = MLA_NOPE + MLA_ROPE
N_GROUPS = 8
TOPK_GROUPS = 4
TOP_K = 8
ROUTED_SCALE = 2.5

LANES = 128
SUBLANES = 8
HEAD_PAD = 2 * LANES
VMEM_LIMIT = 56 * 1024 * 1024
ROW_UNROLL = 8


def _tile(dim, pref):
    t = min(dim, pref)
    while dim % t:
        t //= 2
    return t


def _params(sem):
    return pltpu.CompilerParams(dimension_semantics=sem, vmem_limit_bytes=VMEM_LIMIT)


def _silu(v):
    return v * jax.nn.sigmoid(v)


def _dot(a, b):
    return jnp.dot(a, b, preferred_element_type=F32)


def _dot_nt(a, b):
    return lax.dot_general(a, b, (((1,), (1,)), ((), ())), preferred_element_type=F32)


def _dot_tn(a, b):
    return lax.dot_general(a, b, (((0,), (0,)), ((), ())), preferred_element_type=F32)


HI_HALF = 0xFFFF0000


def _pack_pairs(v):
    half = v.shape[1] // 2
    bits = lax.bitcast_convert_type(v, jnp.uint32)
    return (bits[:, :half] >> 16) | (bits[:, half:] & jnp.uint32(HI_HALF))


def _unpack_pairs(w):
    lo = lax.bitcast_convert_type(w << 16, F32)
    hi = lax.bitcast_convert_type(w & jnp.uint32(HI_HALF), F32)
    return lo, hi


def _round_bf16(v):
    return v.astype(BF16).astype(F32)


def _ada_kernel(c_ref, w_ref, b_ref, o_ref):
    ca = _silu(c_ref[...]).astype(BF16)
    o_ref[...] = _dot(ca, w_ref[...].astype(BF16)) + b_ref[...]


def _ada(c_pad, w, b):
    m, d = c_pad.shape
    n = w.shape[1]
    tn = _tile(n, 512)
    return pl.pallas_call(
        _ada_kernel,
        grid=(n // tn,),
        in_specs=[pl.BlockSpec((m, d), lambda j: (0, 0)),
                  pl.BlockSpec((d, tn), lambda j: (0, j)),
                  pl.BlockSpec((1, tn), lambda j: (0, j))],
        out_specs=pl.BlockSpec((m, tn), lambda j: (0, j)),
        out_shape=jax.ShapeDtypeStruct((m, n), F32),
        compiler_params=_params(("arbitrary",)),
    )(c_pad, w, b)


def _norm_mod(x, g, scale, shift):
    xf = x.astype(F32)
    h = xf * lax.rsqrt(jnp.mean(xf * xf, axis=-1, keepdims=True) + EPS) * g
    return h * (1.0 + scale) + shift


def _norm_mod_kernel(x_ref, g_ref, sc_ref, sh_ref, o_ref):
    o_ref[...] = _norm_mod(x_ref[...], g_ref[...], sc_ref[0], sh_ref[0]).astype(o_ref.dtype)


def _norm_mod_call(x2, g, scale, shift, batch, out_dtype):
    t, d = x2.shape
    s = t // batch
    ts = _tile(s, 512)
    ns = s // ts
    return pl.pallas_call(
        _norm_mod_kernel,
        grid=(batch, ns),
        in_specs=[pl.BlockSpec((ts, d), lambda b, i: (b * ns + i, 0)),
                  pl.BlockSpec((1, d), lambda b, i: (0, 0)),
                  pl.BlockSpec((1, 1, d), lambda b, i: (b, 0, 0)),
                  pl.BlockSpec((1, 1, d), lambda b, i: (b, 0, 0))],
        out_specs=pl.BlockSpec((ts, d), lambda b, i: (b * ns + i, 0)),
        out_shape=jax.ShapeDtypeStruct((t, d), out_dtype),
        compiler_params=_params(("arbitrary", "arbitrary")),
    )(x2, g, scale, shift)


def _mm_kernel(a_ref, w_ref, o_ref):
    o_ref[...] = _dot(a_ref[...], w_ref[...]).astype(o_ref.dtype)


def _mm(a, w, out_dtype, tm_pref=1024, tn_pref=512):
    m, k = a.shape
    n = w.shape[1]
    tm, tn = _tile(m, tm_pref), _tile(n, tn_pref)
    return pl.pallas_call(
        _mm_kernel,
        grid=(m // tm, n // tn),
        in_specs=[pl.BlockSpec((tm, k), lambda i, j: (i, 0)),
                  pl.BlockSpec((k, tn), lambda i, j: (0, j))],
        out_specs=pl.BlockSpec((tm, tn), lambda i, j: (i, j)),
        out_shape=jax.ShapeDtypeStruct((m, n), out_dtype),
        compiler_params=_params(("arbitrary", "arbitrary")),
    )(a, w)


def _mm_wcast_kernel(a_ref, w_ref, o_ref, wb_ref):
    @pl.when(pl.program_id(1) == 0)
    def _():
        wb_ref[...] = w_ref[...].astype(wb_ref.dtype)

    o_ref[...] = _dot(a_ref[...], wb_ref[...]).astype(o_ref.dtype)


def _mm_wcast(a, w3, layer, n, out_dtype, tm_pref=1024, tn_pref=512):
    m, k = a.shape
    tm, tn = _tile(m, tm_pref), _tile(n, tn_pref)
    return pl.pallas_call(
        _mm_wcast_kernel,
        grid=(n // tn, m // tm),
        in_specs=[pl.BlockSpec((tm, k), lambda j, i: (i, 0)),
                  pl.BlockSpec((None, k, tn), lambda j, i: (layer, 0, j))],
        out_specs=pl.BlockSpec((tm, tn), lambda j, i: (i, j)),
        out_shape=jax.ShapeDtypeStruct((m, n), out_dtype),
        scratch_shapes=[pltpu.VMEM((k, tn), BF16)],
        compiler_params=_params(("arbitrary", "arbitrary")),
    )(a, w3)


def _ret_kernel(lg_ref, q_ref, k_ref, v_ref, g_ref, cos_ref, sin_ref, o_ref, state_ref):
    h = pl.program_id(1)

    @pl.when(pl.program_id(2) == 0)
    def _():
        state_ref[...] = jnp.zeros_like(state_ref)

    lg = lg_ref[h]
    c = q_ref.shape[0]
    cos, sin = cos_ref[...], sin_ref[...]
    half = RET_DK // 2

    def rope(v):
        v1, v2 = v[:, :half], v[:, half:]
        return jnp.concatenate([v1 * cos - v2 * sin, v2 * cos + v1 * sin], axis=-1)

    q = rope(q_ref[...].astype(F32))
    k = rope(k_ref[...].astype(F32)) * (RET_DK ** -0.5)
    v = v_ref[...]
    ii = lax.broadcasted_iota(jnp.int32, (c, c), 0)
    jj = lax.broadcasted_iota(jnp.int32, (c, c), 1)
    diff = (ii - jj).astype(F32)
    inner_decay = jnp.where(diff >= 0, jnp.exp(diff * lg), 0.0)
    ri = lax.broadcasted_iota(jnp.int32, (c, 1), 0).astype(F32)
    q_decay = jnp.exp((ri + 1.0) * lg)
    k_decay = jnp.exp((c - 1.0 - ri) * lg)
    chunk_decay = jnp.exp(jnp.full((1, 1), c, F32) * lg)

    qb = q.astype(BF16)
    sc = _dot_nt(qb, k.astype(BF16)) * inner_decay
    inner = _dot(sc.astype(BF16), v)
    state = state_ref[...]
    cross = _dot(qb, state.astype(BF16)) * q_decay
    kv = _dot_tn((k * k_decay).astype(BF16), v)
    state_ref[...] = state * chunk_decay + kv
    out = inner + cross
    r = out * lax.rsqrt(jnp.mean(out * out, axis=-1, keepdims=True) + EPS)
    o_ref[...] = (r * _silu(g_ref[...].astype(F32))).astype(o_ref.dtype)


def _retention(proj, cos_r, sin_r, log_g, batch, n_heads, col0):
    t = proj.shape[0]
    s = t // batch
    c = _tile(s, 256)
    n = s // c
    hh = n_heads

    def blk(off):
        return pl.BlockSpec((c, RET_DK), lambda b, h, i, lg: (b * n + i, col0 + off + h))

    tab = pl.BlockSpec((c, RET_DK // 2), lambda b, h, i, lg: (b * n + i, 0))
    return pl.pallas_call(
        _ret_kernel,
        grid_spec=pltpu.PrefetchScalarGridSpec(
            num_scalar_prefetch=1,
            grid=(batch, hh, n),
            in_specs=[blk(0), blk(hh), blk(2 * hh), blk(3 * hh), tab, tab],
            out_specs=pl.BlockSpec((c, RET_DV), lambda b, h, i, lg: (b * n + i, h)),
            scratch_shapes=[pltpu.VMEM((RET_DK, RET_DV), F32)]),
        out_shape=jax.ShapeDtypeStruct((t, hh * RET_DV), BF16),
        compiler_params=_params(("arbitrary", "arbitrary", "arbitrary")),
    )(log_g, proj, proj, proj, proj, cos_r, sin_r)


def _mla_prep_kernel(p_ref, ct_ref, sa_ref, sb_ref, gqa_ref, gkva_ref, gq_ref, gk_ref,
                     wq_ref, wk_ref, wv_ref, q_ref, k_ref, v_ref, *, qr, kvr, nh):
    p = p_ref[...].astype(F32)
    cq = p[:, :qr]
    krp = p[:, qr:qr + LANES]
    ckv = p[:, qr + LANES:qr + LANES + kvr]

    def rms(v, n):
        return v * lax.rsqrt(jnp.sum(v * v, axis=-1, keepdims=True) * (1.0 / n) + EPS)

    cqn = (rms(cq, qr) * gqa_ref[...]).astype(BF16)
    ckvn = (rms(ckv, kvr) * gkva_ref[...]).astype(BF16)
    qf = _dot(cqn, wq_ref[...])
    kn = _dot(ckvn, wk_ref[...])
    vf = _dot(ckvn, wv_ref[...])
    ones_col = (lax.broadcasted_iota(jnp.int32, (p.shape[0], LANES), 1) == 0).astype(v_ref.dtype)
    for h in range(nh):
        v_ref[:, h * HEAD_PAD:h * HEAD_PAD + MLA_V] = vf[:, h * MLA_V:(h + 1) * MLA_V].astype(v_ref.dtype)
        v_ref[:, h * HEAD_PAD + MLA_V:(h + 1) * HEAD_PAD] = ones_col

    ct, sa, sb = ct_ref[...], sa_ref[...], sb_ref[...]

    def rope(v):
        return v * ct + pltpu.roll(v, LANES - MLA_ROPE // 2, 1) * sa + pltpu.roll(v, MLA_ROPE // 2, 1) * sb

    gq, gk = gq_ref[...], gk_ref[...]
    kr_rot = rope(krp * gk[:, LANES:])
    kr_ss = jnp.sum(krp * krp, axis=-1, keepdims=True)
    qscale = 1.0 / math.sqrt(MLA_QK)
    for h in range(nh):
        qh = qf[:, h * HEAD_PAD:(h + 1) * HEAD_PAD]
        rq = lax.rsqrt(jnp.sum(qh * qh, axis=-1, keepdims=True) * (1.0 / MLA_QK) + EPS) * qscale
        qn = qh * rq * gq
        q_ref[:, h * HEAD_PAD:h * HEAD_PAD + LANES] = qn[:, :LANES].astype(q_ref.dtype)
        q_ref[:, h * HEAD_PAD + LANES:(h + 1) * HEAD_PAD] = rope(qn[:, LANES:]).astype(q_ref.dtype)
        knh = kn[:, h * MLA_NOPE:(h + 1) * MLA_NOPE]
        rk = lax.rsqrt((jnp.sum(knh * knh, axis=-1, keepdims=True) + kr_ss) * (1.0 / MLA_QK) + EPS)
        k_ref[:, h * HEAD_PAD:h * HEAD_PAD + LANES] = (knh * rk * gk[:, :LANES]).astype(k_ref.dtype)
        k_ref[:, h * HEAD_PAD + LANES:(h + 1) * HEAD_PAD] = (kr_rot * rk).astype(k_ref.dtype)


def _mla_prep(proj, wm, tabs, gqa, gkva, gq, gk, wq, wk, wv, qr, kvr, nh):
    t = proj.shape[0]
    tm = _tile(t, 256)
    full = lambda a: pl.BlockSpec(a.shape, lambda i: (0, 0))
    tab = pl.BlockSpec((tm, LANES), lambda i: (i, 0))
    return pl.pallas_call(
        functools.partial(_mla_prep_kernel, qr=qr, kvr=kvr, nh=nh),
        grid=(t // tm,),
        in_specs=[pl.BlockSpec((tm, wm), lambda i: (i, 0)), tab, tab, tab,
                  full(gqa), full(gkva), full(gq), full(gk), full(wq), full(wk), full(wv)],
        out_specs=[pl.BlockSpec((tm, nh * HEAD_PAD), lambda i: (i, 0)),
                   pl.BlockSpec((tm, nh * HEAD_PAD), lambda i: (i, 0)),
                   pl.BlockSpec((tm, nh * HEAD_PAD), lambda i: (i, 0))],
        out_shape=[jax.ShapeDtypeStruct((t, nh * HEAD_PAD), BF16),
                   jax.ShapeDtypeStruct((t, nh * HEAD_PAD), BF16),
                   jax.ShapeDtypeStruct((t, nh * HEAD_PAD), BF16)],
        compiler_params=_params(("arbitrary",)),
    )(proj, *tabs, gqa, gkva, gq, gk, wq, wk, wv)


def _flash_kernel(qi_ref, ki_ref, q_ref, k_ref, v_ref, o_ref, m_ref, acc_ref):
    p = pl.program_id(2)
    qi, ki = qi_ref[p], ki_ref[p]

    @pl.when(ki == 0)
    def _():
        m_ref[...] = jnp.full_like(m_ref, -jnp.inf)
        acc_ref[...] = jnp.zeros_like(acc_ref)

    def update(masked):
        blk = q_ref.shape[0]
        sub = min(blk, HEAD_PAD)
        for i in range(blk // sub):
            rows = slice(i * sub, (i + 1) * sub)
            nk = (i + 1) * sub if masked else blk
            s = _dot_nt(q_ref[rows, :], k_ref[:nk, :])
            if masked:
                row = lax.broadcasted_iota(jnp.int32, s.shape, 0) + i * sub
                col = lax.broadcasted_iota(jnp.int32, s.shape, 1)
                s = jnp.where(col <= row, s, -jnp.inf)
            m_prev = m_ref[rows, :]
            m_new = jnp.maximum(m_prev, jnp.max(s, axis=-1, keepdims=True))
            alpha = jnp.exp(m_prev - m_new)
            pe = jnp.exp((s - m_new).astype(BF16))
            acc_ref[rows, :] = alpha * acc_ref[rows, :] + _dot(pe, v_ref[:nk, :])
            m_ref[rows, :] = m_new

    @pl.when(ki < qi)
    def _():
        update(False)

    @pl.when(ki == qi)
    def _():
        update(True)
        acc = acc_ref[...]
        o_ref[...] = (acc[:, :MLA_V] / acc[:, MLA_V:MLA_V + 1]).astype(o_ref.dtype)


def _flash(q, k, v, batch, nh):
    t = q.shape[0]
    s = t // batch
    blk = _tile(s, 1024)
    nq = s // blk
    pairs = [(a, b) for a in range(nq) for b in range(a + 1)]
    qi = jnp.asarray([a for a, _ in pairs], jnp.int32)
    ki = jnp.asarray([b for _, b in pairs], jnp.int32)
    return pl.pallas_call(
        _flash_kernel,
        grid_spec=pltpu.PrefetchScalarGridSpec(
            num_scalar_prefetch=2,
            grid=(batch, nh, len(pairs)),
            in_specs=[pl.BlockSpec((blk, HEAD_PAD), lambda b, h, p, qi, ki: (b * nq + qi[p], h)),
                      pl.BlockSpec((blk, HEAD_PAD), lambda b, h, p, qi, ki: (b * nq + ki[p], h)),
                      pl.BlockSpec((blk, HEAD_PAD), lambda b, h, p, qi, ki: (b * nq + ki[p], h))],
            out_specs=pl.BlockSpec((blk, MLA_V), lambda b, h, p, qi, ki: (b * nq + qi[p], h)),
            scratch_shapes=[pltpu.VMEM((blk, 1), F32), pltpu.VMEM((blk, HEAD_PAD), F32)]),
        out_shape=jax.ShapeDtypeStruct((t, nh * MLA_V), BF16),
        compiler_params=_params(("arbitrary", "arbitrary", "arbitrary")),
    )(qi, ki, q, k, v)


def _wo_kernel(ret_ref, att_ref, w1_ref, w2_ref, x_ref, gate_ref, o_ref):
    mix = _dot(ret_ref[...], w1_ref[...]) + _dot(att_ref[...], w2_ref[...])
    o_ref[...] = x_ref[...] + gate_ref[0] * mix


def _wo(ret, att, w_o, x2, gate, batch):
    t, wr = ret.shape
    wa = att.shape[1]
    assert wr == wa
    d = w_o.shape[1]
    s = t // batch
    tm = _tile(s, 1024)
    tn = _tile(d, 512)
    ns = s // tm
    return pl.pallas_call(
        _wo_kernel,
        grid=(t // tm, d // tn),
        in_specs=[pl.BlockSpec((tm, wr), lambda i, j: (i, 0)),
                  pl.BlockSpec((tm, wa), lambda i, j: (i, 0)),
                  pl.BlockSpec((wr, tn), lambda i, j: (0, j)),
                  pl.BlockSpec((wa, tn), lambda i, j: (1, j)),
                  pl.BlockSpec((tm, tn), lambda i, j: (i, j)),
                  pl.BlockSpec((1, 1, tn), lambda i, j: (i // ns, 0, j))],
        out_specs=pl.BlockSpec((tm, tn), lambda i, j: (i, j)),
        out_shape=jax.ShapeDtypeStruct((t, d), F32),
        compiler_params=_params(("arbitrary", "arbitrary")),
    )(ret, att, w_o, w_o, x2, gate)


def _router_kernel(x_ref, g_ref, sc_ref, sh_ref, wrh_ref, wrl_ref, bias_ref, wsg_ref, wsu_ref,
                   h_ref, a_ref, e_ref, w_ref):
    h = _norm_mod(x_ref[...], g_ref[...], sc_ref[0], sh_ref[0])
    hb = h.astype(BF16)
    hbf = hb.astype(F32)
    h_ref[...] = _pack_pairs(hbf)
    a_ref[...] = (_silu(_dot(hb, wsg_ref[...])) * _dot(hb, wsu_ref[...])).astype(a_ref.dtype)

    hl = (h - hbf).astype(BF16)
    wrh = wrh_ref[...]
    logits = _dot_nt(wrh, hb) + _dot_nt(wrh, hl) + _dot_nt(wrl_ref[...], hb)
    scores = jax.nn.sigmoid(logits)
    sel = scores + bias_ref[...]
    ne, tm = sel.shape
    per_group = ne // N_GROUPS
    neg = -jnp.inf

    def first_argmax(v, iota, size):
        m = jnp.max(v, axis=0, keepdims=True)
        idx = jnp.min(jnp.where(v == m, iota, size), axis=0, keepdims=True)
        return m, idx

    iota_pg = lax.broadcasted_iota(jnp.int32, (per_group, tm), 0)
    gs = []
    for g in range(N_GROUPS):
        sg = sel[g * per_group:(g + 1) * per_group, :]
        m1, i1 = first_argmax(sg, iota_pg, per_group)
        m2 = jnp.max(jnp.where(iota_pg == i1, neg, sg), axis=0, keepdims=True)
        gs.append(m1 + m2)
    gsc = jnp.concatenate(gs, axis=0)
    iota_g = lax.broadcasted_iota(jnp.int32, (N_GROUPS, tm), 0)
    gmask = jnp.zeros((N_GROUPS, tm), jnp.bool_)
    for _ in range(TOPK_GROUPS):
        _, ig = first_argmax(gsc, iota_g, N_GROUPS)
        hit = iota_g == ig
        gmask = jnp.logical_or(gmask, hit)
        gsc = jnp.where(hit, neg, gsc)
    emask = jnp.concatenate(
        [jnp.broadcast_to(gmask[g:g + 1, :], (per_group, tm)) for g in range(N_GROUPS)], axis=0)
    cand = jnp.where(emask, sel, neg)
    iota_e = lax.broadcasted_iota(jnp.int32, (ne, tm), 0)
    es, ws = [], []
    for _ in range(TOP_K):
        _, ie = first_argmax(cand, iota_e, ne)
        hit = iota_e == ie
        es.append(ie)
        ws.append(jnp.sum(jnp.where(hit, scores, 0.0), axis=0, keepdims=True))
        cand = jnp.where(hit, neg, cand)
    wsum = ws[0]
    for wk in ws[1:]:
        wsum = wsum + wk
    e_ref[...] = jnp.concatenate(es, axis=0)
    w_ref[...] = jnp.concatenate(ws, axis=0) / wsum * ROUTED_SCALE


def _router(x2, g, scale, shift, wrh, wrl, bias, wsg, wsu, batch):
    t, d = x2.shape
    s = t // batch
    tm = _tile(s, 256)
    ns = s // tm
    ne = wrh.shape[0]
    f = wsg.shape[1]
    full = lambda a: pl.BlockSpec(a.shape, lambda i: (0,) * a.ndim)
    bat = pl.BlockSpec((1, 1, d), lambda i: (i // ns, 0, 0))
    return pl.pallas_call(
        _router_kernel,
        grid=(t // tm,),
        in_specs=[pl.BlockSpec((tm, d), lambda i: (i, 0)), full(g), bat, bat,
                  full(wrh), full(wrl), full(bias), full(wsg), full(wsu)],
        out_specs=[pl.BlockSpec((tm, d // 2), lambda i: (i, 0)),
                   pl.BlockSpec((tm, f), lambda i: (i, 0)),
                   pl.BlockSpec((TOP_K, tm), lambda i: (0, i)),
                   pl.BlockSpec((TOP_K, tm), lambda i: (0, i))],
        out_shape=[jax.ShapeDtypeStruct((t, d // 2), jnp.uint32),
                   jax.ShapeDtypeStruct((t, f), BF16),
                   jax.ShapeDtypeStruct((TOP_K, t), jnp.int32),
                   jax.ShapeDtypeStruct((TOP_K, t), F32)],
        compiler_params=_params(("arbitrary",)),
    )(x2, g, scale, shift, wrh, wrl, bias, wsg, wsu)


def _moe_kernel_old(order_ref, te_ref, tr_ref, tn_ref, h_hbm, wg_ref, wu_ref, wd_ref, y_hbm,
                    xbuf, ybuf, gsem, ssem, *, n_tok):
    g = pl.program_id(0)
    n_grid = pl.num_programs(0)
    n_slots = TOP_K * n_tok
    tm = xbuf.shape[1] * SUBLANES
    nxt = jnp.minimum(g + 1, n_grid - 1)
    prv = jnp.maximum(g - 1, 0)
    active = tn_ref[g] > 0
    last_active = jnp.logical_and(active, jnp.logical_or(g + 1 == n_grid, tn_ref[nxt] == 0))

    def slot_index(tile, r):
        return order_ref[jnp.minimum(tr_ref[tile] + r, n_slots - 1)]

    def issue_gather(tile, b):
        for r in range(tm):
            tok = slot_index(tile, r) & (n_tok - 1)
            pltpu.make_async_copy(h_hbm.at[tok >> 3, pl.ds(tok & 7, 1), :],
                                  xbuf.at[b, r // SUBLANES, pl.ds(r % SUBLANES, 1), :], gsem.at[b]).start()

    def issue_scatter(tile, nv, b):
        for r in range(tm):
            dst = jnp.where(r < nv, slot_index(tile, r), n_slots + b * tm + r)
            pltpu.make_async_copy(ybuf.at[b, r // SUBLANES, pl.ds(r % SUBLANES, 1), :],
                                  y_hbm.at[dst >> 3, pl.ds(dst & 7, 1), :], ssem.at[b]).start()

    def wait_gather(b):
        pltpu.make_async_copy(h_hbm.at[pl.ds(0, tm // SUBLANES)], xbuf.at[b], gsem.at[b]).wait()

    def wait_scatter(b):
        pltpu.make_async_copy(ybuf.at[b], y_hbm.at[pl.ds(0, tm // SUBLANES)], ssem.at[b]).wait()

    @pl.when(g == 0)
    def _():
        ybuf[...] = jnp.zeros_like(ybuf)
        for b in range(2):
            dump = pltpu.make_async_copy(
                ybuf.at[b], y_hbm.at[pl.ds((n_slots + b * tm) // SUBLANES, tm // SUBLANES)], ssem.at[b])
            dump.start()
            dump.wait()
        issue_gather(0, 0)

    def step(b):
        wait_gather(b)

        @pl.when(g >= 1)
        def _():
            wait_scatter(b)

        xw = xbuf[b].reshape(tm, xbuf.shape[3])
        lo, hi = _unpack_pairs(xw)
        lo, hi = lo.astype(BF16), hi.astype(BF16)
        half = wg_ref.shape[0] // 2
        issue_gather(nxt, 1 - b)
        gate = _dot(lo, wg_ref[:half, :]) + _dot(hi, wg_ref[half:, :])
        up = _dot(lo, wu_ref[:half, :]) + _dot(hi, wu_ref[half:, :])
        issue_scatter(prv, jnp.where(g >= 1, tn_ref[prv], 0), 1 - b)
        y = _dot((_silu(gate) * up).astype(BF16), wd_ref[...])
        ybuf[b] = _pack_pairs(_round_bf16(y)).reshape(ybuf.shape[1:])

        @pl.when(last_active)
        def _():
            issue_scatter(g, tn_ref[g], b)
            wait_scatter(b)
            wait_scatter(1 - b)
            wait_gather(1 - b)

    for b in range(2):
        @pl.when(jnp.logical_and(active, lax.rem(g, 2) == b))
        def _(b=b):
            step(b)


def _moe_old(order, tile_e, tile_row0, tile_nv, h2p, wg, wu, wd, tm):
    t, dh = h2p.shape
    assert t & (t - 1) == 0, "token count must be a power of two (slot index packing)"
    ne, d, f = wg.shape
    n_tiles = tile_e.shape[0]
    n_rows = TOP_K * t + 2 * tm
    h3 = h2p.reshape(t // SUBLANES, SUBLANES, dh)
    buf = pltpu.VMEM((2, tm // SUBLANES, SUBLANES, dh), jnp.uint32)
    y3 = pl.pallas_call(
        functools.partial(_moe_kernel, n_tok=t),
        grid_spec=pltpu.PrefetchScalarGridSpec(
            num_scalar_prefetch=4,
            grid=(n_tiles,),
            in_specs=[pl.BlockSpec(memory_space=pl.ANY),
                      pl.BlockSpec((None, d, f), lambda g, o, te, tr, tn: (te[g], 0, 0)),
                      pl.BlockSpec((None, d, f), lambda g, o, te, tr, tn: (te[g], 0, 0)),
                      pl.BlockSpec((None, f, d), lambda g, o, te, tr, tn: (te[g], 0, 0))],
            out_specs=pl.BlockSpec(memory_space=pl.ANY),
            scratch_shapes=[buf, buf, pltpu.SemaphoreType.DMA((2,)), pltpu.SemaphoreType.DMA((2,))]),
        out_shape=jax.ShapeDtypeStruct((n_rows // SUBLANES, SUBLANES, dh), jnp.uint32),
        compiler_params=_params(("arbitrary",)),
    )(order, tile_e, tile_row0, tile_nv, h3, wg, wu, wd)
    return y3.reshape(n_rows, dh)


def _moe_kernel(te_ref, tn_ref, tok_tab, dst_tab, h_hbm, wg_ref, wu_ref, wd_ref, y_hbm,
                xbuf, ybuf, tok_idx, dst_idx, gsem, ssem, isem):
    g = pl.program_id(0)
    n_grid = pl.num_programs(0)
    tm = xbuf.shape[1] * SUBLANES
    dump_row = dst_tab.shape[0] - 1
    nxt = jnp.minimum(g + 1, n_grid - 1)
    active = tn_ref[g] > 0
    next_active = jnp.logical_and(g + 1 < n_grid, tn_ref[nxt] > 0)
    last_active = jnp.logical_and(active, jnp.logical_not(next_active))

    def issue_gather(ib, b):
        for r in range(tm):
            tok = tok_idx[ib, 0, r]
            pltpu.make_async_copy(h_hbm.at[tok >> 3, pl.ds(tok & 7, 1), :],
                                  xbuf.at[b, r // SUBLANES, pl.ds(r % SUBLANES, 1), :], gsem.at[b]).start()

    def issue_scatter(ib, b):
        for r in range(tm):
            dst = dst_idx[ib, 0, r]
            pltpu.make_async_copy(ybuf.at[b, r // SUBLANES, pl.ds(r % SUBLANES, 1), :],
                                  y_hbm.at[dst >> 3, pl.ds(dst & 7, 1), :], ssem.at[b]).start()

    def wait_gather(b):
        pltpu.make_async_copy(h_hbm.at[pl.ds(0, tm // SUBLANES)], xbuf.at[b], gsem.at[b]).wait()

    def wait_scatter(b):
        pltpu.make_async_copy(ybuf.at[b], y_hbm.at[pl.ds(0, tm // SUBLANES)], ssem.at[b]).wait()

    def tok_copy(row, b):
        return pltpu.make_async_copy(tok_tab.at[row], tok_idx.at[b], isem.at[b])

    def dst_copy(row, b):
        return pltpu.make_async_copy(dst_tab.at[row], dst_idx.at[b], isem.at[b])

    @pl.when(g == 0)
    def _():
        ybuf[...] = jnp.zeros_like(ybuf)
        for b in range(2):
            dump = pltpu.make_async_copy(
                ybuf.at[b], y_hbm.at[pl.ds(y_hbm.shape[0] - (2 - b) * (tm // SUBLANES), tm // SUBLANES)],
                ssem.at[b])
            dump.start()
            dump.wait()
        first = tok_copy(0, 1)
        first.start()
        first.wait()
        issue_gather(1, 0)
        tok_copy(nxt, 0).start()
        dst_copy(dump_row, 0).start()

    def step(b):
        tok_copy(0, b).wait()
        dst_copy(0, b).wait()

        @pl.when(next_active)
        def _():
            tok_copy(jnp.minimum(g + 2, n_grid - 1), 1 - b).start()
            dst_copy(g, 1 - b).start()

        wait_gather(b)

        @pl.when(g >= 1)
        def _():
            wait_scatter(b)

        xw = xbuf[b].reshape(tm, xbuf.shape[3])
        lo, hi = _unpack_pairs(xw)
        lo, hi = lo.astype(BF16), hi.astype(BF16)
        half = wg_ref.shape[0] // 2
        issue_gather(b, 1 - b)
        gate = _dot(lo, wg_ref[:half, :]) + _dot(hi, wg_ref[half:, :])
        up = _dot(lo, wu_ref[:half, :]) + _dot(hi, wu_ref[half:, :])
        issue_scatter(b, 1 - b)
        y = _dot((_silu(gate) * up).astype(BF16), wd_ref[...])
        ybuf[b] = _pack_pairs(_round_bf16(y)).reshape(ybuf.shape[1:])

        @pl.when(last_active)
        def _():
            own = dst_copy(g, 1 - b)
            own.start()
            own.wait()
            issue_scatter(1 - b, b)
            wait_scatter(b)
            wait_scatter(1 - b)
            wait_gather(1 - b)

    for b in range(2):
        @pl.when(jnp.logical_and(active, lax.rem(g, 2) == b))
        def _(b=b):
            step(b)


def _moe(tile_e, tile_nv, tok_tab, dst_tab, h2p, wg, wu, wd, tm):
    t, dh = h2p.shape
    ne, d, f = wg.shape
    n_tiles = tile_e.shape[0]
    n_rows = TOP_K * t + 2 * tm
    h3 = h2p.reshape(t // SUBLANES, SUBLANES, dh)
    buf = pltpu.VMEM((2, tm // SUBLANES, SUBLANES, dh), jnp.uint32)
    idx = pltpu.SMEM((2, 1, tm), jnp.int32)
    y3 = pl.pallas_call(
        _moe_kernel,
        grid_spec=pltpu.PrefetchScalarGridSpec(
            num_scalar_prefetch=2,
            grid=(n_tiles,),
            in_specs=[pl.BlockSpec(memory_space=pl.ANY),
                      pl.BlockSpec(memory_space=pl.ANY),
                      pl.BlockSpec(memory_space=pl.ANY),
                      pl.BlockSpec((None, d, f), lambda g, te, tn: (te[g], 0, 0)),
                      pl.BlockSpec((None, d, f), lambda g, te, tn: (te[g], 0, 0)),
                      pl.BlockSpec((None, f, d), lambda g, te, tn: (te[g], 0, 0))],
            out_specs=pl.BlockSpec(memory_space=pl.ANY),
            scratch_shapes=[buf, buf, idx, idx, pltpu.SemaphoreType.DMA((2,)), pltpu.SemaphoreType.DMA((2,)),
                            pltpu.SemaphoreType.DMA((2,))]),
        out_shape=jax.ShapeDtypeStruct((n_rows // SUBLANES, SUBLANES, dh), jnp.uint32),
        compiler_params=_params(("arbitrary",)),
    )(tile_e, tile_nv, tok_tab, dst_tab, h3, wg, wu, wd)
    return y3.reshape(n_rows, dh)


def _moe_tables(order, tile_row0, tile_nv, n_tok, tm):
    p = order.shape[0]
    n_tiles = tile_row0.shape[0]
    r = jnp.arange(tm, dtype=jnp.int32)[None, :]
    slot = order[jnp.minimum(tile_row0[:, None] + r, p - 1)]
    parity = (jnp.arange(n_tiles, dtype=jnp.int32) % 2)[:, None]
    dump = p + parity * tm + r
    dst = jnp.where(r < tile_nv[:, None], slot, dump)
    dst = jnp.concatenate([dst, p + tm + r], axis=0)
    return (slot & (n_tok - 1))[:, None, :], dst[:, None, :]


def _moe_schedule(e_idx, ne, tm):
    k, t = e_idx.shape
    p = k * t
    keys = e_idx.reshape(p)
    _, order = lax.sort((keys, jnp.arange(p, dtype=jnp.int32)), num_keys=1, is_stable=True)
    experts = jnp.arange(ne, dtype=jnp.int32)
    counts = jnp.sum((keys[None, :] == experts[:, None]).astype(jnp.int32), axis=1)
    starts = jnp.cumsum(counts) - counts
    ntile = (counts + tm - 1) // tm
    cum = jnp.cumsum(ntile)
    n_tiles = p // tm + ne
    gidx = jnp.arange(n_tiles, dtype=jnp.int32)
    te = jnp.minimum(jnp.sum((gidx[:, None] >= cum[None, :]).astype(jnp.int32), axis=1), ne - 1)
    onehot = (te[:, None] == experts[None, :]).astype(jnp.int32)
    pick = lambda v: jnp.sum(onehot * v[None, :], axis=1)
    local = gidx - pick(cum - ntile)
    nv = jnp.where(gidx < cum[-1], jnp.clip(pick(counts) - local * tm, 0, tm), 0)
    row0 = jnp.where(nv > 0, pick(starts) + local * tm, 0)
    return order, te, row0.astype(jnp.int32), nv.astype(jnp.int32)


def _combine_kernel(*refs):
    y_refs = refs[:TOP_K]
    w_ref, a_ref, wsd_ref, x_ref, gate_ref, o_ref = refs[TOP_K:]
    half = y_refs[0].shape[1]
    w = w_ref[...]
    lo, hi = _unpack_pairs(y_refs[0][...])
    acc_lo, acc_hi = w[:, 0:1] * lo, w[:, 0:1] * hi
    for k in range(1, TOP_K):
        lo, hi = _unpack_pairs(y_refs[k][...])
        acc_lo = acc_lo + w[:, k:k + 1] * lo
        acc_hi = acc_hi + w[:, k:k + 1] * hi
    a = a_ref[...]
    gate = gate_ref[0]
    o_ref[:, :half] = x_ref[:, :half] + gate[:, :half] * (acc_lo + _dot(a, wsd_ref[:, :half]))
    o_ref[:, half:] = x_ref[:, half:] + gate[:, half:] * (acc_hi + _dot(a, wsd_ref[:, half:]))


def _combine(y, w_tk, a_sh, wsd, x2, gate, batch):
    dh = y.shape[1]
    t, d = x2.shape
    kk = w_tk.shape[1]
    assert kk == TOP_K
    f = a_sh.shape[1]
    s = t // batch
    tm = _tile(s, 128)
    ns = s // tm
    nt = t // tm
    return pl.pallas_call(
        _combine_kernel,
        grid=(nt,),
        in_specs=[pl.BlockSpec((tm, dh), lambda i, k=k: (k * nt + i, 0)) for k in range(TOP_K)] + [
                  pl.BlockSpec((tm, kk), lambda i: (i, 0)),
                  pl.BlockSpec((tm, f), lambda i: (i, 0)),
                  pl.BlockSpec((f, d), lambda i: (0, 0)),
                  pl.BlockSpec((tm, d), lambda i: (i, 0)),
                  pl.BlockSpec((1, 1, d), lambda i: (i // ns, 0, 0))],
        out_specs=pl.BlockSpec((tm, d), lambda i: (i, 0)),
        out_shape=jax.ShapeDtypeStruct((t, d), F32),
        compiler_params=_params(("arbitrary",)),
    )(*([y] * TOP_K), w_tk, a_sh, wsd, x2, gate)


def _round_up(v, m):
    return (v + m - 1) // m * m


def kernel(x, c, positions, w_ada, b_ada, norm1_g, w_in, q_a_norm_g, w_uq, kv_a_norm_g, w_ukv, q_norm_g, k_norm_g, w_o, norm2_g, w_router, router_bias, w_exp_gate, w_exp_up, w_exp_down, w_sh_gate, w_sh_up, w_sh_down):
    b, s, d = x.shape
    t = b * s
    depth = w_ada.shape[0]
    qr = q_a_norm_g.shape[1]
    kvr = kv_a_norm_g.shape[1]
    nh = w_uq.shape[2] // MLA_QK
    in_w = w_in.shape[2]
    ret_heads = (in_w - qr - kvr - MLA_ROPE) // (2 * RET_DK + 2 * RET_DV)
    ret_w = ret_heads * RET_DK
    ne = w_router.shape[2]
    moe_tm = 256

    pos = positions.reshape(t).astype(F32)
    inv_r = ROPE_BASE ** (-jnp.arange(0, RET_DK, 2, dtype=F32) / RET_DK)
    ang_r = pos[:, None] * inv_r
    cos_r, sin_r = jnp.cos(ang_r), jnp.sin(ang_r)
    inv_m = ROPE_BASE ** (-jnp.arange(0, MLA_ROPE, 2, dtype=F32) / MLA_ROPE)
    ang_m = pos[:, None] * inv_m
    cm, sm = jnp.cos(ang_m), jnp.sin(ang_m)
    z = jnp.zeros_like(cm)
    mla_tabs = (jnp.concatenate([cm, cm, z, z], axis=-1),
                jnp.concatenate([-sm, z, z, z], axis=-1),
                jnp.concatenate([z, sm, z, z], axis=-1))
    log_g = jnp.log(1.0 - 2.0 ** (-5.0 - jnp.arange(ret_heads, dtype=F32)))

    c_pad = jnp.zeros((8, d), F32).at[:b].set(c)
    x2 = x.reshape(t, d)
    wm = _round_up(qr + LANES + kvr, 512)
    ret_cols = 4 * ret_w

    for l in range(depth):
        mod = _ada(c_pad, w_ada[l], b_ada[l][None, :])[:b]
        shift_a, scale_a, gate_a, shift_m, scale_m, gate_m = (
            m.reshape(b, 1, d) for m in jnp.split(mod, 6, axis=-1))

        wi = w_in[l]
        w_cq = wi[:, ret_cols:ret_cols + qr]
        w_ckv = wi[:, ret_cols + qr:ret_cols + qr + kvr]
        w_kr = wi[:, ret_cols + qr + kvr:]
        w_mla = jnp.concatenate(
            [w_cq, w_kr, jnp.zeros((d, LANES - MLA_ROPE), F32), w_ckv,
             jnp.zeros((d, wm - qr - LANES - kvr), F32)], axis=-1).astype(BF16)

        h1 = _norm_mod_call(x2, norm1_g[l][None, :], scale_a, shift_a, b, BF16)
        proj_ret = _mm_wcast(h1, w_in, l, ret_cols, BF16)
        proj = _mm(h1, w_mla, BF16)

        ret = _retention(proj_ret, cos_r, sin_r, log_g, b, ret_heads, 0)

        wq = jnp.pad(w_uq[l].reshape(qr, nh, MLA_QK), ((0, 0), (0, 0), (0, HEAD_PAD - MLA_QK)))
        wq = wq.reshape(qr, nh * HEAD_PAD).astype(BF16)
        wkv = w_ukv[l].reshape(kvr, nh, MLA_NOPE + MLA_V)
        wk = wkv[:, :, :MLA_NOPE].reshape(kvr, nh * MLA_NOPE).astype(BF16)
        wv = wkv[:, :, MLA_NOPE:].reshape(kvr, nh * MLA_V).astype(BF16)
        gq = jnp.pad(q_norm_g[l], (0, HEAD_PAD - MLA_QK))[None, :]
        gk = jnp.pad(k_norm_g[l], (0, HEAD_PAD - MLA_QK))[None, :]
        q, k, v = _mla_prep(proj, wm, mla_tabs, q_a_norm_g[l][None, :], kv_a_norm_g[l][None, :],
                            gq, gk, wq, wk, wv, qr, kvr, nh)
        att = _flash(q, k, v, b, nh)

        x2 = _wo(ret, att, w_o[l].astype(BF16), x2, gate_a, b)

        wr_t = w_router[l].T
        wrh = wr_t.astype(BF16)
        wrl = (wr_t - wrh.astype(F32)).astype(BF16)
        h2p, a_sh, e_idx, w_kt = _router(x2, norm2_g[l][None, :], scale_m, shift_m, wrh, wrl,
                                         router_bias[l][:, None], w_sh_gate[l].astype(BF16),
                                         w_sh_up[l].astype(BF16), b)
        order, tile_e, tile_row0, tile_nv = _moe_schedule(e_idx, ne, moe_tm)
        tok_tab, dst_tab = _moe_tables(order, tile_row0, tile_nv, t, moe_tm)
        y = _moe(tile_e, tile_nv, tok_tab, dst_tab, h2p, w_exp_gate[l].astype(BF16),
                 w_exp_up[l].astype(BF16), w_exp_down[l].astype(BF16), moe_tm)
        x2 = _combine(y, w_kt.T, a_sh, w_sh_down[l].astype(BF16), x2, gate_m, b)

    return x2.reshape(b, s, d)
```

```python
import functools
import math

import jax
import jax.numpy as jnp
from jax import lax
from jax.experimental import pallas as pl
from jax.experimental.pallas import tpu as pltpu

F32 = jnp.float32
BF16 = jnp.bfloat16

EPS = 1e-6
ROPE_BASE = 10000.0
RET_DK = 256
RET_DV = 256
MLA_NOPE = 128
MLA_ROPE = 64
MLA_V = 128
MLA_QK = MLA_NOPE + MLA_ROPE
N_GROUPS = 8
TOPK_GROUPS = 4
TOP_K = 8
ROUTED_SCALE = 2.5

LANES = 128
SUBLANES = 8
HEAD_PAD = 2 * LANES
VMEM_LIMIT = 56 * 1024 * 1024
ROW_UNROLL = 8


def _tile(dim, pref):
    t = min(dim, pref)
    while dim % t:
        t //= 2
    return t


def _params(sem):
    return pltpu.CompilerParams(dimension_semantics=sem, vmem_limit_bytes=VMEM_LIMIT)


def _silu(v):
    return v * jax.nn.sigmoid(v)


def _dot(a, b):
    return jnp.dot(a, b, preferred_element_type=F32)


def _dot_nt(a, b):
    return lax.dot_general(a, b, (((1,), (1,)), ((), ())), preferred_element_type=F32)


def _dot_tn(a, b):
    return lax.dot_general(a, b, (((0,), (0,)), ((), ())), preferred_element_type=F32)


HI_HALF = 0xFFFF0000


def _pack_pairs(v):
    half = v.shape[1] // 2
    bits = lax.bitcast_convert_type(v, jnp.uint32)
    return (bits[:, :half] >> 16) | (bits[:, half:] & jnp.uint32(HI_HALF))


def _unpack_pairs(w):
    lo = lax.bitcast_convert_type(w << 16, F32)
    hi = lax.bitcast_convert_type(w & jnp.uint32(HI_HALF), F32)
    return lo, hi


def _round_bf16(v):
    return v.astype(BF16).astype(F32)


def _store_slabs(ref, v):
    rows, n = v.shape
    per = n // LANES
    for s in range(per):
        ref[pl.ds(s, rows, stride=per), :] = v[:, s * LANES:(s + 1) * LANES]


def _load_slabs(ref, rows):
    per = ref.shape[0] // rows
    return jnp.concatenate([ref[pl.ds(s, rows, stride=per), :] for s in range(per)], axis=1)


def _ada_kernel(c_ref, w_ref, b_ref, o_ref):
    ca = _silu(c_ref[...]).astype(BF16)
    o_ref[...] = _dot(ca, w_ref[...].astype(BF16)) + b_ref[...]


def _ada(c_pad, w, b):
    m, d = c_pad.shape
    n = w.shape[1]
    tn = _tile(n, 512)
    return pl.pallas_call(
        _ada_kernel,
        grid=(n // tn,),
        in_specs=[pl.BlockSpec((m, d), lambda j: (0, 0)),
                  pl.BlockSpec((d, tn), lambda j: (0, j)),
                  pl.BlockSpec((1, tn), lambda j: (0, j))],
        out_specs=pl.BlockSpec((m, tn), lambda j: (0, j)),
        out_shape=jax.ShapeDtypeStruct((m, n), F32),
        compiler_params=_params(("arbitrary",)),
    )(c_pad, w, b)


def _norm_mod(x, g, scale, shift):
    xf = x.astype(F32)
    h = xf * lax.rsqrt(jnp.mean(xf * xf, axis=-1, keepdims=True) + EPS) * g
    return h * (1.0 + scale) + shift


def _norm_mod_kernel(x_ref, g_ref, sc_ref, sh_ref, o_ref):
    o_ref[...] = _norm_mod(x_ref[...], g_ref[...], sc_ref[0], sh_ref[0]).astype(o_ref.dtype)


def _norm_mod_call(x2, g, scale, shift, batch, out_dtype):
    t, d = x2.shape
    s = t // batch
    ts = _tile(s, 512)
    ns = s // ts
    return pl.pallas_call(
        _norm_mod_kernel,
        grid=(batch, ns),
        in_specs=[pl.BlockSpec((ts, d), lambda b, i: (b * ns + i, 0)),
                  pl.BlockSpec((1, d), lambda b, i: (0, 0)),
                  pl.BlockSpec((1, 1, d), lambda b, i: (b, 0, 0)),
                  pl.BlockSpec((1, 1, d), lambda b, i: (b, 0, 0))],
        out_specs=pl.BlockSpec((ts, d), lambda b, i: (b * ns + i, 0)),
        out_shape=jax.ShapeDtypeStruct((t, d), out_dtype),
        compiler_params=_params(("arbitrary", "arbitrary")),
    )(x2, g, scale, shift)


def _mm_kernel(a_ref, w_ref, o_ref):
    o_ref[...] = _dot(a_ref[...], w_ref[...]).astype(o_ref.dtype)


def _mm(a, w, out_dtype, tm_pref=1024, tn_pref=512):
    m, k = a.shape
    n = w.shape[1]
    tm, tn = _tile(m, tm_pref), _tile(n, tn_pref)
    return pl.pallas_call(
        _mm_kernel,
        grid=(m // tm, n // tn),
        in_specs=[pl.BlockSpec((tm, k), lambda i, j: (i, 0)),
                  pl.BlockSpec((k, tn), lambda i, j: (0, j))],
        out_specs=pl.BlockSpec((tm, tn), lambda i, j: (i, j)),
        out_shape=jax.ShapeDtypeStruct((m, n), out_dtype),
        compiler_params=_params(("arbitrary", "arbitrary")),
    )(a, w)


def _mm_wcast_kernel(a_ref, w_ref, o_ref, wb_ref):
    @pl.when(pl.program_id(1) == 0)
    def _():
        wb_ref[...] = w_ref[...].astype(wb_ref.dtype)

    o_ref[...] = _dot(a_ref[...], wb_ref[...]).astype(o_ref.dtype)


def _mm_wcast(a, w3, layer, n, out_dtype, tm_pref=1024, tn_pref=512):
    m, k = a.shape
    tm, tn = _tile(m, tm_pref), _tile(n, tn_pref)
    return pl.pallas_call(
        _mm_wcast_kernel,
        grid=(n // tn, m // tm),
        in_specs=[pl.BlockSpec((tm, k), lambda j, i: (i, 0)),
                  pl.BlockSpec((None, k, tn), lambda j, i: (layer, 0, j))],
        out_specs=pl.BlockSpec((tm, tn), lambda j, i: (i, j)),
        out_shape=jax.ShapeDtypeStruct((m, n), out_dtype),
        scratch_shapes=[pltpu.VMEM((k, tn), BF16)],
        compiler_params=_params(("arbitrary", "arbitrary")),
    )(a, w3)


def _ret_kernel(lg_ref, q_ref, k_ref, v_ref, g_ref, cos_ref, sin_ref, o_ref, state_ref):
    h = pl.program_id(1)

    @pl.when(pl.program_id(2) == 0)
    def _():
        state_ref[...] = jnp.zeros_like(state_ref)

    lg = lg_ref[h]
    c = q_ref.shape[0]
    cos, sin = cos_ref[...], sin_ref[...]
    half = RET_DK // 2

    def rope(v):
        v1, v2 = v[:, :half], v[:, half:]
        return jnp.concatenate([v1 * cos - v2 * sin, v2 * cos + v1 * sin], axis=-1)

    q = rope(q_ref[...].astype(F32))
    k = rope(k_ref[...].astype(F32)) * (RET_DK ** -0.5)
    v = v_ref[...]
    ii = lax.broadcasted_iota(jnp.int32, (c, c), 0)
    jj = lax.broadcasted_iota(jnp.int32, (c, c), 1)
    diff = (ii - jj).astype(F32)
    inner_decay = jnp.where(diff >= 0, jnp.exp(diff * lg), 0.0)
    ri = lax.broadcasted_iota(jnp.int32, (c, 1), 0).astype(F32)
    q_decay = jnp.exp((ri + 1.0) * lg)
    k_decay = jnp.exp((c - 1.0 - ri) * lg)
    chunk_decay = jnp.exp(jnp.full((1, 1), c, F32) * lg)

    qb = q.astype(BF16)
    sc = _dot_nt(qb, k.astype(BF16)) * inner_decay
    inner = _dot(sc.astype(BF16), v)
    state = state_ref[...]
    cross = _dot(qb, state.astype(BF16)) * q_decay
    kv = _dot_tn((k * k_decay).astype(BF16), v)
    state_ref[...] = state * chunk_decay + kv
    out = inner + cross
    r = out * lax.rsqrt(jnp.mean(out * out, axis=-1, keepdims=True) + EPS)
    o_ref[...] = (r * _silu(g_ref[...].astype(F32))).astype(o_ref.dtype)


def _retention(proj, cos_r, sin_r, log_g, batch, n_heads, col0):
    t = proj.shape[0]
    s = t // batch
    c = _tile(s, 256)
    n = s // c
    hh = n_heads

    def blk(off):
        return pl.BlockSpec((c, RET_DK), lambda b, h, i, lg: (b * n + i, col0 + off + h))

    tab = pl.BlockSpec((c, RET_DK // 2), lambda b, h, i, lg: (b * n + i, 0))
    return pl.pallas_call(
        _ret_kernel,
        grid_spec=pltpu.PrefetchScalarGridSpec(
            num_scalar_prefetch=1,
            grid=(batch, hh, n),
            in_specs=[blk(0), blk(hh), blk(2 * hh), blk(3 * hh), tab, tab],
            out_specs=pl.BlockSpec((c, RET_DV), lambda b, h, i, lg: (b * n + i, h)),
            scratch_shapes=[pltpu.VMEM((RET_DK, RET_DV), F32)]),
        out_shape=jax.ShapeDtypeStruct((t, hh * RET_DV), BF16),
        compiler_params=_params(("arbitrary", "arbitrary", "arbitrary")),
    )(log_g, proj, proj, proj, proj, cos_r, sin_r)


def _mla_prep_kernel(p_ref, ct_ref, sa_ref, sb_ref, gqa_ref, gkva_ref, gq_ref, gk_ref,
                     wq_ref, wk_ref, wv_ref, q_ref, k_ref, v_ref, *, qr, kvr, nh):
    p = p_ref[...].astype(F32)
    cq = p[:, :qr]
    krp = p[:, qr:qr + LANES]
    ckv = p[:, qr + LANES:qr + LANES + kvr]

    def rms(v, n):
        return v * lax.rsqrt(jnp.sum(v * v, axis=-1, keepdims=True) * (1.0 / n) + EPS)

    cqn = (rms(cq, qr) * gqa_ref[...]).astype(BF16)
    ckvn = (rms(ckv, kvr) * gkva_ref[...]).astype(BF16)
    qf = _dot(cqn, wq_ref[...])
    kn = _dot(ckvn, wk_ref[...])
    vf = _dot(ckvn, wv_ref[...])
    ones_col = (lax.broadcasted_iota(jnp.int32, (p.shape[0], LANES), 1) == 0).astype(v_ref.dtype)
    for h in range(nh):
        v_ref[:, h * HEAD_PAD:h * HEAD_PAD + MLA_V] = vf[:, h * MLA_V:(h + 1) * MLA_V].astype(v_ref.dtype)
        v_ref[:, h * HEAD_PAD + MLA_V:(h + 1) * HEAD_PAD] = ones_col

    ct, sa, sb = ct_ref[...], sa_ref[...], sb_ref[...]

    def rope(v):
        return v * ct + pltpu.roll(v, LANES - MLA_ROPE // 2, 1) * sa + pltpu.roll(v, MLA_ROPE // 2, 1) * sb

    gq, gk = gq_ref[...], gk_ref[...]
    kr_rot = rope(krp * gk[:, LANES:])
    kr_ss = jnp.sum(krp * krp, axis=-1, keepdims=True)
    qscale = 1.0 / math.sqrt(MLA_QK)
    for h in range(nh):
        qh = qf[:, h * HEAD_PAD:(h + 1) * HEAD_PAD]
        rq = lax.rsqrt(jnp.sum(qh * qh, axis=-1, keepdims=True) * (1.0 / MLA_QK) + EPS) * qscale
        qn = qh * rq * gq
        q_ref[:, h * HEAD_PAD:h * HEAD_PAD + LANES] = qn[:, :LANES].astype(q_ref.dtype)
        q_ref[:, h * HEAD_PAD + LANES:(h + 1) * HEAD_PAD] = rope(qn[:, LANES:]).astype(q_ref.dtype)
        knh = kn[:, h * MLA_NOPE:(h + 1) * MLA_NOPE]
        rk = lax.rsqrt((jnp.sum(knh * knh, axis=-1, keepdims=True) + kr_ss) * (1.0 / MLA_QK) + EPS)
        k_ref[:, h * HEAD_PAD:h * HEAD_PAD + LANES] = (knh * rk * gk[:, :LANES]).astype(k_ref.dtype)
        k_ref[:, h * HEAD_PAD + LANES:(h + 1) * HEAD_PAD] = (kr_rot * rk).astype(k_ref.dtype)


def _mla_prep(proj, wm, tabs, gqa, gkva, gq, gk, wq, wk, wv, qr, kvr, nh):
    t = proj.shape[0]
    tm = _tile(t, 256)
    full = lambda a: pl.BlockSpec(a.shape, lambda i: (0, 0))
    tab = pl.BlockSpec((tm, LANES), lambda i: (i, 0))
    return pl.pallas_call(
        functools.partial(_mla_prep_kernel, qr=qr, kvr=kvr, nh=nh),
        grid=(t // tm,),
        in_specs=[pl.BlockSpec((tm, wm), lambda i: (i, 0)), tab, tab, tab,
                  full(gqa), full(gkva), full(gq), full(gk), full(wq), full(wk), full(wv)],
        out_specs=[pl.BlockSpec((tm, nh * HEAD_PAD), lambda i: (i, 0)),
                   pl.BlockSpec((tm, nh * HEAD_PAD), lambda i: (i, 0)),
                   pl.BlockSpec((tm, nh * HEAD_PAD), lambda i: (i, 0))],
        out_shape=[jax.ShapeDtypeStruct((t, nh * HEAD_PAD), BF16),
                   jax.ShapeDtypeStruct((t, nh * HEAD_PAD), BF16),
                   jax.ShapeDtypeStruct((t, nh * HEAD_PAD), BF16)],
        compiler_params=_params(("arbitrary",)),
    )(proj, *tabs, gqa, gkva, gq, gk, wq, wk, wv)


def _flash_kernel(qi_ref, ki_ref, q_ref, k_ref, v_ref, o_ref, m_ref, acc_ref):
    p = pl.program_id(2)
    qi, ki = qi_ref[p], ki_ref[p]

    @pl.when(ki == 0)
    def _():
        m_ref[...] = jnp.full_like(m_ref, -jnp.inf)
        acc_ref[...] = jnp.zeros_like(acc_ref)

    def update(masked):
        blk = q_ref.shape[0]
        sub = min(blk, HEAD_PAD)
        for i in range(blk // sub):
            rows = slice(i * sub, (i + 1) * sub)
            nk = (i + 1) * sub if masked else blk
            s = _dot_nt(q_ref[rows, :], k_ref[:nk, :])
            if masked:
                row = lax.broadcasted_iota(jnp.int32, s.shape, 0) + i * sub
                col = lax.broadcasted_iota(jnp.int32, s.shape, 1)
                s = jnp.where(col <= row, s, -jnp.inf)
            m_prev = m_ref[rows, :]
            m_new = jnp.maximum(m_prev, jnp.max(s, axis=-1, keepdims=True))
            alpha = jnp.exp(m_prev - m_new)
            pe = jnp.exp((s - m_new).astype(BF16))
            acc_ref[rows, :] = alpha * acc_ref[rows, :] + _dot(pe, v_ref[:nk, :])
            m_ref[rows, :] = m_new

    @pl.when(ki < qi)
    def _():
        update(False)

    @pl.when(ki == qi)
    def _():
        update(True)
        acc = acc_ref[...]
        o_ref[...] = (acc[:, :MLA_V] / acc[:, MLA_V:MLA_V + 1]).astype(o_ref.dtype)


def _flash(q, k, v, batch, nh):
    t = q.shape[0]
    s = t // batch
    blk = _tile(s, 1024)
    nq = s // blk
    pairs = [(a, b) for a in range(nq) for b in range(a + 1)]
    qi = jnp.asarray([a for a, _ in pairs], jnp.int32)
    ki = jnp.asarray([b for _, b in pairs], jnp.int32)
    return pl.pallas_call(
        _flash_kernel,
        grid_spec=pltpu.PrefetchScalarGridSpec(
            num_scalar_prefetch=2,
            grid=(batch, nh, len(pairs)),
            in_specs=[pl.BlockSpec((blk, HEAD_PAD), lambda b, h, p, qi, ki: (b * nq + qi[p], h)),
                      pl.BlockSpec((blk, HEAD_PAD), lambda b, h, p, qi, ki: (b * nq + ki[p], h)),
                      pl.BlockSpec((blk, HEAD_PAD), lambda b, h, p, qi, ki: (b * nq + ki[p], h))],
            out_specs=pl.BlockSpec((blk, MLA_V), lambda b, h, p, qi, ki: (b * nq + qi[p], h)),
            scratch_shapes=[pltpu.VMEM((blk, 1), F32), pltpu.VMEM((blk, HEAD_PAD), F32)]),
        out_shape=jax.ShapeDtypeStruct((t, nh * MLA_V), BF16),
        compiler_params=_params(("arbitrary", "arbitrary", "arbitrary")),
    )(qi, ki, q, k, v)


def _wo_kernel(ret_ref, att_ref, w1_ref, w2_ref, x_ref, gate_ref, o_ref):
    mix = _dot(ret_ref[...], w1_ref[...]) + _dot(att_ref[...], w2_ref[...])
    o_ref[...] = x_ref[...] + gate_ref[0] * mix


def _wo(ret, att, w_o, x2, gate, batch):
    t, wr = ret.shape
    wa = att.shape[1]
    assert wr == wa
    d = w_o.shape[1]
    s = t // batch
    tm = _tile(s, 1024)
    tn = _tile(d, 512)
    ns = s // tm
    return pl.pallas_call(
        _wo_kernel,
        grid=(t // tm, d // tn),
        in_specs=[pl.BlockSpec((tm, wr), lambda i, j: (i, 0)),
                  pl.BlockSpec((tm, wa), lambda i, j: (i, 0)),
                  pl.BlockSpec((wr, tn), lambda i, j: (0, j)),
                  pl.BlockSpec((wa, tn), lambda i, j: (1, j)),
                  pl.BlockSpec((tm, tn), lambda i, j: (i, j)),
                  pl.BlockSpec((1, 1, tn), lambda i, j: (i // ns, 0, j))],
        out_specs=pl.BlockSpec((tm, tn), lambda i, j: (i, j)),
        out_shape=jax.ShapeDtypeStruct((t, d), F32),
        compiler_params=_params(("arbitrary", "arbitrary")),
    )(ret, att, w_o, w_o, x2, gate)


def _router_kernel(x_ref, g_ref, sc_ref, sh_ref, wrh_ref, wrl_ref, bias_ref, wsg_ref, wsu_ref,
                   h_ref, a_ref, e_ref, w_ref):
    h = _norm_mod(x_ref[...], g_ref[...], sc_ref[0], sh_ref[0])
    hb = h.astype(BF16)
    hbf = hb.astype(F32)
    _store_slabs(h_ref, _pack_pairs(hbf))
    a_ref[...] = (_silu(_dot(hb, wsg_ref[...])) * _dot(hb, wsu_ref[...])).astype(a_ref.dtype)

    hl = (h - hbf).astype(BF16)
    wrh = wrh_ref[...]
    logits = _dot_nt(wrh, hb) + _dot_nt(wrh, hl) + _dot_nt(wrl_ref[...], hb)
    scores = jax.nn.sigmoid(logits)
    sel = scores + bias_ref[...]
    ne, tm = sel.shape
    per_group = ne // N_GROUPS
    neg = -jnp.inf

    def first_argmax(v, iota, size):
        m = jnp.max(v, axis=0, keepdims=True)
        idx = jnp.min(jnp.where(v == m, iota, size), axis=0, keepdims=True)
        return m, idx

    iota_pg = lax.broadcasted_iota(jnp.int32, (per_group, tm), 0)
    gs = []
    for g in range(N_GROUPS):
        sg = sel[g * per_group:(g + 1) * per_group, :]
        m1, i1 = first_argmax(sg, iota_pg, per_group)
        m2 = jnp.max(jnp.where(iota_pg == i1, neg, sg), axis=0, keepdims=True)
        gs.append(m1 + m2)
    gsc = jnp.concatenate(gs, axis=0)
    iota_g = lax.broadcasted_iota(jnp.int32, (N_GROUPS, tm), 0)
    gmask = jnp.zeros((N_GROUPS, tm), jnp.bool_)
    for _ in range(TOPK_GROUPS):
        _, ig = first_argmax(gsc, iota_g, N_GROUPS)
        hit = iota_g == ig
        gmask = jnp.logical_or(gmask, hit)
        gsc = jnp.where(hit, neg, gsc)
    emask = jnp.concatenate(
        [jnp.broadcast_to(gmask[g:g + 1, :], (per_group, tm)) for g in range(N_GROUPS)], axis=0)
    cand = jnp.where(emask, sel, neg)
    iota_e = lax.broadcasted_iota(jnp.int32, (ne, tm), 0)
    es, ws = [], []
    for _ in range(TOP_K):
        _, ie = first_argmax(cand, iota_e, ne)
        hit = iota_e == ie
        es.append(ie)
        ws.append(jnp.sum(jnp.where(hit, scores, 0.0), axis=0, keepdims=True))
        cand = jnp.where(hit, neg, cand)
    wsum = ws[0]
    for wk in ws[1:]:
        wsum = wsum + wk
    e_ref[...] = jnp.concatenate(es, axis=0)
    w_ref[...] = jnp.concatenate(ws, axis=0) / wsum * ROUTED_SCALE


def _router(x2, g, scale, shift, wrh, wrl, bias, wsg, wsu, batch):
    t, d = x2.shape
    s = t // batch
    tm = _tile(s, 256)
    ns = s // tm
    ne = wrh.shape[0]
    f = wsg.shape[1]
    full = lambda a: pl.BlockSpec(a.shape, lambda i: (0,) * a.ndim)
    bat = pl.BlockSpec((1, 1, d), lambda i: (i // ns, 0, 0))
    return pl.pallas_call(
        _router_kernel,
        grid=(t // tm,),
        in_specs=[pl.BlockSpec((tm, d), lambda i: (i, 0)), full(g), bat, bat,
                  full(wrh), full(wrl), full(bias), full(wsg), full(wsu)],
        out_specs=[pl.BlockSpec((tm * (d // 2 // LANES), LANES), lambda i: (i, 0)),
                   pl.BlockSpec((tm, f), lambda i: (i, 0)),
                   pl.BlockSpec((TOP_K, tm), lambda i: (0, i)),
                   pl.BlockSpec((TOP_K, tm), lambda i: (0, i))],
        out_shape=[jax.ShapeDtypeStruct((t * (d // 2 // LANES), LANES), jnp.uint32),
                   jax.ShapeDtypeStruct((t, f), BF16),
                   jax.ShapeDtypeStruct((TOP_K, t), jnp.int32),
                   jax.ShapeDtypeStruct((TOP_K, t), F32)],
        compiler_params=_params(("arbitrary",)),
    )(x2, g, scale, shift, wrh, wrl, bias, wsg, wsu)


def _moe_kernel(te_ref, tn_ref, tok_tab, dst_tab, h_hbm, wg_ref, wu_ref, wd_ref, y_hbm,
                xbuf, ybuf, tok_idx, dst_idx, gsem, ssem, isem):
    g = pl.program_id(0)
    n_grid = pl.num_programs(0)
    tm = tok_idx.shape[2]
    per = xbuf.shape[1] // tm
    dump_row = dst_tab.shape[0] - 1
    nxt = jnp.minimum(g + 1, n_grid - 1)
    active = tn_ref[g] > 0
    next_active = jnp.logical_and(g + 1 < n_grid, tn_ref[nxt] > 0)
    last_active = jnp.logical_and(active, jnp.logical_not(next_active))

    def issue_gather(ib, b):
        for r in range(tm):
            src = pl.multiple_of(tok_idx[ib, 0, r], per)
            pltpu.make_async_copy(h_hbm.at[pl.ds(src, per), :], xbuf.at[b, pl.ds(r * per, per), :],
                                  gsem.at[b]).start()

    def issue_scatter(ib, b):
        for r in range(tm):
            dst = pl.multiple_of(dst_idx[ib, 0, r], per)
            pltpu.make_async_copy(ybuf.at[b, pl.ds(r * per, per), :], y_hbm.at[pl.ds(dst, per), :],
                                  ssem.at[b]).start()

    def wait_gather(b):
        pltpu.make_async_copy(h_hbm.at[pl.ds(0, tm * per), :], xbuf.at[b], gsem.at[b]).wait()

    def wait_scatter(b):
        pltpu.make_async_copy(ybuf.at[b], y_hbm.at[pl.ds(0, tm * per), :], ssem.at[b]).wait()

    def tok_copy(row, b):
        return pltpu.make_async_copy(tok_tab.at[row], tok_idx.at[b], isem.at[b])

    def dst_copy(row, b):
        return pltpu.make_async_copy(dst_tab.at[row], dst_idx.at[b], isem.at[b])

    @pl.when(g == 0)
    def _():
        ybuf[...] = jnp.zeros_like(ybuf)
        for b in range(2):
            dump = pltpu.make_async_copy(
                ybuf.at[b], y_hbm.at[pl.ds(y_hbm.shape[0] - (2 - b) * tm * per, tm * per), :], ssem.at[b])
            dump.start()
            dump.wait()
        first = tok_copy(0, 1)
        first.start()
        first.wait()
        issue_gather(1, 0)
        tok_copy(nxt, 0).start()
        dst_copy(dump_row, 0).start()

    def step(b):
        tok_copy(0, b).wait()
        dst_copy(0, b).wait()

        @pl.when(next_active)
        def _():
            tok_copy(jnp.minimum(g + 2, n_grid - 1), 1 - b).start()
            dst_copy(g, 1 - b).start()

        wait_gather(b)

        @pl.when(g >= 1)
        def _():
            wait_scatter(b)

        lo, hi = _unpack_pairs(_load_slabs(xbuf.at[b], tm))
        lo, hi = lo.astype(BF16), hi.astype(BF16)
        half = wg_ref.shape[0] // 2
        issue_gather(b, 1 - b)
        gate = _dot(lo, wg_ref[:half, :]) + _dot(hi, wg_ref[half:, :])
        up = _dot(lo, wu_ref[:half, :]) + _dot(hi, wu_ref[half:, :])
        issue_scatter(b, 1 - b)
        y = _dot((_silu(gate) * up).astype(BF16), wd_ref[...])
        _store_slabs(ybuf.at[b], _pack_pairs(_round_bf16(y)))

        @pl.when(last_active)
        def _():
            own = dst_copy(g, 1 - b)
            own.start()
            own.wait()
            issue_scatter(1 - b, b)
            wait_scatter(b)
            wait_scatter(1 - b)
            wait_gather(1 - b)

    for b in range(2):
        @pl.when(jnp.logical_and(active, lax.rem(g, 2) == b))
        def _(b=b):
            step(b)


def _moe(tile_e, tile_nv, tok_tab, dst_tab, h2s, wg, wu, wd, n_tok, tm):
    ne, d, f = wg.shape
    per = d // 2 // LANES
    n_tiles = tile_e.shape[0]
    n_rows = TOP_K * n_tok + 2 * tm
    buf = pltpu.VMEM((2, tm * per, LANES), jnp.uint32)
    idx = pltpu.SMEM((2, 1, tm), jnp.int32)
    return pl.pallas_call(
        _moe_kernel,
        grid_spec=pltpu.PrefetchScalarGridSpec(
            num_scalar_prefetch=2,
            grid=(n_tiles,),
            in_specs=[pl.BlockSpec(memory_space=pl.ANY),
                      pl.BlockSpec(memory_space=pl.ANY),
                      pl.BlockSpec(memory_space=pl.ANY),
                      pl.BlockSpec((None, d, f), lambda g, te, tn: (te[g], 0, 0)),
                      pl.BlockSpec((None, d, f), lambda g, te, tn: (te[g], 0, 0)),
                      pl.BlockSpec((None, f, d), lambda g, te, tn: (te[g], 0, 0))],
            out_specs=pl.BlockSpec(memory_space=pl.ANY),
            scratch_shapes=[buf, buf, idx, idx, pltpu.SemaphoreType.DMA((2,)), pltpu.SemaphoreType.DMA((2,)),
                            pltpu.SemaphoreType.DMA((2,))]),
        out_shape=jax.ShapeDtypeStruct((n_rows * per, LANES), jnp.uint32),
        compiler_params=_params(("arbitrary",)),
    )(tile_e, tile_nv, tok_tab, dst_tab, h2s, wg, wu, wd)


def _moe_tables(order, tile_row0, tile_nv, n_tok, tm, per):
    assert n_tok & (n_tok - 1) == 0, "token count must be a power of two (slot index packing)"
    p = order.shape[0]
    n_tiles = tile_row0.shape[0]
    r = jnp.arange(tm, dtype=jnp.int32)[None, :]
    slot = order[jnp.minimum(tile_row0[:, None] + r, p - 1)]
    parity = (jnp.arange(n_tiles, dtype=jnp.int32) % 2)[:, None]
    dump = p + parity * tm + r
    dst = jnp.where(r < tile_nv[:, None], slot, dump)
    dst = jnp.concatenate([dst, p + tm + r], axis=0)
    tok = slot & (n_tok - 1)
    return (tok * per)[:, None, :], (dst * per)[:, None, :]


def _moe_schedule(e_idx, ne, tm):
    k, t = e_idx.shape
    p = k * t
    keys = e_idx.reshape(p)
    _, order = lax.sort((keys, jnp.arange(p, dtype=jnp.int32)), num_keys=1, is_stable=True)
    experts = jnp.arange(ne, dtype=jnp.int32)
    counts = jnp.sum((keys[None, :] == experts[:, None]).astype(jnp.int32), axis=1)
    starts = jnp.cumsum(counts) - counts
    ntile = (counts + tm - 1) // tm
    cum = jnp.cumsum(ntile)
    n_tiles = p // tm + ne
    gidx = jnp.arange(n_tiles, dtype=jnp.int32)
    te = jnp.minimum(jnp.sum((gidx[:, None] >= cum[None, :]).astype(jnp.int32), axis=1), ne - 1)
    onehot = (te[:, None] == experts[None, :]).astype(jnp.int32)
    pick = lambda v: jnp.sum(onehot * v[None, :], axis=1)
    local = gidx - pick(cum - ntile)
    nv = jnp.where(gidx < cum[-1], jnp.clip(pick(counts) - local * tm, 0, tm), 0)
    row0 = jnp.where(nv > 0, pick(starts) + local * tm, 0)
    return order, te, row0.astype(jnp.int32), nv.astype(jnp.int32)


def _combine_kernel(*refs):
    y_refs = refs[:TOP_K]
    w_ref, a_ref, wsd_ref, x_ref, gate_ref, o_ref = refs[TOP_K:]
    tm = x_ref.shape[0]
    half = x_ref.shape[1] // 2
    w = w_ref[...]
    lo, hi = _unpack_pairs(_load_slabs(y_refs[0], tm))
    acc_lo, acc_hi = w[:, 0:1] * lo, w[:, 0:1] * hi
    for k in range(1, TOP_K):
        lo, hi = _unpack_pairs(_load_slabs(y_refs[k], tm))
        acc_lo = acc_lo + w[:, k:k + 1] * lo
        acc_hi = acc_hi + w[:, k:k + 1] * hi
    a = a_ref[...]
    gate = gate_ref[0]
    o_ref[:, :half] = x_ref[:, :half] + gate[:, :half] * (acc_lo + _dot(a, wsd_ref[:, :half]))
    o_ref[:, half:] = x_ref[:, half:] + gate[:, half:] * (acc_hi + _dot(a, wsd_ref[:, half:]))


def _combine(y, w_tk, a_sh, wsd, x2, gate, batch):
    t, d = x2.shape
    per = d // 2 // LANES
    kk = w_tk.shape[1]
    assert kk == TOP_K
    f = a_sh.shape[1]
    s = t // batch
    tm = _tile(s, 128)
    ns = s // tm
    nt = t // tm
    return pl.pallas_call(
        _combine_kernel,
        grid=(nt,),
        in_specs=[pl.BlockSpec((tm * per, LANES), lambda i, k=k: (k * nt + i, 0)) for k in range(TOP_K)] + [
                  pl.BlockSpec((tm, kk), lambda i: (i, 0)),
                  pl.BlockSpec((tm, f), lambda i: (i, 0)),
                  pl.BlockSpec((f, d), lambda i: (0, 0)),
                  pl.BlockSpec((tm, d), lambda i: (i, 0)),
                  pl.BlockSpec((1, 1, d), lambda i: (i // ns, 0, 0))],
        out_specs=pl.BlockSpec((tm, d), lambda i: (i, 0)),
        out_shape=jax.ShapeDtypeStruct((t, d), F32),
        compiler_params=_params(("arbitrary",)),
    )(*([y] * TOP_K), w_tk, a_sh, wsd, x2, gate)


def _round_up(v, m):
    return (v + m - 1) // m * m


def kernel(x, c, positions, w_ada, b_ada, norm1_g, w_in, q_a_norm_g, w_uq, kv_a_norm_g, w_ukv, q_norm_g, k_norm_g, w_o, norm2_g, w_router, router_bias, w_exp_gate, w_exp_up, w_exp_down, w_sh_gate, w_sh_up, w_sh_down):
    b, s, d = x.shape
    t = b * s
    depth = w_ada.shape[0]
    qr = q_a_norm_g.shape[1]
    kvr = kv_a_norm_g.shape[1]
    nh = w_uq.shape[2] // MLA_QK
    in_w = w_in.shape[2]
    ret_heads = (in_w - qr - kvr - MLA_ROPE) // (2 * RET_DK + 2 * RET_DV)
    ret_w = ret_heads * RET_DK
    ne = w_router.shape[2]
    moe_tm = 256

    pos = positions.reshape(t).astype(F32)
    inv_r = ROPE_BASE ** (-jnp.arange(0, RET_DK, 2, dtype=F32) / RET_DK)
    ang_r = pos[:, None] * inv_r
    cos_r, sin_r = jnp.cos(ang_r), jnp.sin(ang_r)
    inv_m = ROPE_BASE ** (-jnp.arange(0, MLA_ROPE, 2, dtype=F32) / MLA_ROPE)
    ang_m = pos[:, None] * inv_m
    cm, sm = jnp.cos(ang_m), jnp.sin(ang_m)
    z = jnp.zeros_like(cm)
    mla_tabs = (jnp.concatenate([cm, cm, z, z], axis=-1),
                jnp.concatenate([-sm, z, z, z], axis=-1),
                jnp.concatenate([z, sm, z, z], axis=-1))
    log_g = jnp.log(1.0 - 2.0 ** (-5.0 - jnp.arange(ret_heads, dtype=F32)))

    c_pad = jnp.zeros((8, d), F32).at[:b].set(c)
    x2 = x.reshape(t, d)
    wm = _round_up(qr + LANES + kvr, 512)
    ret_cols = 4 * ret_w

    for l in range(depth):
        mod = _ada(c_pad, w_ada[l], b_ada[l][None, :])[:b]
        shift_a, scale_a, gate_a, shift_m, scale_m, gate_m = (
            m.reshape(b, 1, d) for m in jnp.split(mod, 6, axis=-1))

        wi = w_in[l]
        w_cq = wi[:, ret_cols:ret_cols + qr]
        w_ckv = wi[:, ret_cols + qr:ret_cols + qr + kvr]
        w_kr = wi[:, ret_cols + qr + kvr:]
        w_mla = jnp.concatenate(
            [w_cq, w_kr, jnp.zeros((d, LANES - MLA_ROPE), F32), w_ckv,
             jnp.zeros((d, wm - qr - LANES - kvr), F32)], axis=-1).astype(BF16)

        h1 = _norm_mod_call(x2, norm1_g[l][None, :], scale_a, shift_a, b, BF16)
        proj_ret = _mm_wcast(h1, w_in, l, ret_cols, BF16)
        proj = _mm(h1, w_mla, BF16)

        ret = _retention(proj_ret, cos_r, sin_r, log_g, b, ret_heads, 0)

        wq = jnp.pad(w_uq[l].reshape(qr, nh, MLA_QK), ((0, 0), (0, 0), (0, HEAD_PAD - MLA_QK)))
        wq = wq.reshape(qr, nh * HEAD_PAD).astype(BF16)
        wkv = w_ukv[l].reshape(kvr, nh, MLA_NOPE + MLA_V)
        wk = wkv[:, :, :MLA_NOPE].reshape(kvr, nh * MLA_NOPE).astype(BF16)
        wv = wkv[:, :, MLA_NOPE:].reshape(kvr, nh * MLA_V).astype(BF16)
        gq = jnp.pad(q_norm_g[l], (0, HEAD_PAD - MLA_QK))[None, :]
        gk = jnp.pad(k_norm_g[l], (0, HEAD_PAD - MLA_QK))[None, :]
        q, k, v = _mla_prep(proj, wm, mla_tabs, q_a_norm_g[l][None, :], kv_a_norm_g[l][None, :],
                            gq, gk, wq, wk, wv, qr, kvr, nh)
        att = _flash(q, k, v, b, nh)

        x2 = _wo(ret, att, w_o[l].astype(BF16), x2, gate_a, b)

        wr_t = w_router[l].T
        wrh = wr_t.astype(BF16)
        wrl = (wr_t - wrh.astype(F32)).astype(BF16)
        h2p, a_sh, e_idx, w_kt = _router(x2, norm2_g[l][None, :], scale_m, shift_m, wrh, wrl,
                                         router_bias[l][:, None], w_sh_gate[l].astype(BF16),
                                         w_sh_up[l].astype(BF16), b)
        order, tile_e, tile_row0, tile_nv = _moe_schedule(e_idx, ne, moe_tm)
        tok_tab, dst_tab = _moe_tables(order, tile_row0, tile_nv, t, moe_tm, d // 2 // LANES)
        y = _moe(tile_e, tile_nv, tok_tab, dst_tab, h2p, w_exp_gate[l].astype(BF16),
                 w_exp_up[l].astype(BF16), w_exp_down[l].astype(BF16), t, moe_tm)
        x2 = _combine(y, w_kt.T, a_sh, w_sh_down[l].astype(BF16), x2, gate_m, b)

    return x2.reshape(b, s, d)
```

```python
import functools
import math

import jax
import jax.numpy as jnp
from jax import lax
from jax.experimental import pallas as pl
from jax.experimental.pallas import tpu as pltpu

F32 = jnp.float32
BF16 = jnp.bfloat16

EPS = 1e-6
ROPE_BASE = 10000.0
RET_DK = 256
RET_DV = 256
MLA_NOPE = 128
MLA_ROPE = 64
MLA_V = 128
MLA_QK = MLA_NOPE + MLA_ROPE
N_GROUPS = 8
TOPK_GROUPS = 4
TOP_K = 8
ROUTED_SCALE = 2.5

LANES = 128
SUBLANES = 8
HEAD_PAD = 2 * LANES
VMEM_LIMIT = 56 * 1024 * 1024
MOE_PHASES = 6


def _tile(dim, pref):
    t = min(dim, pref)
    while dim % t:
        t //= 2
    return t


def _params(sem):
    return pltpu.CompilerParams(dimension_semantics=sem, vmem_limit_bytes=VMEM_LIMIT)


def _silu(v):
    return v * jax.nn.sigmoid(v)


def _dot(a, b):
    return jnp.dot(a, b, preferred_element_type=F32)


def _dot_nt(a, b):
    return lax.dot_general(a, b, (((1,), (1,)), ((), ())), preferred_element_type=F32)


def _dot_tn(a, b):
    return lax.dot_general(a, b, (((0,), (0,)), ((), ())), preferred_element_type=F32)


HI_HALF = 0xFFFF0000


def _pack_pairs(v):
    half = v.shape[1] // 2
    bits = lax.bitcast_convert_type(v, jnp.uint32)
    return (bits[:, :half] >> 16) | (bits[:, half:] & jnp.uint32(HI_HALF))


def _unpack_pairs(w):
    lo = lax.bitcast_convert_type(w << 16, F32)
    hi = lax.bitcast_convert_type(w & jnp.uint32(HI_HALF), F32)
    return lo, hi


def _round_bf16(v):
    return v.astype(BF16).astype(F32)


def _store_slabs(ref, v):
    rows, n = v.shape
    per = n // LANES
    for s in range(per):
        ref[pl.ds(s, rows, stride=per), :] = v[:, s * LANES:(s + 1) * LANES]


def _load_slabs(ref, rows):
    per = ref.shape[0] // rows
    return jnp.concatenate([ref[pl.ds(s, rows, stride=per), :] for s in range(per)], axis=1)


def _ada_kernel(c_ref, w_ref, b_ref, o_ref):
    ca = _silu(c_ref[...]).astype(BF16)
    o_ref[...] = _dot(ca, w_ref[...].astype(BF16)) + b_ref[...]


def _ada(c_pad, w, b):
    m, d = c_pad.shape
    n = w.shape[1]
    tn = _tile(n, 512)
    return pl.pallas_call(
        _ada_kernel,
        grid=(n // tn,),
        in_specs=[pl.BlockSpec((m, d), lambda j: (0, 0)),
                  pl.BlockSpec((d, tn), lambda j: (0, j)),
                  pl.BlockSpec((1, tn), lambda j: (0, j))],
        out_specs=pl.BlockSpec((m, tn), lambda j: (0, j)),
        out_shape=jax.ShapeDtypeStruct((m, n), F32),
        compiler_params=_params(("arbitrary",)),
    )(c_pad, w, b)


def _norm_mod(x, g, scale, shift):
    xf = x.astype(F32)
    h = xf * lax.rsqrt(jnp.mean(xf * xf, axis=-1, keepdims=True) + EPS) * g
    return h * (1.0 + scale) + shift


def _norm_mod_kernel(x_ref, g_ref, sc_ref, sh_ref, o_ref):
    o_ref[...] = _norm_mod(x_ref[...], g_ref[...], sc_ref[0], sh_ref[0]).astype(o_ref.dtype)


def _norm_mod_call(x2, g, scale, shift, batch, out_dtype):
    t, d = x2.shape
    s = t // batch
    ts = _tile(s, 512)
    ns = s // ts
    return pl.pallas_call(
        _norm_mod_kernel,
        grid=(batch, ns),
        in_specs=[pl.BlockSpec((ts, d), lambda b, i: (b * ns + i, 0)),
                  pl.BlockSpec((1, d), lambda b, i: (0, 0)),
                  pl.BlockSpec((1, 1, d), lambda b, i: (b, 0, 0)),
                  pl.BlockSpec((1, 1, d), lambda b, i: (b, 0, 0))],
        out_specs=pl.BlockSpec((ts, d), lambda b, i: (b * ns + i, 0)),
        out_shape=jax.ShapeDtypeStruct((t, d), out_dtype),
        compiler_params=_params(("arbitrary", "arbitrary")),
    )(x2, g, scale, shift)


def _mm_kernel(a_ref, w_ref, o_ref):
    o_ref[...] = _dot(a_ref[...], w_ref[...]).astype(o_ref.dtype)


def _mm(a, w, out_dtype, tm_pref=1024, tn_pref=512):
    m, k = a.shape
    n = w.shape[1]
    tm, tn = _tile(m, tm_pref), _tile(n, tn_pref)
    return pl.pallas_call(
        _mm_kernel,
        grid=(m // tm, n // tn),
        in_specs=[pl.BlockSpec((tm, k), lambda i, j: (i, 0)),
                  pl.BlockSpec((k, tn), lambda i, j: (0, j))],
        out_specs=pl.BlockSpec((tm, tn), lambda i, j: (i, j)),
        out_shape=jax.ShapeDtypeStruct((m, n), out_dtype),
        compiler_params=_params(("arbitrary", "arbitrary")),
    )(a, w)


def _mm_wcast_kernel(a_ref, w_ref, o_ref, wb_ref):
    @pl.when(pl.program_id(1) == 0)
    def _():
        wb_ref[...] = w_ref[...].astype(wb_ref.dtype)

    o_ref[...] = _dot(a_ref[...], wb_ref[...]).astype(o_ref.dtype)


def _mm_wcast(a, w3, layer, n, out_dtype, tm_pref=1024, tn_pref=512):
    m, k = a.shape
    tm, tn = _tile(m, tm_pref), _tile(n, tn_pref)
    return pl.pallas_call(
        _mm_wcast_kernel,
        grid=(n // tn, m // tm),
        in_specs=[pl.BlockSpec((tm, k), lambda j, i: (i, 0)),
                  pl.BlockSpec((None, k, tn), lambda j, i: (layer, 0, j))],
        out_specs=pl.BlockSpec((tm, tn), lambda j, i: (i, j)),
        out_shape=jax.ShapeDtypeStruct((m, n), out_dtype),
        scratch_shapes=[pltpu.VMEM((k, tn), BF16)],
        compiler_params=_params(("arbitrary", "arbitrary")),
    )(a, w3)


def _ret_kernel(lg_ref, q_ref, k_ref, v_ref, g_ref, cos_ref, sin_ref, o_ref, state_ref):
    h = pl.program_id(1)

    @pl.when(pl.program_id(2) == 0)
    def _():
        state_ref[...] = jnp.zeros_like(state_ref)

    lg = lg_ref[h]
    c = q_ref.shape[0]
    cos, sin = cos_ref[...], sin_ref[...]
    half = RET_DK // 2

    def rope(v):
        v1, v2 = v[:, :half], v[:, half:]
        return jnp.concatenate([v1 * cos - v2 * sin, v2 * cos + v1 * sin], axis=-1)

    q = rope(q_ref[...].astype(F32))
    k = rope(k_ref[...].astype(F32)) * (RET_DK ** -0.5)
    v = v_ref[...]
    ii = lax.broadcasted_iota(jnp.int32, (c, c), 0)
    jj = lax.broadcasted_iota(jnp.int32, (c, c), 1)
    diff = (ii - jj).astype(F32)
    inner_decay = jnp.where(diff >= 0, jnp.exp(diff * lg), 0.0)
    ri = lax.broadcasted_iota(jnp.int32, (c, 1), 0).astype(F32)
    q_decay = jnp.exp((ri + 1.0) * lg)
    k_decay = jnp.exp((c - 1.0 - ri) * lg)
    chunk_decay = jnp.exp(jnp.full((1, 1), c, F32) * lg)

    qb = q.astype(BF16)
    sc = _dot_nt(qb, k.astype(BF16)) * inner_decay
    inner = _dot(sc.astype(BF16), v)
    state = state_ref[...]
    cross = _dot(qb, state.astype(BF16)) * q_decay
    kv = _dot_tn((k * k_decay).astype(BF16), v)
    state_ref[...] = state * chunk_decay + kv
    out = inner + cross
    r = out * lax.rsqrt(jnp.mean(out * out, axis=-1, keepdims=True) + EPS)
    o_ref[...] = (r * _silu(g_ref[...].astype(F32))).astype(o_ref.dtype)


def _retention(proj, cos_r, sin_r, log_g, batch, n_heads, col0):
    t = proj.shape[0]
    s = t // batch
    c = _tile(s, 512)
    n = s // c
    hh = n_heads

    def blk(off):
        return pl.BlockSpec((c, RET_DK), lambda b, h, i, lg: (b * n + i, col0 + off + h))

    tab = pl.BlockSpec((c, RET_DK // 2), lambda b, h, i, lg: (b * n + i, 0))
    return pl.pallas_call(
        _ret_kernel,
        grid_spec=pltpu.PrefetchScalarGridSpec(
            num_scalar_prefetch=1,
            grid=(batch, hh, n),
            in_specs=[blk(0), blk(hh), blk(2 * hh), blk(3 * hh), tab, tab],
            out_specs=pl.BlockSpec((c, RET_DV), lambda b, h, i, lg: (b * n + i, h)),
            scratch_shapes=[pltpu.VMEM((RET_DK, RET_DV), F32)]),
        out_shape=jax.ShapeDtypeStruct((t, hh * RET_DV), BF16),
        compiler_params=_params(("arbitrary", "arbitrary", "arbitrary")),
    )(log_g, proj, proj, proj, proj, cos_r, sin_r)


def _mla_prep_kernel(p_ref, ct_ref, sa_ref, sb_ref, gqa_ref, gkva_ref, gq_ref, gk_ref,
                     wq_ref, wk_ref, wv_ref, q_ref, k_ref, v_ref, *, qr, kvr, nh):
    p = p_ref[...].astype(F32)
    cq = p[:, :qr]
    krp = p[:, qr:qr + LANES]
    ckv = p[:, qr + LANES:qr + LANES + kvr]

    def rms(v, n):
        return v * lax.rsqrt(jnp.sum(v * v, axis=-1, keepdims=True) * (1.0 / n) + EPS)

    cqn = (rms(cq, qr) * gqa_ref[...]).astype(BF16)
    ckvn = (rms(ckv, kvr) * gkva_ref[...]).astype(BF16)
    qf = _dot(cqn, wq_ref[...])
    kn = _dot(ckvn, wk_ref[...])
    vf = _dot(ckvn, wv_ref[...])
    ones_col = (lax.broadcasted_iota(jnp.int32, (p.shape[0], LANES), 1) == 0).astype(v_ref.dtype)
    for h in range(nh):
        v_ref[:, h * HEAD_PAD:h * HEAD_PAD + MLA_V] = vf[:, h * MLA_V:(h + 1) * MLA_V].astype(v_ref.dtype)
        v_ref[:, h * HEAD_PAD + MLA_V:(h + 1) * HEAD_PAD] = ones_col

    ct, sa, sb = ct_ref[...], sa_ref[...], sb_ref[...]

    def rope(v):
        return v * ct + pltpu.roll(v, LANES - MLA_ROPE // 2, 1) * sa + pltpu.roll(v, MLA_ROPE // 2, 1) * sb

    gq, gk = gq_ref[...], gk_ref[...]
    kr_rot = rope(krp * gk[:, LANES:])
    kr_ss = jnp.sum(krp * krp, axis=-1, keepdims=True)
    qscale = 1.0 / math.sqrt(MLA_QK)
    for h in range(nh):
        qh = qf[:, h * HEAD_PAD:(h + 1) * HEAD_PAD]
        rq = lax.rsqrt(jnp.sum(qh * qh, axis=-1, keepdims=True) * (1.0 / MLA_QK) + EPS) * qscale
        qn = qh * rq * gq
        q_ref[:, h * HEAD_PAD:h * HEAD_PAD + LANES] = qn[:, :LANES].astype(q_ref.dtype)
        q_ref[:, h * HEAD_PAD + LANES:(h + 1) * HEAD_PAD] = rope(qn[:, LANES:]).astype(q_ref.dtype)
        knh = kn[:, h * MLA_NOPE:(h + 1) * MLA_NOPE]
        rk = lax.rsqrt((jnp.sum(knh * knh, axis=-1, keepdims=True) + kr_ss) * (1.0 / MLA_QK) + EPS)
        k_ref[:, h * HEAD_PAD:h * HEAD_PAD + LANES] = (knh * rk * gk[:, :LANES]).astype(k_ref.dtype)
        k_ref[:, h * HEAD_PAD + LANES:(h + 1) * HEAD_PAD] = (kr_rot * rk).astype(k_ref.dtype)


def _mla_prep(proj, wm, tabs, gqa, gkva, gq, gk, wq, wk, wv, qr, kvr, nh):
    t = proj.shape[0]
    tm = _tile(t, 256)
    full = lambda a: pl.BlockSpec(a.shape, lambda i: (0, 0))
    tab = pl.BlockSpec((tm, LANES), lambda i: (i, 0))
    return pl.pallas_call(
        functools.partial(_mla_prep_kernel, qr=qr, kvr=kvr, nh=nh),
        grid=(t // tm,),
        in_specs=[pl.BlockSpec((tm, wm), lambda i: (i, 0)), tab, tab, tab,
                  full(gqa), full(gkva), full(gq), full(gk), full(wq), full(wk), full(wv)],
        out_specs=[pl.BlockSpec((tm, nh * HEAD_PAD), lambda i: (i, 0)),
                   pl.BlockSpec((tm, nh * HEAD_PAD), lambda i: (i, 0)),
                   pl.BlockSpec((tm, nh * HEAD_PAD), lambda i: (i, 0))],
        out_shape=[jax.ShapeDtypeStruct((t, nh * HEAD_PAD), BF16),
                   jax.ShapeDtypeStruct((t, nh * HEAD_PAD), BF16),
                   jax.ShapeDtypeStruct((t, nh * HEAD_PAD), BF16)],
        compiler_params=_params(("arbitrary",)),
    )(proj, *tabs, gqa, gkva, gq, gk, wq, wk, wv)


def _flash_kernel(qi_ref, ki_ref, q_ref, k_ref, v_ref, o_ref, m_ref, acc_ref):
    p = pl.program_id(2)
    qi, ki = qi_ref[p], ki_ref[p]

    @pl.when(ki == 0)
    def _():
        m_ref[...] = jnp.full_like(m_ref, -jnp.inf)
        acc_ref[...] = jnp.zeros_like(acc_ref)

    def update(masked):
        blk = q_ref.shape[0]
        sub = min(blk, HEAD_PAD)
        for i in range(blk // sub):
            rows = slice(i * sub, (i + 1) * sub)
            nk = (i + 1) * sub if masked else blk
            s = _dot_nt(q_ref[rows, :], k_ref[:nk, :])
            if masked:
                row = lax.broadcasted_iota(jnp.int32, s.shape, 0) + i * sub
                col = lax.broadcasted_iota(jnp.int32, s.shape, 1)
                s = jnp.where(col <= row, s, -jnp.inf)
            m_prev = m_ref[rows, :]
            m_new = jnp.maximum(m_prev, jnp.max(s, axis=-1, keepdims=True))
            alpha = jnp.exp(m_prev - m_new)
            pe = jnp.exp((s - m_new).astype(BF16))
            acc_ref[rows, :] = alpha * acc_ref[rows, :] + _dot(pe, v_ref[:nk, :])
            m_ref[rows, :] = m_new

    @pl.when(ki < qi)
    def _():
        update(False)

    @pl.when(ki == qi)
    def _():
        update(True)
        acc = acc_ref[...]
        o_ref[...] = (acc[:, :MLA_V] / acc[:, MLA_V:MLA_V + 1]).astype(o_ref.dtype)


def _flash(q, k, v, batch, nh):
    t = q.shape[0]
    s = t // batch
    blk = _tile(s, 1024)
    nq = s // blk
    pairs = [(a, b) for a in range(nq) for b in range(a + 1)]
    qi = jnp.asarray([a for a, _ in pairs], jnp.int32)
    ki = jnp.asarray([b for _, b in pairs], jnp.int32)
    return pl.pallas_call(
        _flash_kernel,
        grid_spec=pltpu.PrefetchScalarGridSpec(
            num_scalar_prefetch=2,
            grid=(batch, nh, len(pairs)),
            in_specs=[pl.BlockSpec((blk, HEAD_PAD), lambda b, h, p, qi, ki: (b * nq + qi[p], h)),
                      pl.BlockSpec((blk, HEAD_PAD), lambda b, h, p, qi, ki: (b * nq + ki[p], h)),
                      pl.BlockSpec((blk, HEAD_PAD), lambda b, h, p, qi, ki: (b * nq + ki[p], h))],
            out_specs=pl.BlockSpec((blk, MLA_V), lambda b, h, p, qi, ki: (b * nq + qi[p], h)),
            scratch_shapes=[pltpu.VMEM((blk, 1), F32), pltpu.VMEM((blk, HEAD_PAD), F32)]),
        out_shape=jax.ShapeDtypeStruct((t, nh * MLA_V), BF16),
        compiler_params=_params(("arbitrary", "arbitrary", "arbitrary")),
    )(qi, ki, q, k, v)


def _wo_kernel(ret_ref, att_ref, w1_ref, w2_ref, x_ref, gate_ref, o_ref):
    mix = _dot(ret_ref[...], w1_ref[...]) + _dot(att_ref[...], w2_ref[...])
    o_ref[...] = x_ref[...] + gate_ref[0] * mix


def _wo(ret, att, w_o, x2, gate, batch):
    t, wr = ret.shape
    wa = att.shape[1]
    assert wr == wa
    d = w_o.shape[1]
    s = t // batch
    tm = _tile(s, 1024)
    tn = _tile(d, 512)
    ns = s // tm
    return pl.pallas_call(
        _wo_kernel,
        grid=(t // tm, d // tn),
        in_specs=[pl.BlockSpec((tm, wr), lambda i, j: (i, 0)),
                  pl.BlockSpec((tm, wa), lambda i, j: (i, 0)),
                  pl.BlockSpec((wr, tn), lambda i, j: (0, j)),
                  pl.BlockSpec((wa, tn), lambda i, j: (1, j)),
                  pl.BlockSpec((tm, tn), lambda i, j: (i, j)),
                  pl.BlockSpec((1, 1, tn), lambda i, j: (i // ns, 0, j))],
        out_specs=pl.BlockSpec((tm, tn), lambda i, j: (i, j)),
        out_shape=jax.ShapeDtypeStruct((t, d), F32),
        compiler_params=_params(("arbitrary", "arbitrary")),
    )(ret, att, w_o, w_o, x2, gate)


def _router_kernel(x_ref, g_ref, sc_ref, sh_ref, wrh_ref, wrl_ref, bias_ref, wsg_ref, wsu_ref,
                   h_ref, a_ref, e_ref, w_ref):
    h = _norm_mod(x_ref[...], g_ref[...], sc_ref[0], sh_ref[0])
    hb = h.astype(BF16)
    hbf = hb.astype(F32)
    _store_slabs(h_ref, _pack_pairs(hbf))
    a_ref[...] = (_silu(_dot(hb, wsg_ref[...])) * _dot(hb, wsu_ref[...])).astype(a_ref.dtype)

    hl = (h - hbf).astype(BF16)
    wrh = wrh_ref[...]
    logits = _dot_nt(wrh, hb) + _dot_nt(wrh, hl) + _dot_nt(wrl_ref[...], hb)
    scores = jax.nn.sigmoid(logits)
    sel = scores + bias_ref[...]
    ne, tm = sel.shape
    per_group = ne // N_GROUPS
    neg = -jnp.inf

    def first_argmax(v, iota, size):
        m = jnp.max(v, axis=0, keepdims=True)
        idx = jnp.min(jnp.where(v == m, iota, size), axis=0, keepdims=True)
        return m, idx

    iota_pg = lax.broadcasted_iota(jnp.int32, (per_group, tm), 0)
    gs = []
    for g in range(N_GROUPS):
        sg = sel[g * per_group:(g + 1) * per_group, :]
        m1, i1 = first_argmax(sg, iota_pg, per_group)
        m2 = jnp.max(jnp.where(iota_pg == i1, neg, sg), axis=0, keepdims=True)
        gs.append(m1 + m2)
    gsc = jnp.concatenate(gs, axis=0)
    iota_g = lax.broadcasted_iota(jnp.int32, (N_GROUPS, tm), 0)
    gmask = jnp.zeros((N_GROUPS, tm), jnp.bool_)
    for _ in range(TOPK_GROUPS):
        _, ig = first_argmax(gsc, iota_g, N_GROUPS)
        hit = iota_g == ig
        gmask = jnp.logical_or(gmask, hit)
        gsc = jnp.where(hit, neg, gsc)
    emask = jnp.concatenate(
        [jnp.broadcast_to(gmask[g:g + 1, :], (per_group, tm)) for g in range(N_GROUPS)], axis=0)
    cand = jnp.where(emask, sel, neg)
    iota_e = lax.broadcasted_iota(jnp.int32, (ne, tm), 0)
    es, ws = [], []
    for _ in range(TOP_K):
        _, ie = first_argmax(cand, iota_e, ne)
        hit = iota_e == ie
        es.append(ie)
        ws.append(jnp.sum(jnp.where(hit, scores, 0.0), axis=0, keepdims=True))
        cand = jnp.where(hit, neg, cand)
    wsum = ws[0]
    for wk in ws[1:]:
        wsum = wsum + wk
    e_ref[...] = jnp.concatenate(es, axis=0)
    w_ref[...] = jnp.concatenate(ws, axis=0) / wsum * ROUTED_SCALE


def _router(x2, g, scale, shift, wrh, wrl, bias, wsg, wsu, batch):
    t, d = x2.shape
    s = t // batch
    tm = _tile(s, 256)
    ns = s // tm
    ne = wrh.shape[0]
    f = wsg.shape[1]
    full = lambda a: pl.BlockSpec(a.shape, lambda i: (0,) * a.ndim)
    bat = pl.BlockSpec((1, 1, d), lambda i: (i // ns, 0, 0))
    return pl.pallas_call(
        _router_kernel,
        grid=(t // tm,),
        in_specs=[pl.BlockSpec((tm, d), lambda i: (i, 0)), full(g), bat, bat,
                  full(wrh), full(wrl), full(bias), full(wsg), full(wsu)],
        out_specs=[pl.BlockSpec((tm * (d // 2 // LANES), LANES), lambda i: (i, 0)),
                   pl.BlockSpec((tm, f), lambda i: (i, 0)),
                   pl.BlockSpec((TOP_K, tm), lambda i: (0, i)),
                   pl.BlockSpec((TOP_K, tm), lambda i: (0, i))],
        out_shape=[jax.ShapeDtypeStruct((t * (d // 2 // LANES), LANES), jnp.uint32),
                   jax.ShapeDtypeStruct((t, f), BF16),
                   jax.ShapeDtypeStruct((TOP_K, t), jnp.int32),
                   jax.ShapeDtypeStruct((TOP_K, t), F32)],
        compiler_params=_params(("arbitrary",)),
    )(x2, g, scale, shift, wrh, wrl, bias, wsg, wsu)


def _moe_kernel(te_ref, tn_ref, tok_tab, dst_tab, h_hbm, w_ref, wg_ref, wu_ref, wd_ref, y_hbm,
                xbuf, ybuf, xlo, xhi, gacc, uacc, act, ylo, tok_idx, dst_idx, gsem, ssem, isem):
    g = pl.program_id(0)
    n_grid = pl.num_programs(0)
    tm = tok_idx.shape[2]
    per = xbuf.shape[1] // tm
    dump_row = dst_tab.shape[0] - 1
    nxt = jnp.minimum(g + 1, n_grid - 1)
    active = tn_ref[g] > 0
    next_active = jnp.logical_and(g + 1 < n_grid, tn_ref[nxt] > 0)
    last_active = jnp.logical_and(active, jnp.logical_not(next_active))

    def rows_of(phase):
        if phase is None:
            return range(tm)
        return range(phase * tm // MOE_PHASES, (phase + 1) * tm // MOE_PHASES)

    def issue_gather(ib, b, phase=None):
        for r in rows_of(phase):
            src = pl.multiple_of(tok_idx[ib, 0, r], per)
            pltpu.make_async_copy(h_hbm.at[pl.ds(src, per), :], xbuf.at[b, pl.ds(r * per, per), :],
                                  gsem.at[b]).start()

    def issue_scatter(ib, b, phase=None):
        for r in rows_of(phase):
            dst = pl.multiple_of(dst_idx[ib, 0, r], per)
            pltpu.make_async_copy(ybuf.at[b, pl.ds(r * per, per), :], y_hbm.at[pl.ds(dst, per), :],
                                  ssem.at[b]).start()

    def wait_gather(b):
        pltpu.make_async_copy(h_hbm.at[pl.ds(0, tm * per), :], xbuf.at[b], gsem.at[b]).wait()

    def wait_scatter(b):
        pltpu.make_async_copy(ybuf.at[b], y_hbm.at[pl.ds(0, tm * per), :], ssem.at[b]).wait()

    def tok_copy(row, b):
        return pltpu.make_async_copy(tok_tab.at[row], tok_idx.at[b], isem.at[b])

    def dst_copy(row, b):
        return pltpu.make_async_copy(dst_tab.at[row], dst_idx.at[b], isem.at[b])

    @pl.when(g == 0)
    def _():
        ybuf[...] = jnp.zeros_like(ybuf)
        for b in range(2):
            dump = pltpu.make_async_copy(
                ybuf.at[b], y_hbm.at[pl.ds(y_hbm.shape[0] - (2 - b) * tm * per, tm * per), :], ssem.at[b])
            dump.start()
            dump.wait()
        first = tok_copy(0, 1)
        first.start()
        first.wait()
        issue_gather(1, 0)
        tok_copy(nxt, 0).start()
        dst_copy(dump_row, 0).start()

    def step(b):
        tok_copy(0, b).wait()
        dst_copy(0, b).wait()

        @pl.when(next_active)
        def _():
            tok_copy(jnp.minimum(g + 2, n_grid - 1), 1 - b).start()
            dst_copy(g, 1 - b).start()

        wait_gather(b)

        @pl.when(g >= 1)
        def _():
            wait_scatter(b)

        half = wg_ref.shape[0] // 2
        on = tn_ref[g] > 0

        def phase(i, body):
            def run():
                issue_gather(b, 1 - b, i)
                issue_scatter(b, 1 - b, i)
                body()
            if i == 0:
                run()
            else:
                pl.when(on)(run)

        def p0():
            lo, hi = _unpack_pairs(_load_slabs(xbuf.at[b], tm))
            xlo[...] = lo.astype(BF16)
            xhi[...] = hi.astype(BF16)
            gacc[...] = _dot(xlo[...], wg_ref[:half, :])

        def p1():
            gacc[...] += _dot(xhi[...], wg_ref[half:, :])

        def p2():
            uacc[...] = _dot(xlo[...], wu_ref[:half, :])

        def p3():
            up = uacc[...] + _dot(xhi[...], wu_ref[half:, :])
            wcol = jnp.transpose(w_ref[...])[:, 0:1]
            act[...] = (_silu(gacc[...]) * up * wcol).astype(BF16)

        def p4():
            y_lo = _round_bf16(_dot(act[...], wd_ref[:, :half]))
            ylo[...] = lax.bitcast_convert_type(y_lo, jnp.uint32) >> 16

        def p5():
            y_hi = _round_bf16(_dot(act[...], wd_ref[:, half:]))
            _store_slabs(ybuf.at[b], ylo[...] | (lax.bitcast_convert_type(y_hi, jnp.uint32) & jnp.uint32(HI_HALF)))

        for i, body in enumerate((p0, p1, p2, p3, p4, p5)):
            phase(i, body)

        @pl.when(last_active)
        def _():
            own = dst_copy(g, 1 - b)
            own.start()
            own.wait()
            issue_scatter(1 - b, b)
            wait_scatter(b)
            wait_scatter(1 - b)
            wait_gather(1 - b)

    for b in range(2):
        @pl.when(jnp.logical_and(active, lax.rem(g, 2) == b))
        def _(b=b):
            step(b)


def _moe(tile_e, tile_nv, tok_tab, dst_tab, w_tab, h2s, wg, wu, wd, n_tok, tm):
    ne, d, f = wg.shape
    per = d // 2 // LANES
    n_tiles = tile_e.shape[0]
    n_rows = TOP_K * n_tok + 2 * tm
    buf = pltpu.VMEM((2, tm * per, LANES), jnp.uint32)
    idx = pltpu.SMEM((2, 1, tm), jnp.int32)
    return pl.pallas_call(
        _moe_kernel,
        grid_spec=pltpu.PrefetchScalarGridSpec(
            num_scalar_prefetch=2,
            grid=(n_tiles,),
            in_specs=[pl.BlockSpec(memory_space=pl.ANY),
                      pl.BlockSpec(memory_space=pl.ANY),
                      pl.BlockSpec(memory_space=pl.ANY),
                      pl.BlockSpec((None, SUBLANES, tm), lambda g, te, tn: (g, 0, 0)),
                      pl.BlockSpec((None, d, f), lambda g, te, tn: (te[g], 0, 0)),
                      pl.BlockSpec((None, d, f), lambda g, te, tn: (te[g], 0, 0)),
                      pl.BlockSpec((None, f, d), lambda g, te, tn: (te[g], 0, 0))],
            out_specs=pl.BlockSpec(memory_space=pl.ANY),
            scratch_shapes=[buf, buf,
                            pltpu.VMEM((tm, d // 2), BF16), pltpu.VMEM((tm, d // 2), BF16),
                            pltpu.VMEM((tm, f), F32), pltpu.VMEM((tm, f), F32), pltpu.VMEM((tm, f), BF16),
                            pltpu.VMEM((tm, d // 2), jnp.uint32),
                            idx, idx, pltpu.SemaphoreType.DMA((2,)), pltpu.SemaphoreType.DMA((2,)),
                            pltpu.SemaphoreType.DMA((2,))]),
        out_shape=jax.ShapeDtypeStruct((n_rows * per, LANES), jnp.uint32),
        compiler_params=_params(("arbitrary",)),
    )(tile_e, tile_nv, tok_tab, dst_tab, h2s, w_tab, wg, wu, wd)


def _moe_tables(order, tile_row0, tile_nv, w_flat, n_tok, tm, per):
    assert n_tok & (n_tok - 1) == 0, "token count must be a power of two (slot index packing)"
    p = order.shape[0]
    n_tiles = tile_row0.shape[0]
    r = jnp.arange(tm, dtype=jnp.int32)[None, :]
    slot = order[jnp.minimum(tile_row0[:, None] + r, p - 1)]
    parity = (jnp.arange(n_tiles, dtype=jnp.int32) % 2)[:, None]
    dump = p + parity * tm + r
    dst = jnp.where(r < tile_nv[:, None], slot, dump)
    dst = jnp.concatenate([dst, p + tm + r], axis=0)
    tok = slot & (n_tok - 1)
    w_tab = jnp.broadcast_to(w_flat[slot][:, None, :], (n_tiles, SUBLANES, tm))
    return (tok * per)[:, None, :], (dst * per)[:, None, :], w_tab


def _moe_schedule(e_idx, ne, tm):
    k, t = e_idx.shape
    p = k * t
    keys = e_idx.reshape(p)
    _, order = lax.sort((keys, jnp.arange(p, dtype=jnp.int32)), num_keys=1, is_stable=True)
    experts = jnp.arange(ne, dtype=jnp.int32)
    counts = jnp.sum((keys[None, :] == experts[:, None]).astype(jnp.int32), axis=1)
    starts = jnp.cumsum(counts) - counts
    ntile = (counts + tm - 1) // tm
    cum = jnp.cumsum(ntile)
    n_tiles = p // tm + ne
    gidx = jnp.arange(n_tiles, dtype=jnp.int32)
    te = jnp.minimum(jnp.sum((gidx[:, None] >= cum[None, :]).astype(jnp.int32), axis=1), ne - 1)
    onehot = (te[:, None] == experts[None, :]).astype(jnp.int32)
    pick = lambda v: jnp.sum(onehot * v[None, :], axis=1)
    local = gidx - pick(cum - ntile)
    nv = jnp.where(gidx < cum[-1], jnp.clip(pick(counts) - local * tm, 0, tm), 0)
    row0 = jnp.where(nv > 0, pick(starts) + local * tm, 0)
    return order, te, row0.astype(jnp.int32), nv.astype(jnp.int32)


def _combine_kernel(*refs):
    y_refs = refs[:TOP_K]
    a_ref, wsd_ref, x_ref, gate_ref, o_ref = refs[TOP_K:]
    tm = x_ref.shape[0]
    half = x_ref.shape[1] // 2
    acc_lo, acc_hi = _unpack_pairs(_load_slabs(y_refs[0], tm))
    for k in range(1, TOP_K):
        lo, hi = _unpack_pairs(_load_slabs(y_refs[k], tm))
        acc_lo = acc_lo + lo
        acc_hi = acc_hi + hi
    a = a_ref[...]
    gate = gate_ref[0]
    o_ref[:, :half] = x_ref[:, :half] + gate[:, :half] * (acc_lo + _dot(a, wsd_ref[:, :half]))
    o_ref[:, half:] = x_ref[:, half:] + gate[:, half:] * (acc_hi + _dot(a, wsd_ref[:, half:]))


def _combine(y, a_sh, wsd, x2, gate, batch):
    t, d = x2.shape
    per = d // 2 // LANES
    f = a_sh.shape[1]
    s = t // batch
    tm = _tile(s, 128)
    ns = s // tm
    nt = t // tm
    return pl.pallas_call(
        _combine_kernel,
        grid=(nt,),
        in_specs=[pl.BlockSpec((tm * per, LANES), lambda i, k=k: (k * nt + i, 0)) for k in range(TOP_K)] + [
                  pl.BlockSpec((tm, f), lambda i: (i, 0)),
                  pl.BlockSpec((f, d), lambda i: (0, 0)),
                  pl.BlockSpec((tm, d), lambda i: (i, 0)),
                  pl.BlockSpec((1, 1, d), lambda i: (i // ns, 0, 0))],
        out_specs=pl.BlockSpec((tm, d), lambda i: (i, 0)),
        out_shape=jax.ShapeDtypeStruct((t, d), F32),
        compiler_params=_params(("arbitrary",)),
    )(*([y] * TOP_K), a_sh, wsd, x2, gate)


def _round_up(v, m):
    return (v + m - 1) // m * m


def kernel(x, c, positions, w_ada, b_ada, norm1_g, w_in, q_a_norm_g, w_uq, kv_a_norm_g, w_ukv, q_norm_g, k_norm_g, w_o, norm2_g, w_router, router_bias, w_exp_gate, w_exp_up, w_exp_down, w_sh_gate, w_sh_up, w_sh_down):
    b, s, d = x.shape
    t = b * s
    depth = w_ada.shape[0]
    qr = q_a_norm_g.shape[1]
    kvr = kv_a_norm_g.shape[1]
    nh = w_uq.shape[2] // MLA_QK
    in_w = w_in.shape[2]
    ret_heads = (in_w - qr - kvr - MLA_ROPE) // (2 * RET_DK + 2 * RET_DV)
    ret_w = ret_heads * RET_DK
    ne = w_router.shape[2]
    moe_tm = 256

    pos = positions.reshape(t).astype(F32)
    inv_r = ROPE_BASE ** (-jnp.arange(0, RET_DK, 2, dtype=F32) / RET_DK)
    ang_r = pos[:, None] * inv_r
    cos_r, sin_r = jnp.cos(ang_r), jnp.sin(ang_r)
    inv_m = ROPE_BASE ** (-jnp.arange(0, MLA_ROPE, 2, dtype=F32) / MLA_ROPE)
    ang_m = pos[:, None] * inv_m
    cm, sm = jnp.cos(ang_m), jnp.sin(ang_m)
    z = jnp.zeros_like(cm)
    mla_tabs = (jnp.concatenate([cm, cm, z, z], axis=-1),
                jnp.concatenate([-sm, z, z, z], axis=-1),
                jnp.concatenate([z, sm, z, z], axis=-1))
    log_g = jnp.log(1.0 - 2.0 ** (-5.0 - jnp.arange(ret_heads, dtype=F32)))

    c_pad = jnp.zeros((8, d), F32).at[:b].set(c)
    x2 = x.reshape(t, d)
    wm = _round_up(qr + LANES + kvr, 512)
    ret_cols = 4 * ret_w

    for l in range(depth):
        mod = _ada(c_pad, w_ada[l], b_ada[l][None, :])[:b]
        shift_a, scale_a, gate_a, shift_m, scale_m, gate_m = (
            m.reshape(b, 1, d) for m in jnp.split(mod, 6, axis=-1))

        wi = w_in[l]
        w_cq = wi[:, ret_cols:ret_cols + qr]
        w_ckv = wi[:, ret_cols + qr:ret_cols + qr + kvr]
        w_kr = wi[:, ret_cols + qr + kvr:]
        w_mla = jnp.concatenate(
            [w_cq, w_kr, jnp.zeros((d, LANES - MLA_ROPE), F32), w_ckv,
             jnp.zeros((d, wm - qr - LANES - kvr), F32)], axis=-1).astype(BF16)

        h1 = _norm_mod_call(x2, norm1_g[l][None, :], scale_a, shift_a, b, BF16)
        proj_ret = _mm_wcast(h1, w_in, l, ret_cols, BF16)
        proj = _mm(h1, w_mla, BF16)

        ret = _retention(proj_ret, cos_r, sin_r, log_g, b, ret_heads, 0)

        wq = jnp.pad(w_uq[l].reshape(qr, nh, MLA_QK), ((0, 0), (0, 0), (0, HEAD_PAD - MLA_QK)))
        wq = wq.reshape(qr, nh * HEAD_PAD).astype(BF16)
        wkv = w_ukv[l].reshape(kvr, nh, MLA_NOPE + MLA_V)
        wk = wkv[:, :, :MLA_NOPE].reshape(kvr, nh * MLA_NOPE).astype(BF16)
        wv = wkv[:, :, MLA_NOPE:].reshape(kvr, nh * MLA_V).astype(BF16)
        gq = jnp.pad(q_norm_g[l], (0, HEAD_PAD - MLA_QK))[None, :]
        gk = jnp.pad(k_norm_g[l], (0, HEAD_PAD - MLA_QK))[None, :]
        q, k, v = _mla_prep(proj, wm, mla_tabs, q_a_norm_g[l][None, :], kv_a_norm_g[l][None, :],
                            gq, gk, wq, wk, wv, qr, kvr, nh)
        att = _flash(q, k, v, b, nh)

        x2 = _wo(ret, att, w_o[l].astype(BF16), x2, gate_a, b)

        wr_t = w_router[l].T
        wrh = wr_t.astype(BF16)
        wrl = (wr_t - wrh.astype(F32)).astype(BF16)
        h2p, a_sh, e_idx, w_kt = _router(x2, norm2_g[l][None, :], scale_m, shift_m, wrh, wrl,
                                         router_bias[l][:, None], w_sh_gate[l].astype(BF16),
                                         w_sh_up[l].astype(BF16), b)
        order, tile_e, tile_row0, tile_nv = _moe_schedule(e_idx, ne, moe_tm)
        tok_tab, dst_tab, w_tab = _moe_tables(order, tile_row0, tile_nv, w_kt.reshape(-1), t, moe_tm,
                                              d // 2 // LANES)
        y = _moe(tile_e, tile_nv, tok_tab, dst_tab, w_tab, h2p, w_exp_gate[l].astype(BF16),
                 w_exp_up[l].astype(BF16), w_exp_down[l].astype(BF16), t, moe_tm)
        x2 = _combine(y, a_sh, w_sh_down[l].astype(BF16), x2, gate_m, b)

    return x2.reshape(b, s, d)
```

```python
import functools
import math

import jax
import jax.numpy as jnp
from jax import lax
from jax.experimental import pallas as pl
from jax.experimental.pallas import tpu as pltpu

F32 = jnp.float32
BF16 = jnp.bfloat16

EPS = 1e-6
ROPE_BASE = 10000.0
RET_DK = 256
RET_DV = 256
MLA_NOPE = 128
MLA_ROPE = 64
MLA_V = 128
MLA_QK = MLA_NOPE + MLA_ROPE
N_GROUPS = 8
TOPK_GROUPS = 4
TOP_K = 8
ROUTED_SCALE = 2.5

LANES = 128
SUBLANES = 8
HEAD_PAD = 2 * LANES
VMEM_LIMIT = 56 * 1024 * 1024


def _tile(dim, pref):
    t = min(dim, pref)
    while dim % t:
        t //= 2
    return t


def _params(sem):
    return pltpu.CompilerParams(dimension_semantics=sem, vmem_limit_bytes=VMEM_LIMIT)


def _silu(v):
    return v * jax.nn.sigmoid(v)


def _dot(a, b):
    return jnp.dot(a, b, preferred_element_type=F32)


def _dot_nt(a, b):
    return lax.dot_general(a, b, (((1,), (1,)), ((), ())), preferred_element_type=F32)


def _dot_tn(a, b):
    return lax.dot_general(a, b, (((0,), (0,)), ((), ())), preferred_element_type=F32)


HI_HALF = 0xFFFF0000


def _pack_pairs(v):
    half = v.shape[1] // 2
    bits = lax.bitcast_convert_type(v, jnp.uint32)
    return (bits[:, :half] >> 16) | (bits[:, half:] & jnp.uint32(HI_HALF))


def _unpack_pairs(w):
    lo = lax.bitcast_convert_type(w << 16, F32)
    hi = lax.bitcast_convert_type(w & jnp.uint32(HI_HALF), F32)
    return lo, hi


def _round_bf16(v):
    return v.astype(BF16).astype(F32)


def _store_slabs(ref, v):
    rows, n = v.shape
    per = n // LANES
    for s in range(per):
        ref[pl.ds(s, rows, stride=per), :] = v[:, s * LANES:(s + 1) * LANES]


def _load_slabs(ref, rows):
    per = ref.shape[0] // rows
    return jnp.concatenate([ref[pl.ds(s, rows, stride=per), :] for s in range(per)], axis=1)


def _ada_kernel(c_ref, w_ref, b_ref, o_ref):
    ca = _silu(c_ref[...]).astype(BF16)
    o_ref[...] = _dot(ca, w_ref[...].astype(BF16)) + b_ref[...]


def _ada(c_pad, w, b):
    m, d = c_pad.shape
    n = w.shape[1]
    tn = _tile(n, 512)
    return pl.pallas_call(
        _ada_kernel,
        grid=(n // tn,),
        in_specs=[pl.BlockSpec((m, d), lambda j: (0, 0)),
                  pl.BlockSpec((d, tn), lambda j: (0, j)),
                  pl.BlockSpec((1, tn), lambda j: (0, j))],
        out_specs=pl.BlockSpec((m, tn), lambda j: (0, j)),
        out_shape=jax.ShapeDtypeStruct((m, n), F32),
        compiler_params=_params(("arbitrary",)),
    )(c_pad, w, b)


def _norm_mod(x, g, scale, shift):
    xf = x.astype(F32)
    h = xf * lax.rsqrt(jnp.mean(xf * xf, axis=-1, keepdims=True) + EPS) * g
    return h * (1.0 + scale) + shift


def _norm_mod_kernel(x_ref, g_ref, sc_ref, sh_ref, o_ref):
    o_ref[...] = _norm_mod(x_ref[...], g_ref[...], sc_ref[0], sh_ref[0]).astype(o_ref.dtype)


def _norm_mod_call(x2, g, scale, shift, batch, out_dtype):
    t, d = x2.shape
    s = t // batch
    ts = _tile(s, 512)
    ns = s // ts
    return pl.pallas_call(
        _norm_mod_kernel,
        grid=(batch, ns),
        in_specs=[pl.BlockSpec((ts, d), lambda b, i: (b * ns + i, 0)),
                  pl.BlockSpec((1, d), lambda b, i: (0, 0)),
                  pl.BlockSpec((1, 1, d), lambda b, i: (b, 0, 0)),
                  pl.BlockSpec((1, 1, d), lambda b, i: (b, 0, 0))],
        out_specs=pl.BlockSpec((ts, d), lambda b, i: (b * ns + i, 0)),
        out_shape=jax.ShapeDtypeStruct((t, d), out_dtype),
        compiler_params=_params(("arbitrary", "arbitrary")),
    )(x2, g, scale, shift)


def _mm_kernel(a_ref, w_ref, o_ref):
    o_ref[...] = _dot(a_ref[...], w_ref[...]).astype(o_ref.dtype)


def _mm(a, w, out_dtype, tm_pref=1024, tn_pref=512):
    m, k = a.shape
    n = w.shape[1]
    tm, tn = _tile(m, tm_pref), _tile(n, tn_pref)
    return pl.pallas_call(
        _mm_kernel,
        grid=(m // tm, n // tn),
        in_specs=[pl.BlockSpec((tm, k), lambda i, j: (i, 0)),
                  pl.BlockSpec((k, tn), lambda i, j: (0, j))],
        out_specs=pl.BlockSpec((tm, tn), lambda i, j: (i, j)),
        out_shape=jax.ShapeDtypeStruct((m, n), out_dtype),
        compiler_params=_params(("arbitrary", "arbitrary")),
    )(a, w)


def _mm_wcast_kernel(a_ref, w_ref, o_ref, wb_ref):
    @pl.when(pl.program_id(1) == 0)
    def _():
        wb_ref[...] = w_ref[...].astype(wb_ref.dtype)

    o_ref[...] = _dot_nt(a_ref[...], wb_ref[...]).astype(o_ref.dtype)


def _mm_wcast(a, wt3, layer, col0, n, out_dtype, tm_pref=1024, tn_pref=512):
    m, k = a.shape
    tm, tn = _tile(m, tm_pref), _tile(n, tn_pref)
    assert col0 % tn == 0
    j0 = col0 // tn
    return pl.pallas_call(
        _mm_wcast_kernel,
        grid=(n // tn, m // tm),
        in_specs=[pl.BlockSpec((tm, k), lambda j, i: (i, 0)),
                  pl.BlockSpec((None, tn, k), lambda j, i: (layer, j0 + j, 0))],
        out_specs=pl.BlockSpec((tm, tn), lambda j, i: (i, j)),
        out_shape=jax.ShapeDtypeStruct((m, n), out_dtype),
        scratch_shapes=[pltpu.VMEM((tn, k), BF16)],
        compiler_params=_params(("arbitrary", "arbitrary")),
    )(a, wt3)


def _ret_kernel(lg_ref, q_ref, k_ref, v_ref, g_ref, cos_ref, sin_ref, o_ref, state_ref):
    h = pl.program_id(1)

    @pl.when(pl.program_id(2) == 0)
    def _():
        state_ref[...] = jnp.zeros_like(state_ref)

    lg = lg_ref[h]
    c = q_ref.shape[0]
    cos, sin = cos_ref[...], sin_ref[...]
    half = RET_DK // 2

    def rope(v):
        v1, v2 = v[:, :half], v[:, half:]
        return jnp.concatenate([v1 * cos - v2 * sin, v2 * cos + v1 * sin], axis=-1)

    q = rope(q_ref[...].astype(F32))
    k = rope(k_ref[...].astype(F32)) * (RET_DK ** -0.5)
    v = v_ref[...]
    ii = lax.broadcasted_iota(jnp.int32, (c, c), 0)
    jj = lax.broadcasted_iota(jnp.int32, (c, c), 1)
    diff = (ii - jj).astype(F32)
    inner_decay = jnp.where(diff >= 0, jnp.exp(diff * lg), 0.0)
    ri = lax.broadcasted_iota(jnp.int32, (c, 1), 0).astype(F32)
    q_decay = jnp.exp((ri + 1.0) * lg)
    k_decay = jnp.exp((c - 1.0 - ri) * lg)
    chunk_decay = jnp.exp(jnp.full((1, 1), c, F32) * lg)

    qb = q.astype(BF16)
    sc = _dot_nt(qb, k.astype(BF16)) * inner_decay
    inner = _dot(sc.astype(BF16), v)
    state = state_ref[...]
    cross = _dot(qb, state.astype(BF16)) * q_decay
    kv = _dot_tn((k * k_decay).astype(BF16), v)
    state_ref[...] = state * chunk_decay + kv
    out = inner + cross
    r = out * lax.rsqrt(jnp.mean(out * out, axis=-1, keepdims=True) + EPS)
    o_ref[...] = (r * _silu(g_ref[...].astype(F32))).astype(o_ref.dtype)


def _retention(proj, cos_r, sin_r, log_g, batch, n_heads, col0):
    t = proj.shape[0]
    s = t // batch
    c = _tile(s, 512)
    n = s // c
    hh = n_heads

    def blk(off):
        return pl.BlockSpec((c, RET_DK), lambda b, h, i, lg: (b * n + i, col0 + off + h))

    tab = pl.BlockSpec((c, RET_DK // 2), lambda b, h, i, lg: (b * n + i, 0))
    return pl.pallas_call(
        _ret_kernel,
        grid_spec=pltpu.PrefetchScalarGridSpec(
            num_scalar_prefetch=1,
            grid=(batch, hh, n),
            in_specs=[blk(0), blk(hh), blk(2 * hh), blk(3 * hh), tab, tab],
            out_specs=pl.BlockSpec((c, RET_DV), lambda b, h, i, lg: (b * n + i, h)),
            scratch_shapes=[pltpu.VMEM((RET_DK, RET_DV), F32)]),
        out_shape=jax.ShapeDtypeStruct((t, hh * RET_DV), BF16),
        compiler_params=_params(("arbitrary", "arbitrary", "arbitrary")),
    )(log_g, proj, proj, proj, proj, cos_r, sin_r)


def _mla_prep_kernel(p_ref, ct_ref, sa_ref, sb_ref, gqa_ref, gkva_ref, gq_ref, gk_ref,
                     wq_ref, wk_ref, wv_ref, q_ref, k_ref, v_ref, *, qr, kvr, nh):
    p = p_ref[...].astype(F32)
    cq = p[:, :qr]
    ckv = p[:, qr:qr + kvr]
    lane = lax.broadcasted_iota(jnp.int32, (p.shape[0], LANES), 1)
    krp = jnp.where(lane < MLA_ROPE, p[:, qr + kvr:qr + kvr + LANES], 0.0)

    def rms(v, n):
        return v * lax.rsqrt(jnp.sum(v * v, axis=-1, keepdims=True) * (1.0 / n) + EPS)

    cqn = (rms(cq, qr) * gqa_ref[...]).astype(BF16)
    ckvn = (rms(ckv, kvr) * gkva_ref[...]).astype(BF16)
    qf = _dot(cqn, wq_ref[...])
    kn = _dot(ckvn, wk_ref[...])
    vf = _dot(ckvn, wv_ref[...])
    ones_col = (lax.broadcasted_iota(jnp.int32, (p.shape[0], LANES), 1) == 0).astype(v_ref.dtype)
    for h in range(nh):
        v_ref[:, h * HEAD_PAD:h * HEAD_PAD + MLA_V] = vf[:, h * MLA_V:(h + 1) * MLA_V].astype(v_ref.dtype)
        v_ref[:, h * HEAD_PAD + MLA_V:(h + 1) * HEAD_PAD] = ones_col

    ct, sa, sb = ct_ref[...], sa_ref[...], sb_ref[...]

    def rope(v):
        return v * ct + pltpu.roll(v, LANES - MLA_ROPE // 2, 1) * sa + pltpu.roll(v, MLA_ROPE // 2, 1) * sb

    gq, gk = gq_ref[...], gk_ref[...]
    kr_rot = rope(krp * gk[:, LANES:])
    kr_ss = jnp.sum(krp * krp, axis=-1, keepdims=True)
    qscale = 1.0 / math.sqrt(MLA_QK)
    for h in range(nh):
        qh = qf[:, h * HEAD_PAD:(h + 1) * HEAD_PAD]
        rq = lax.rsqrt(jnp.sum(qh * qh, axis=-1, keepdims=True) * (1.0 / MLA_QK) + EPS) * qscale
        qn = qh * rq * gq
        q_ref[:, h * HEAD_PAD:h * HEAD_PAD + LANES] = qn[:, :LANES].astype(q_ref.dtype)
        q_ref[:, h * HEAD_PAD + LANES:(h + 1) * HEAD_PAD] = rope(qn[:, LANES:]).astype(q_ref.dtype)
        knh = kn[:, h * MLA_NOPE:(h + 1) * MLA_NOPE]
        rk = lax.rsqrt((jnp.sum(knh * knh, axis=-1, keepdims=True) + kr_ss) * (1.0 / MLA_QK) + EPS)
        k_ref[:, h * HEAD_PAD:h * HEAD_PAD + LANES] = (knh * rk * gk[:, :LANES]).astype(k_ref.dtype)
        k_ref[:, h * HEAD_PAD + LANES:(h + 1) * HEAD_PAD] = (kr_rot * rk).astype(k_ref.dtype)


def _mla_prep(proj, wm, tabs, gqa, gkva, gq, gk, wq, wk, wv, qr, kvr, nh):
    t = proj.shape[0]
    tm = _tile(t, 256)
    full = lambda a: pl.BlockSpec(a.shape, lambda i: (0, 0))
    tab = pl.BlockSpec((tm, LANES), lambda i: (i, 0))
    return pl.pallas_call(
        functools.partial(_mla_prep_kernel, qr=qr, kvr=kvr, nh=nh),
        grid=(t // tm,),
        in_specs=[pl.BlockSpec((tm, wm), lambda i: (i, 0)), tab, tab, tab,
                  full(gqa), full(gkva), full(gq), full(gk), full(wq), full(wk), full(wv)],
        out_specs=[pl.BlockSpec((tm, nh * HEAD_PAD), lambda i: (i, 0)),
                   pl.BlockSpec((tm, nh * HEAD_PAD), lambda i: (i, 0)),
                   pl.BlockSpec((tm, nh * HEAD_PAD), lambda i: (i, 0))],
        out_shape=[jax.ShapeDtypeStruct((t, nh * HEAD_PAD), BF16),
                   jax.ShapeDtypeStruct((t, nh * HEAD_PAD), BF16),
                   jax.ShapeDtypeStruct((t, nh * HEAD_PAD), BF16)],
        compiler_params=_params(("arbitrary",)),
    )(proj, *tabs, gqa, gkva, gq, gk, wq, wk, wv)


def _flash_kernel(qi_ref, ki_ref, q_ref, k_ref, v_ref, o_ref, m_ref, acc_ref):
    p = pl.program_id(2)
    qi, ki = qi_ref[p], ki_ref[p]

    @pl.when(ki == 0)
    def _():
        m_ref[...] = jnp.full_like(m_ref, -jnp.inf)
        acc_ref[...] = jnp.zeros_like(acc_ref)

    def update(masked):
        blk = q_ref.shape[0]
        sub = min(blk, HEAD_PAD)
        for i in range(blk // sub):
            rows = slice(i * sub, (i + 1) * sub)
            nk = (i + 1) * sub if masked else blk
            s = _dot_nt(q_ref[rows, :], k_ref[:nk, :])
            if masked:
                row = lax.broadcasted_iota(jnp.int32, s.shape, 0) + i * sub
                col = lax.broadcasted_iota(jnp.int32, s.shape, 1)
                s = jnp.where(col <= row, s, -jnp.inf)
            m_prev = m_ref[rows, :]
            m_new = jnp.maximum(m_prev, jnp.max(s, axis=-1, keepdims=True))
            alpha = jnp.exp(m_prev - m_new)
            pe = jnp.exp((s - m_new).astype(BF16))
            acc_ref[rows, :] = alpha * acc_ref[rows, :] + _dot(pe, v_ref[:nk, :])
            m_ref[rows, :] = m_new

    @pl.when(ki < qi)
    def _():
        update(False)

    @pl.when(ki == qi)
    def _():
        update(True)
        acc = acc_ref[...]
        o_ref[...] = (acc[:, :MLA_V] / acc[:, MLA_V:MLA_V + 1]).astype(o_ref.dtype)


def _flash(q, k, v, batch, nh):
    t = q.shape[0]
    s = t // batch
    blk = _tile(s, 1024)
    nq = s // blk
    pairs = [(a, b) for a in range(nq) for b in range(a + 1)]
    qi = jnp.asarray([a for a, _ in pairs], jnp.int32)
    ki = jnp.asarray([b for _, b in pairs], jnp.int32)
    return pl.pallas_call(
        _flash_kernel,
        grid_spec=pltpu.PrefetchScalarGridSpec(
            num_scalar_prefetch=2,
            grid=(batch, nh, len(pairs)),
            in_specs=[pl.BlockSpec((blk, HEAD_PAD), lambda b, h, p, qi, ki: (b * nq + qi[p], h)),
                      pl.BlockSpec((blk, HEAD_PAD), lambda b, h, p, qi, ki: (b * nq + ki[p], h)),
                      pl.BlockSpec((blk, HEAD_PAD), lambda b, h, p, qi, ki: (b * nq + ki[p], h))],
            out_specs=pl.BlockSpec((blk, MLA_V), lambda b, h, p, qi, ki: (b * nq + qi[p], h)),
            scratch_shapes=[pltpu.VMEM((blk, 1), F32), pltpu.VMEM((blk, HEAD_PAD), F32)]),
        out_shape=jax.ShapeDtypeStruct((t, nh * MLA_V), BF16),
        compiler_params=_params(("arbitrary", "arbitrary", "arbitrary")),
    )(qi, ki, q, k, v)


def _wo_kernel(ret_ref, att_ref, w1_ref, w2_ref, x_ref, gate_ref, o_ref):
    mix = _dot(ret_ref[...], w1_ref[...]) + _dot(att_ref[...], w2_ref[...])
    o_ref[...] = x_ref[...] + gate_ref[0] * mix


def _wo(ret, att, w_o, x2, gate, batch):
    t, wr = ret.shape
    wa = att.shape[1]
    assert wr == wa
    d = w_o.shape[1]
    s = t // batch
    tm = _tile(s, 1024)
    tn = _tile(d, 512)
    ns = s // tm
    return pl.pallas_call(
        _wo_kernel,
        grid=(t // tm, d // tn),
        in_specs=[pl.BlockSpec((tm, wr), lambda i, j: (i, 0)),
                  pl.BlockSpec((tm, wa), lambda i, j: (i, 0)),
                  pl.BlockSpec((wr, tn), lambda i, j: (0, j)),
                  pl.BlockSpec((wa, tn), lambda i, j: (1, j)),
                  pl.BlockSpec((tm, tn), lambda i, j: (i, j)),
                  pl.BlockSpec((1, 1, tn), lambda i, j: (i // ns, 0, j))],
        out_specs=pl.BlockSpec((tm, tn), lambda i, j: (i, j)),
        out_shape=jax.ShapeDtypeStruct((t, d), F32),
        compiler_params=_params(("arbitrary", "arbitrary")),
    )(ret, att, w_o, w_o, x2, gate)


def _router_kernel(x_ref, g_ref, sc_ref, sh_ref, wrh_ref, wrl_ref, bias_ref, wsg_ref, wsu_ref,
                   h_ref, a_ref, e_ref, w_ref):
    h = _norm_mod(x_ref[...], g_ref[...], sc_ref[0], sh_ref[0])
    hb = h.astype(BF16)
    hbf = hb.astype(F32)
    _store_slabs(h_ref, _pack_pairs(hbf))
    a_ref[...] = (_silu(_dot(hb, wsg_ref[...])) * _dot(hb, wsu_ref[...])).astype(a_ref.dtype)

    hl = (h - hbf).astype(BF16)
    wrh = wrh_ref[...]
    logits = _dot_nt(wrh, hb) + _dot_nt(wrh, hl) + _dot_nt(wrl_ref[...], hb)
    scores = jax.nn.sigmoid(logits)
    sel = scores + bias_ref[...]
    ne, tm = sel.shape
    per_group = ne // N_GROUPS
    neg = -jnp.inf

    def first_argmax(v, iota, size):
        m = jnp.max(v, axis=0, keepdims=True)
        idx = jnp.min(jnp.where(v == m, iota, size), axis=0, keepdims=True)
        return m, idx

    iota_pg = lax.broadcasted_iota(jnp.int32, (per_group, tm), 0)
    gs = []
    for g in range(N_GROUPS):
        sg = sel[g * per_group:(g + 1) * per_group, :]
        m1, i1 = first_argmax(sg, iota_pg, per_group)
        m2 = jnp.max(jnp.where(iota_pg == i1, neg, sg), axis=0, keepdims=True)
        gs.append(m1 + m2)
    gsc = jnp.concatenate(gs, axis=0)
    iota_g = lax.broadcasted_iota(jnp.int32, (N_GROUPS, tm), 0)
    gmask = jnp.zeros((N_GROUPS, tm), jnp.bool_)
    for _ in range(TOPK_GROUPS):
        _, ig = first_argmax(gsc, iota_g, N_GROUPS)
        hit = iota_g == ig
        gmask = jnp.logical_or(gmask, hit)
        gsc = jnp.where(hit, neg, gsc)
    emask = jnp.concatenate(
        [jnp.broadcast_to(gmask[g:g + 1, :], (per_group, tm)) for g in range(N_GROUPS)], axis=0)
    cand = jnp.where(emask, sel, neg)
    iota_e = lax.broadcasted_iota(jnp.int32, (ne, tm), 0)
    es, ws = [], []
    for _ in range(TOP_K):
        _, ie = first_argmax(cand, iota_e, ne)
        hit = iota_e == ie
        es.append(ie)
        ws.append(jnp.sum(jnp.where(hit, scores, 0.0), axis=0, keepdims=True))
        cand = jnp.where(hit, neg, cand)
    wsum = ws[0]
    for wk in ws[1:]:
        wsum = wsum + wk
    e_ref[...] = jnp.concatenate(es, axis=0)
    w_ref[...] = jnp.concatenate(ws, axis=0) / wsum * ROUTED_SCALE


def _router(x2, g, scale, shift, wrh, wrl, bias, wsg, wsu, batch):
    t, d = x2.shape
    s = t // batch
    tm = _tile(s, 256)
    ns = s // tm
    ne = wrh.shape[0]
    f = wsg.shape[1]
    full = lambda a: pl.BlockSpec(a.shape, lambda i: (0,) * a.ndim)
    bat = pl.BlockSpec((1, 1, d), lambda i: (i // ns, 0, 0))
    return pl.pallas_call(
        _router_kernel,
        grid=(t // tm,),
        in_specs=[pl.BlockSpec((tm, d), lambda i: (i, 0)), full(g), bat, bat,
                  full(wrh), full(wrl), full(bias), full(wsg), full(wsu)],
        out_specs=[pl.BlockSpec((tm * (d // 2 // LANES), LANES), lambda i: (i, 0)),
                   pl.BlockSpec((tm, f), lambda i: (i, 0)),
                   pl.BlockSpec((TOP_K, tm), lambda i: (0, i)),
                   pl.BlockSpec((TOP_K, tm), lambda i: (0, i))],
        out_shape=[jax.ShapeDtypeStruct((t * (d // 2 // LANES), LANES), jnp.uint32),
                   jax.ShapeDtypeStruct((t, f), BF16),
                   jax.ShapeDtypeStruct((TOP_K, t), jnp.int32),
                   jax.ShapeDtypeStruct((TOP_K, t), F32)],
        compiler_params=_params(("arbitrary",)),
    )(x2, g, scale, shift, wrh, wrl, bias, wsg, wsu)


def _moe_kernel(te_ref, tn_ref, tok_tab, dst_tab, h_hbm, w_ref, wg_ref, wu_ref, wd_ref, y_hbm,
                xbuf, ybuf, tok_idx, dst_idx, gsem, ssem, isem):
    g = pl.program_id(0)
    n_grid = pl.num_programs(0)
    tm = tok_idx.shape[2]
    per = xbuf.shape[1] // tm
    dump_row = dst_tab.shape[0] - 1
    nxt = jnp.minimum(g + 1, n_grid - 1)
    active = tn_ref[g] > 0
    next_active = jnp.logical_and(g + 1 < n_grid, tn_ref[nxt] > 0)
    last_active = jnp.logical_and(active, jnp.logical_not(next_active))

    def issue_gather(ib, b):
        for r in range(tm):
            src = pl.multiple_of(tok_idx[ib, 0, r], per)
            pltpu.make_async_copy(h_hbm.at[pl.ds(src, per), :], xbuf.at[b, pl.ds(r * per, per), :],
                                  gsem.at[b]).start()

    def issue_scatter(ib, b):
        for r in range(tm):
            dst = pl.multiple_of(dst_idx[ib, 0, r], per)
            pltpu.make_async_copy(ybuf.at[b, pl.ds(r * per, per), :], y_hbm.at[pl.ds(dst, per), :],
                                  ssem.at[b]).start()

    def wait_gather(b):
        pltpu.make_async_copy(h_hbm.at[pl.ds(0, tm * per), :], xbuf.at[b], gsem.at[b]).wait()

    def wait_scatter(b):
        pltpu.make_async_copy(ybuf.at[b], y_hbm.at[pl.ds(0, tm * per), :], ssem.at[b]).wait()

    def tok_copy(row, b):
        return pltpu.make_async_copy(tok_tab.at[row], tok_idx.at[b], isem.at[b])

    def dst_copy(row, b):
        return pltpu.make_async_copy(dst_tab.at[row], dst_idx.at[b], isem.at[b])

    @pl.when(g == 0)
    def _():
        ybuf[...] = jnp.zeros_like(ybuf)
        for b in range(2):
            dump = pltpu.make_async_copy(
                ybuf.at[b], y_hbm.at[pl.ds(y_hbm.shape[0] - (2 - b) * tm * per, tm * per), :], ssem.at[b])
            dump.start()
            dump.wait()
        first = tok_copy(0, 1)
        first.start()
        first.wait()
        issue_gather(1, 0)
        tok_copy(nxt, 0).start()
        dst_copy(dump_row, 0).start()

    def step(b):
        tok_copy(0, b).wait()
        dst_copy(0, b).wait()

        @pl.when(next_active)
        def _():
            tok_copy(jnp.minimum(g + 2, n_grid - 1), 1 - b).start()
            dst_copy(g, 1 - b).start()

        wait_gather(b)

        @pl.when(g >= 1)
        def _():
            wait_scatter(b)

        lo, hi = _unpack_pairs(_load_slabs(xbuf.at[b], tm))
        lo, hi = lo.astype(BF16), hi.astype(BF16)
        half = wg_ref.shape[0] // 2
        issue_gather(b, 1 - b)
        gate = _dot(lo, wg_ref[:half, :]) + _dot(hi, wg_ref[half:, :])
        up = _dot(lo, wu_ref[:half, :]) + _dot(hi, wu_ref[half:, :])
        issue_scatter(b, 1 - b)
        wcol = jnp.transpose(w_ref[...])[:, 0:1]
        y = _dot((_silu(gate) * up * wcol).astype(BF16), wd_ref[...])
        _store_slabs(ybuf.at[b], _pack_pairs(_round_bf16(y)))

        @pl.when(last_active)
        def _():
            own = dst_copy(g, 1 - b)
            own.start()
            own.wait()
            issue_scatter(1 - b, b)
            wait_scatter(b)
            wait_scatter(1 - b)
            wait_gather(1 - b)

    for b in range(2):
        @pl.when(jnp.logical_and(active, lax.rem(g, 2) == b))
        def _(b=b):
            step(b)


def _moe(tile_e, tile_nv, tok_tab, dst_tab, w_tab, h2s, wg, wu, wd, n_tok, tm):
    ne, d, f = wg.shape
    per = d // 2 // LANES
    n_tiles = tile_e.shape[0]
    n_rows = TOP_K * n_tok + 2 * tm
    buf = pltpu.VMEM((2, tm * per, LANES), jnp.uint32)
    idx = pltpu.SMEM((2, 1, tm), jnp.int32)
    return pl.pallas_call(
        _moe_kernel,
        grid_spec=pltpu.PrefetchScalarGridSpec(
            num_scalar_prefetch=2,
            grid=(n_tiles,),
            in_specs=[pl.BlockSpec(memory_space=pl.ANY),
                      pl.BlockSpec(memory_space=pl.ANY),
                      pl.BlockSpec(memory_space=pl.ANY),
                      pl.BlockSpec((None, SUBLANES, tm), lambda g, te, tn: (g, 0, 0)),
                      pl.BlockSpec((None, d, f), lambda g, te, tn: (te[g], 0, 0)),
                      pl.BlockSpec((None, d, f), lambda g, te, tn: (te[g], 0, 0)),
                      pl.BlockSpec((None, f, d), lambda g, te, tn: (te[g], 0, 0))],
            out_specs=pl.BlockSpec(memory_space=pl.ANY),
            scratch_shapes=[buf, buf, idx, idx, pltpu.SemaphoreType.DMA((2,)), pltpu.SemaphoreType.DMA((2,)),
                            pltpu.SemaphoreType.DMA((2,))]),
        out_shape=jax.ShapeDtypeStruct((n_rows * per, LANES), jnp.uint32),
        compiler_params=_params(("arbitrary",)),
    )(tile_e, tile_nv, tok_tab, dst_tab, h2s, w_tab, wg, wu, wd)


def _moe_tables(order, tile_row0, tile_nv, w_flat, n_tok, tm, per):
    assert n_tok & (n_tok - 1) == 0, "token count must be a power of two (slot index packing)"
    p = order.shape[0]
    n_tiles = tile_row0.shape[0]
    r = jnp.arange(tm, dtype=jnp.int32)[None, :]
    slot = order[jnp.minimum(tile_row0[:, None] + r, p - 1)]
    parity = (jnp.arange(n_tiles, dtype=jnp.int32) % 2)[:, None]
    dump = p + parity * tm + r
    dst = jnp.where(r < tile_nv[:, None], slot, dump)
    dst = jnp.concatenate([dst, p + tm + r], axis=0)
    tok = slot & (n_tok - 1)
    w_tab = jnp.broadcast_to(w_flat[slot][:, None, :], (n_tiles, SUBLANES, tm))
    return (tok * per)[:, None, :], (dst * per)[:, None, :], w_tab


def _moe_schedule(e_idx, ne, tm):
    k, t = e_idx.shape
    p = k * t
    keys = e_idx.reshape(p)
    _, order = lax.sort((keys, jnp.arange(p, dtype=jnp.int32)), num_keys=1, is_stable=True)
    experts = jnp.arange(ne, dtype=jnp.int32)
    counts = jnp.sum((keys[None, :] == experts[:, None]).astype(jnp.int32), axis=1)
    starts = jnp.cumsum(counts) - counts
    ntile = (counts + tm - 1) // tm
    cum = jnp.cumsum(ntile)
    n_tiles = p // tm + ne
    gidx = jnp.arange(n_tiles, dtype=jnp.int32)
    te = jnp.minimum(jnp.sum((gidx[:, None] >= cum[None, :]).astype(jnp.int32), axis=1), ne - 1)
    onehot = (te[:, None] == experts[None, :]).astype(jnp.int32)
    pick = lambda v: jnp.sum(onehot * v[None, :], axis=1)
    local = gidx - pick(cum - ntile)
    nv = jnp.where(gidx < cum[-1], jnp.clip(pick(counts) - local * tm, 0, tm), 0)
    row0 = jnp.where(nv > 0, pick(starts) + local * tm, 0)
    return order, te, row0.astype(jnp.int32), nv.astype(jnp.int32)


def _combine_kernel(*refs):
    y_refs = refs[:TOP_K]
    a_ref, wsd_ref, x_ref, gate_ref, o_ref = refs[TOP_K:]
    tm = x_ref.shape[0]
    half = x_ref.shape[1] // 2
    acc_lo, acc_hi = _unpack_pairs(_load_slabs(y_refs[0], tm))
    for k in range(1, TOP_K):
        lo, hi = _unpack_pairs(_load_slabs(y_refs[k], tm))
        acc_lo = acc_lo + lo
        acc_hi = acc_hi + hi
    a = a_ref[...]
    gate = gate_ref[0]
    o_ref[:, :half] = x_ref[:, :half] + gate[:, :half] * (acc_lo + _dot(a, wsd_ref[:, :half]))
    o_ref[:, half:] = x_ref[:, half:] + gate[:, half:] * (acc_hi + _dot(a, wsd_ref[:, half:]))


def _combine(y, a_sh, wsd, x2, gate, batch):
    t, d = x2.shape
    per = d // 2 // LANES
    f = a_sh.shape[1]
    s = t // batch
    tm = _tile(s, 128)
    ns = s // tm
    nt = t // tm
    return pl.pallas_call(
        _combine_kernel,
        grid=(nt,),
        in_specs=[pl.BlockSpec((tm * per, LANES), lambda i, k=k: (k * nt + i, 0)) for k in range(TOP_K)] + [
                  pl.BlockSpec((tm, f), lambda i: (i, 0)),
                  pl.BlockSpec((f, d), lambda i: (0, 0)),
                  pl.BlockSpec((tm, d), lambda i: (i, 0)),
                  pl.BlockSpec((1, 1, d), lambda i: (i // ns, 0, 0))],
        out_specs=pl.BlockSpec((tm, d), lambda i: (i, 0)),
        out_shape=jax.ShapeDtypeStruct((t, d), F32),
        compiler_params=_params(("arbitrary",)),
    )(*([y] * TOP_K), a_sh, wsd, x2, gate)


def _round_up(v, m):
    return (v + m - 1) // m * m


def kernel(x, c, positions, w_ada, b_ada, norm1_g, w_in, q_a_norm_g, w_uq, kv_a_norm_g, w_ukv, q_norm_g, k_norm_g, w_o, norm2_g, w_router, router_bias, w_exp_gate, w_exp_up, w_exp_down, w_sh_gate, w_sh_up, w_sh_down):
    b, s, d = x.shape
    t = b * s
    depth = w_ada.shape[0]
    qr = q_a_norm_g.shape[1]
    kvr = kv_a_norm_g.shape[1]
    nh = w_uq.shape[2] // MLA_QK
    in_w = w_in.shape[2]
    ret_heads = (in_w - qr - kvr - MLA_ROPE) // (2 * RET_DK + 2 * RET_DV)
    ret_w = ret_heads * RET_DK
    ne = w_router.shape[2]
    moe_tm = 256

    pos = positions.reshape(t).astype(F32)
    inv_r = ROPE_BASE ** (-jnp.arange(0, RET_DK, 2, dtype=F32) / RET_DK)
    ang_r = pos[:, None] * inv_r
    cos_r, sin_r = jnp.cos(ang_r), jnp.sin(ang_r)
    inv_m = ROPE_BASE ** (-jnp.arange(0, MLA_ROPE, 2, dtype=F32) / MLA_ROPE)
    ang_m = pos[:, None] * inv_m
    cm, sm = jnp.cos(ang_m), jnp.sin(ang_m)
    z = jnp.zeros_like(cm)
    mla_tabs = (jnp.concatenate([cm, cm, z, z], axis=-1),
                jnp.concatenate([-sm, z, z, z], axis=-1),
                jnp.concatenate([z, sm, z, z], axis=-1))
    log_g = jnp.log(1.0 - 2.0 ** (-5.0 - jnp.arange(ret_heads, dtype=F32)))

    c_pad = jnp.zeros((8, d), F32).at[:b].set(c)
    x2 = x.reshape(t, d)
    wm = _round_up(qr + kvr + LANES, HEAD_PAD)
    ret_cols = 4 * ret_w

    for l in range(depth):
        mod = _ada(c_pad, w_ada[l], b_ada[l][None, :])[:b]
        shift_a, scale_a, gate_a, shift_m, scale_m, gate_m = (
            m.reshape(b, 1, d) for m in jnp.split(mod, 6, axis=-1))

        h1 = _norm_mod_call(x2, norm1_g[l][None, :], scale_a, shift_a, b, BF16)
        w_in_t = jnp.swapaxes(w_in, 1, 2)
        proj_ret = _mm_wcast(h1, w_in_t, l, 0, ret_cols, BF16)
        proj = _mm_wcast(h1, w_in_t, l, ret_cols, wm, BF16, tn_pref=HEAD_PAD)

        ret = _retention(proj_ret, cos_r, sin_r, log_g, b, ret_heads, 0)

        wq = jnp.pad(w_uq[l].reshape(qr, nh, MLA_QK), ((0, 0), (0, 0), (0, HEAD_PAD - MLA_QK)))
        wq = wq.reshape(qr, nh * HEAD_PAD).astype(BF16)
        wkv = w_ukv[l].reshape(kvr, nh, MLA_NOPE + MLA_V)
        wk = wkv[:, :, :MLA_NOPE].reshape(kvr, nh * MLA_NOPE).astype(BF16)
        wv = wkv[:, :, MLA_NOPE:].reshape(kvr, nh * MLA_V).astype(BF16)
        gq = jnp.pad(q_norm_g[l], (0, HEAD_PAD - MLA_QK))[None, :]
        gk = jnp.pad(k_norm_g[l], (0, HEAD_PAD - MLA_QK))[None, :]
        q, k, v = _mla_prep(proj, wm, mla_tabs, q_a_norm_g[l][None, :], kv_a_norm_g[l][None, :],
                            gq, gk, wq, wk, wv, qr, kvr, nh)
        att = _flash(q, k, v, b, nh)

        x2 = _wo(ret, att, w_o[l].astype(BF16), x2, gate_a, b)

        wr_t = w_router[l].T
        wrh = wr_t.astype(BF16)
        wrl = (wr_t - wrh.astype(F32)).astype(BF16)
        h2p, a_sh, e_idx, w_kt = _router(x2, norm2_g[l][None, :], scale_m, shift_m, wrh, wrl,
                                         router_bias[l][:, None], w_sh_gate[l].astype(BF16),
                                         w_sh_up[l].astype(BF16), b)
        order, tile_e, tile_row0, tile_nv = _moe_schedule(e_idx, ne, moe_tm)
        tok_tab, dst_tab, w_tab = _moe_tables(order, tile_row0, tile_nv, w_kt.reshape(-1), t, moe_tm,
                                              d // 2 // LANES)
        y = _moe(tile_e, tile_nv, tok_tab, dst_tab, w_tab, h2p, w_exp_gate[l].astype(BF16),
                 w_exp_up[l].astype(BF16), w_exp_down[l].astype(BF16), t, moe_tm)
        x2 = _combine(y, a_sh, w_sh_down[l].astype(BF16), x2, gate_m, b)

    return x2.reshape(b, s, d)
```

```python
import functools
import math

import jax
import jax.numpy as jnp
from jax import lax
from jax.experimental import pallas as pl
from jax.experimental.pallas import tpu as pltpu

F32 = jnp.float32
BF16 = jnp.bfloat16

EPS = 1e-6
ROPE_BASE = 10000.0
RET_DK = 256
RET_DV = 256
MLA_NOPE = 128
MLA_ROPE = 64
MLA_V = 128
MLA_QK = MLA_NOPE + MLA_ROPE
N_GROUPS = 8
TOPK_GROUPS = 4
TOP_K = 8
ROUTED_SCALE = 2.5

LANES = 128
SUBLANES = 8
HEAD_PAD = 2 * LANES
VMEM_LIMIT = 56 * 1024 * 1024


def _tile(dim, pref):
    t = min(dim, pref)
    while dim % t:
        t //= 2
    return t


def _params(sem):
    return pltpu.CompilerParams(dimension_semantics=sem, vmem_limit_bytes=VMEM_LIMIT)


def _silu(v):
    return v * jax.nn.sigmoid(v)


def _dot(a, b):
    return jnp.dot(a, b, preferred_element_type=F32)


def _dot_nt(a, b):
    return lax.dot_general(a, b, (((1,), (1,)), ((), ())), preferred_element_type=F32)


def _dot_tn(a, b):
    return lax.dot_general(a, b, (((0,), (0,)), ((), ())), preferred_element_type=F32)


HI_HALF = 0xFFFF0000


def _pack_pairs(v):
    half = v.shape[1] // 2
    bits = lax.bitcast_convert_type(v, jnp.uint32)
    return (bits[:, :half] >> 16) | (bits[:, half:] & jnp.uint32(HI_HALF))


def _unpack_pairs(w):
    lo = lax.bitcast_convert_type(w << 16, F32)
    hi = lax.bitcast_convert_type(w & jnp.uint32(HI_HALF), F32)
    return lo, hi


def _round_bf16(v):
    return v.astype(BF16).astype(F32)


def _store_slabs(ref, v):
    rows, n = v.shape
    per = n // LANES
    for s in range(per):
        ref[pl.ds(s, rows, stride=per), :] = v[:, s * LANES:(s + 1) * LANES]


def _load_slabs(ref, rows):
    per = ref.shape[0] // rows
    return jnp.concatenate([ref[pl.ds(s, rows, stride=per), :] for s in range(per)], axis=1)


def _ada_kernel(c_ref, w_ref, b_ref, o_ref):
    ca = _silu(c_ref[...]).astype(BF16)
    o_ref[...] = _dot(ca, w_ref[...].astype(BF16)) + b_ref[...]


def _ada(c_pad, w, b):
    m, d = c_pad.shape
    n = w.shape[1]
    tn = _tile(n, 512)
    return pl.pallas_call(
        _ada_kernel,
        grid=(n // tn,),
        in_specs=[pl.BlockSpec((m, d), lambda j: (0, 0)),
                  pl.BlockSpec((d, tn), lambda j: (0, j)),
                  pl.BlockSpec((1, tn), lambda j: (0, j))],
        out_specs=pl.BlockSpec((m, tn), lambda j: (0, j)),
        out_shape=jax.ShapeDtypeStruct((m, n), F32),
        compiler_params=_params(("arbitrary",)),
    )(c_pad, w, b)


def _norm_mod(x, g, scale, shift):
    xf = x.astype(F32)
    h = xf * lax.rsqrt(jnp.mean(xf * xf, axis=-1, keepdims=True) + EPS) * g
    return h * (1.0 + scale) + shift


def _norm_mod_kernel(x_ref, g_ref, sc_ref, sh_ref, o_ref):
    o_ref[...] = _norm_mod(x_ref[...], g_ref[...], sc_ref[0], sh_ref[0]).astype(o_ref.dtype)


def _norm_mod_call(x2, g, scale, shift, batch, out_dtype):
    t, d = x2.shape
    s = t // batch
    ts = _tile(s, 512)
    ns = s // ts
    return pl.pallas_call(
        _norm_mod_kernel,
        grid=(batch, ns),
        in_specs=[pl.BlockSpec((ts, d), lambda b, i: (b * ns + i, 0)),
                  pl.BlockSpec((1, d), lambda b, i: (0, 0)),
                  pl.BlockSpec((1, 1, d), lambda b, i: (b, 0, 0)),
                  pl.BlockSpec((1, 1, d), lambda b, i: (b, 0, 0))],
        out_specs=pl.BlockSpec((ts, d), lambda b, i: (b * ns + i, 0)),
        out_shape=jax.ShapeDtypeStruct((t, d), out_dtype),
        compiler_params=_params(("arbitrary", "arbitrary")),
    )(x2, g, scale, shift)


def _mm_kernel(a_ref, w_ref, o_ref):
    o_ref[...] = _dot_nt(a_ref[...], w_ref[...]).astype(o_ref.dtype)


def _mm(a, wt, out_dtype, tm_pref=1024, tn_pref=512):
    m, k = a.shape
    n = wt.shape[0]
    tm, tn = _tile(m, tm_pref), _tile(n, tn_pref)
    return pl.pallas_call(
        _mm_kernel,
        grid=(m // tm, n // tn),
        in_specs=[pl.BlockSpec((tm, k), lambda i, j: (i, 0)),
                  pl.BlockSpec((tn, k), lambda i, j: (j, 0))],
        out_specs=pl.BlockSpec((tm, tn), lambda i, j: (i, j)),
        out_shape=jax.ShapeDtypeStruct((m, n), out_dtype),
        compiler_params=_params(("arbitrary", "arbitrary")),
    )(a, wt)


def _mm_wcast_kernel(a_ref, w_ref, o_ref, wb_ref):
    @pl.when(pl.program_id(1) == 0)
    def _():
        wb_ref[...] = w_ref[...].astype(wb_ref.dtype)

    o_ref[...] = _dot_nt(a_ref[...], wb_ref[...]).astype(o_ref.dtype)


def _mm_wcast(a, wt3, layer, col0, n, out_dtype, tm_pref=1024, tn_pref=512):
    m, k = a.shape
    tm, tn = _tile(m, tm_pref), _tile(n, tn_pref)
    assert col0 % tn == 0
    j0 = col0 // tn
    return pl.pallas_call(
        _mm_wcast_kernel,
        grid=(n // tn, m // tm),
        in_specs=[pl.BlockSpec((tm, k), lambda j, i: (i, 0)),
                  pl.BlockSpec((None, tn, k), lambda j, i: (layer, j0 + j, 0))],
        out_specs=pl.BlockSpec((tm, tn), lambda j, i: (i, j)),
        out_shape=jax.ShapeDtypeStruct((m, n), out_dtype),
        scratch_shapes=[pltpu.VMEM((tn, k), BF16)],
        compiler_params=_params(("arbitrary", "arbitrary")),
    )(a, wt3)


def _ret_kernel(lg_ref, q_ref, k_ref, v_ref, g_ref, cos_ref, sin_ref, o_ref, state_ref):
    h = pl.program_id(1)

    @pl.when(pl.program_id(2) == 0)
    def _():
        state_ref[...] = jnp.zeros_like(state_ref)

    lg = lg_ref[h]
    c = q_ref.shape[0]
    cos, sin = cos_ref[...], sin_ref[...]
    half = RET_DK // 2

    def rope(v):
        v1, v2 = v[:, :half], v[:, half:]
        return jnp.concatenate([v1 * cos - v2 * sin, v2 * cos + v1 * sin], axis=-1)

    q = rope(q_ref[...].astype(F32))
    k = rope(k_ref[...].astype(F32)) * (RET_DK ** -0.5)
    v = v_ref[...]
    ii = lax.broadcasted_iota(jnp.int32, (c, c), 0)
    jj = lax.broadcasted_iota(jnp.int32, (c, c), 1)
    diff = (ii - jj).astype(F32)
    inner_decay = jnp.where(diff >= 0, jnp.exp(diff * lg), 0.0)
    ri = lax.broadcasted_iota(jnp.int32, (c, 1), 0).astype(F32)
    q_decay = jnp.exp((ri + 1.0) * lg)
    k_decay = jnp.exp((c - 1.0 - ri) * lg)
    chunk_decay = jnp.exp(jnp.full((1, 1), c, F32) * lg)

    qb = q.astype(BF16)
    sc = _dot_nt(qb, k.astype(BF16)) * inner_decay
    inner = _dot(sc.astype(BF16), v)
    state = state_ref[...]
    cross = _dot(qb, state.astype(BF16)) * q_decay
    kv = _dot_tn((k * k_decay).astype(BF16), v)
    state_ref[...] = state * chunk_decay + kv
    out = inner + cross
    r = out * lax.rsqrt(jnp.mean(out * out, axis=-1, keepdims=True) + EPS)
    o_ref[...] = (r * _silu(g_ref[...].astype(F32))).astype(o_ref.dtype)


def _retention(proj, cos_r, sin_r, log_g, batch, n_heads, col0):
    t = proj.shape[0]
    s = t // batch
    c = _tile(s, 512)
    n = s // c
    hh = n_heads

    def blk(off):
        return pl.BlockSpec((c, RET_DK), lambda b, h, i, lg: (b * n + i, col0 + off + h))

    tab = pl.BlockSpec((c, RET_DK // 2), lambda b, h, i, lg: (b * n + i, 0))
    return pl.pallas_call(
        _ret_kernel,
        grid_spec=pltpu.PrefetchScalarGridSpec(
            num_scalar_prefetch=1,
            grid=(batch, hh, n),
            in_specs=[blk(0), blk(hh), blk(2 * hh), blk(3 * hh), tab, tab],
            out_specs=pl.BlockSpec((c, RET_DV), lambda b, h, i, lg: (b * n + i, h)),
            scratch_shapes=[pltpu.VMEM((RET_DK, RET_DV), F32)]),
        out_shape=jax.ShapeDtypeStruct((t, hh * RET_DV), BF16),
        compiler_params=_params(("arbitrary", "arbitrary", "arbitrary")),
    )(log_g, proj, proj, proj, proj, cos_r, sin_r)


def _mla_prep_kernel(p_ref, ct_ref, sa_ref, sb_ref, gqa_ref, gkva_ref, gq_ref, gk_ref,
                     wq_ref, wk_ref, wv_ref, q_ref, k_ref, v_ref, *, qr, kvr, nh):
    p = p_ref[...].astype(F32)
    cq = p[:, :qr]
    ckv = p[:, qr:qr + kvr]
    krp = p[:, qr + kvr:qr + kvr + LANES]

    def rms(v, n):
        return v * lax.rsqrt(jnp.sum(v * v, axis=-1, keepdims=True) * (1.0 / n) + EPS)

    cqn = (rms(cq, qr) * gqa_ref[...]).astype(BF16)
    ckvn = (rms(ckv, kvr) * gkva_ref[...]).astype(BF16)
    qf = _dot(cqn, wq_ref[...])
    kn = _dot(ckvn, wk_ref[...])
    vf = _dot(ckvn, wv_ref[...])
    ones_col = (lax.broadcasted_iota(jnp.int32, (p.shape[0], LANES), 1) == 0).astype(v_ref.dtype)
    for h in range(nh):
        v_ref[:, h * HEAD_PAD:h * HEAD_PAD + MLA_V] = vf[:, h * MLA_V:(h + 1) * MLA_V].astype(v_ref.dtype)
        v_ref[:, h * HEAD_PAD + MLA_V:(h + 1) * HEAD_PAD] = ones_col

    ct, sa, sb = ct_ref[...], sa_ref[...], sb_ref[...]

    def rope(v):
        return v * ct + pltpu.roll(v, LANES - MLA_ROPE // 2, 1) * sa + pltpu.roll(v, MLA_ROPE // 2, 1) * sb

    gq, gk = gq_ref[...], gk_ref[...]
    kr_rot = rope(krp * gk[:, LANES:])
    kr_ss = jnp.sum(krp * krp, axis=-1, keepdims=True)
    qscale = 1.0 / math.sqrt(MLA_QK)
    for h in range(nh):
        qh = qf[:, h * HEAD_PAD:(h + 1) * HEAD_PAD]
        rq = lax.rsqrt(jnp.sum(qh * qh, axis=-1, keepdims=True) * (1.0 / MLA_QK) + EPS) * qscale
        qn = qh * rq * gq
        q_ref[:, h * HEAD_PAD:h * HEAD_PAD + LANES] = qn[:, :LANES].astype(q_ref.dtype)
        q_ref[:, h * HEAD_PAD + LANES:(h + 1) * HEAD_PAD] = rope(qn[:, LANES:]).astype(q_ref.dtype)
        knh = kn[:, h * MLA_NOPE:(h + 1) * MLA_NOPE]
        rk = lax.rsqrt((jnp.sum(knh * knh, axis=-1, keepdims=True) + kr_ss) * (1.0 / MLA_QK) + EPS)
        k_ref[:, h * HEAD_PAD:h * HEAD_PAD + LANES] = (knh * rk * gk[:, :LANES]).astype(k_ref.dtype)
        k_ref[:, h * HEAD_PAD + LANES:(h + 1) * HEAD_PAD] = (kr_rot * rk).astype(k_ref.dtype)


def _mla_prep(proj, wm, tabs, gqa, gkva, gq, gk, wq, wk, wv, qr, kvr, nh):
    t = proj.shape[0]
    tm = _tile(t, 256)
    full = lambda a: pl.BlockSpec(a.shape, lambda i: (0, 0))
    tab = pl.BlockSpec((tm, LANES), lambda i: (i, 0))
    return pl.pallas_call(
        functools.partial(_mla_prep_kernel, qr=qr, kvr=kvr, nh=nh),
        grid=(t // tm,),
        in_specs=[pl.BlockSpec((tm, wm), lambda i: (i, 0)), tab, tab, tab,
                  full(gqa), full(gkva), full(gq), full(gk), full(wq), full(wk), full(wv)],
        out_specs=[pl.BlockSpec((tm, nh * HEAD_PAD), lambda i: (i, 0)),
                   pl.BlockSpec((tm, nh * HEAD_PAD), lambda i: (i, 0)),
                   pl.BlockSpec((tm, nh * HEAD_PAD), lambda i: (i, 0))],
        out_shape=[jax.ShapeDtypeStruct((t, nh * HEAD_PAD), BF16),
                   jax.ShapeDtypeStruct((t, nh * HEAD_PAD), BF16),
                   jax.ShapeDtypeStruct((t, nh * HEAD_PAD), BF16)],
        compiler_params=_params(("arbitrary",)),
    )(proj, *tabs, gqa, gkva, gq, gk, wq, wk, wv)


def _flash_kernel(qi_ref, ki_ref, q_ref, k_ref, v_ref, o_ref, m_ref, acc_ref):
    p = pl.program_id(2)
    qi, ki = qi_ref[p], ki_ref[p]

    @pl.when(ki == 0)
    def _():
        m_ref[...] = jnp.full_like(m_ref, -jnp.inf)
        acc_ref[...] = jnp.zeros_like(acc_ref)

    def update(masked):
        blk = q_ref.shape[0]
        sub = min(blk, HEAD_PAD)
        for i in range(blk // sub):
            rows = slice(i * sub, (i + 1) * sub)
            nk = (i + 1) * sub if masked else blk
            s = _dot_nt(q_ref[rows, :], k_ref[:nk, :])
            if masked:
                row = lax.broadcasted_iota(jnp.int32, s.shape, 0) + i * sub
                col = lax.broadcasted_iota(jnp.int32, s.shape, 1)
                s = jnp.where(col <= row, s, -jnp.inf)
            m_prev = m_ref[rows, :]
            m_new = jnp.maximum(m_prev, jnp.max(s, axis=-1, keepdims=True))
            alpha = jnp.exp(m_prev - m_new)
            pe = jnp.exp((s - m_new).astype(BF16))
            acc_ref[rows, :] = alpha * acc_ref[rows, :] + _dot(pe, v_ref[:nk, :])
            m_ref[rows, :] = m_new

    @pl.when(ki < qi)
    def _():
        update(False)

    @pl.when(ki == qi)
    def _():
        update(True)
        acc = acc_ref[...]
        o_ref[...] = (acc[:, :MLA_V] / acc[:, MLA_V:MLA_V + 1]).astype(o_ref.dtype)


def _flash(q, k, v, batch, nh):
    t = q.shape[0]
    s = t // batch
    blk = _tile(s, 1024)
    nq = s // blk
    pairs = [(a, b) for a in range(nq) for b in range(a + 1)]
    qi = jnp.asarray([a for a, _ in pairs], jnp.int32)
    ki = jnp.asarray([b for _, b in pairs], jnp.int32)
    return pl.pallas_call(
        _flash_kernel,
        grid_spec=pltpu.PrefetchScalarGridSpec(
            num_scalar_prefetch=2,
            grid=(batch, nh, len(pairs)),
            in_specs=[pl.BlockSpec((blk, HEAD_PAD), lambda b, h, p, qi, ki: (b * nq + qi[p], h)),
                      pl.BlockSpec((blk, HEAD_PAD), lambda b, h, p, qi, ki: (b * nq + ki[p], h)),
                      pl.BlockSpec((blk, HEAD_PAD), lambda b, h, p, qi, ki: (b * nq + ki[p], h))],
            out_specs=pl.BlockSpec((blk, MLA_V), lambda b, h, p, qi, ki: (b * nq + qi[p], h)),
            scratch_shapes=[pltpu.VMEM((blk, 1), F32), pltpu.VMEM((blk, HEAD_PAD), F32)]),
        out_shape=jax.ShapeDtypeStruct((t, nh * MLA_V), BF16),
        compiler_params=_params(("arbitrary", "arbitrary", "arbitrary")),
    )(qi, ki, q, k, v)


def _wo_kernel(ret_ref, att_ref, w1_ref, w2_ref, x_ref, gate_ref, o_ref):
    mix = _dot(ret_ref[...], w1_ref[...]) + _dot(att_ref[...], w2_ref[...])
    o_ref[...] = x_ref[...] + gate_ref[0] * mix


def _wo(ret, att, w_o, x2, gate, batch):
    t, wr = ret.shape
    wa = att.shape[1]
    assert wr == wa
    d = w_o.shape[1]
    s = t // batch
    tm = _tile(s, 1024)
    tn = _tile(d, 512)
    ns = s // tm
    return pl.pallas_call(
        _wo_kernel,
        grid=(t // tm, d // tn),
        in_specs=[pl.BlockSpec((tm, wr), lambda i, j: (i, 0)),
                  pl.BlockSpec((tm, wa), lambda i, j: (i, 0)),
                  pl.BlockSpec((wr, tn), lambda i, j: (0, j)),
                  pl.BlockSpec((wa, tn), lambda i, j: (1, j)),
                  pl.BlockSpec((tm, tn), lambda i, j: (i, j)),
                  pl.BlockSpec((1, 1, tn), lambda i, j: (i // ns, 0, j))],
        out_specs=pl.BlockSpec((tm, tn), lambda i, j: (i, j)),
        out_shape=jax.ShapeDtypeStruct((t, d), F32),
        compiler_params=_params(("arbitrary", "arbitrary")),
    )(ret, att, w_o, w_o, x2, gate)


def _router_kernel(x_ref, g_ref, sc_ref, sh_ref, wrh_ref, wrl_ref, bias_ref, wsg_ref, wsu_ref,
                   h_ref, a_ref, e_ref, w_ref):
    h = _norm_mod(x_ref[...], g_ref[...], sc_ref[0], sh_ref[0])
    hb = h.astype(BF16)
    hbf = hb.astype(F32)
    _store_slabs(h_ref, _pack_pairs(hbf))
    a_ref[...] = (_silu(_dot(hb, wsg_ref[...])) * _dot(hb, wsu_ref[...])).astype(a_ref.dtype)

    hl = (h - hbf).astype(BF16)
    wrh = wrh_ref[...]
    logits = _dot_nt(wrh, hb) + _dot_nt(wrh, hl) + _dot_nt(wrl_ref[...], hb)
    scores = jax.nn.sigmoid(logits)
    sel = scores + bias_ref[...]
    ne, tm = sel.shape
    per_group = ne // N_GROUPS
    neg = -jnp.inf

    def first_argmax(v, iota, size):
        m = jnp.max(v, axis=0, keepdims=True)
        idx = jnp.min(jnp.where(v == m, iota, size), axis=0, keepdims=True)
        return m, idx

    iota_pg = lax.broadcasted_iota(jnp.int32, (per_group, tm), 0)
    gs = []
    for g in range(N_GROUPS):
        sg = sel[g * per_group:(g + 1) * per_group, :]
        m1, i1 = first_argmax(sg, iota_pg, per_group)
        m2 = jnp.max(jnp.where(iota_pg == i1, neg, sg), axis=0, keepdims=True)
        gs.append(m1 + m2)
    gsc = jnp.concatenate(gs, axis=0)
    iota_g = lax.broadcasted_iota(jnp.int32, (N_GROUPS, tm), 0)
    gmask = jnp.zeros((N_GROUPS, tm), jnp.bool_)
    for _ in range(TOPK_GROUPS):
        _, ig = first_argmax(gsc, iota_g, N_GROUPS)
        hit = iota_g == ig
        gmask = jnp.logical_or(gmask, hit)
        gsc = jnp.where(hit, neg, gsc)
    emask = jnp.concatenate(
        [jnp.broadcast_to(gmask[g:g + 1, :], (per_group, tm)) for g in range(N_GROUPS)], axis=0)
    cand = jnp.where(emask, sel, neg)
    iota_e = lax.broadcasted_iota(jnp.int32, (ne, tm), 0)
    es, ws = [], []
    for _ in range(TOP_K):
        _, ie = first_argmax(cand, iota_e, ne)
        hit = iota_e == ie
        es.append(ie)
        ws.append(jnp.sum(jnp.where(hit, scores, 0.0), axis=0, keepdims=True))
        cand = jnp.where(hit, neg, cand)
    wsum = ws[0]
    for wk in ws[1:]:
        wsum = wsum + wk
    e_ref[...] = jnp.concatenate(es, axis=0)
    w_ref[...] = jnp.concatenate(ws, axis=0) / wsum * ROUTED_SCALE


def _router(x2, g, scale, shift, wrh, wrl, bias, wsg, wsu, batch):
    t, d = x2.shape
    s = t // batch
    tm = _tile(s, 256)
    ns = s // tm
    ne = wrh.shape[0]
    f = wsg.shape[1]
    full = lambda a: pl.BlockSpec(a.shape, lambda i: (0,) * a.ndim)
    bat = pl.BlockSpec((1, 1, d), lambda i: (i // ns, 0, 0))
    return pl.pallas_call(
        _router_kernel,
        grid=(t // tm,),
        in_specs=[pl.BlockSpec((tm, d), lambda i: (i, 0)), full(g), bat, bat,
                  full(wrh), full(wrl), full(bias), full(wsg), full(wsu)],
        out_specs=[pl.BlockSpec((tm * (d // 2 // LANES), LANES), lambda i: (i, 0)),
                   pl.BlockSpec((tm, f), lambda i: (i, 0)),
                   pl.BlockSpec((TOP_K, tm), lambda i: (0, i)),
                   pl.BlockSpec((TOP_K, tm), lambda i: (0, i))],
        out_shape=[jax.ShapeDtypeStruct((t * (d // 2 // LANES), LANES), jnp.uint32),
                   jax.ShapeDtypeStruct((t, f), BF16),
                   jax.ShapeDtypeStruct((TOP_K, t), jnp.int32),
                   jax.ShapeDtypeStruct((TOP_K, t), F32)],
        compiler_params=_params(("arbitrary",)),
    )(x2, g, scale, shift, wrh, wrl, bias, wsg, wsu)


def _moe_kernel(te_ref, tn_ref, tok_tab, dst_tab, h_hbm, wg_ref, wu_ref, wd_ref, y_hbm,
                xbuf, ybuf, tok_idx, dst_idx, gsem, ssem, isem):
    g = pl.program_id(0)
    n_grid = pl.num_programs(0)
    tm = tok_idx.shape[2]
    per = xbuf.shape[1] // tm
    dump_row = dst_tab.shape[0] - 1
    nxt = jnp.minimum(g + 1, n_grid - 1)
    active = tn_ref[g] > 0
    next_active = jnp.logical_and(g + 1 < n_grid, tn_ref[nxt] > 0)
    last_active = jnp.logical_and(active, jnp.logical_not(next_active))

    def issue_gather(ib, b):
        for r in range(tm):
            src = pl.multiple_of(tok_idx[ib, 0, r], per)
            pltpu.make_async_copy(h_hbm.at[pl.ds(src, per), :], xbuf.at[b, pl.ds(r * per, per), :],
                                  gsem.at[b]).start()

    def issue_scatter(ib, b):
        for r in range(tm):
            dst = pl.multiple_of(dst_idx[ib, 0, r], per)
            pltpu.make_async_copy(ybuf.at[b, pl.ds(r * per, per), :], y_hbm.at[pl.ds(dst, per), :],
                                  ssem.at[b]).start()

    def wait_gather(b):
        pltpu.make_async_copy(h_hbm.at[pl.ds(0, tm * per), :], xbuf.at[b], gsem.at[b]).wait()

    def wait_scatter(b):
        pltpu.make_async_copy(ybuf.at[b], y_hbm.at[pl.ds(0, tm * per), :], ssem.at[b]).wait()

    def tok_copy(row, b):
        return pltpu.make_async_copy(tok_tab.at[row], tok_idx.at[b], isem.at[b])

    def dst_copy(row, b):
        return pltpu.make_async_copy(dst_tab.at[row], dst_idx.at[b], isem.at[b])

    @pl.when(g == 0)
    def _():
        ybuf[...] = jnp.zeros_like(ybuf)
        for b in range(2):
            dump = pltpu.make_async_copy(
                ybuf.at[b], y_hbm.at[pl.ds(y_hbm.shape[0] - (2 - b) * tm * per, tm * per), :], ssem.at[b])
            dump.start()
            dump.wait()
        first = tok_copy(0, 1)
        first.start()
        first.wait()
        issue_gather(1, 0)
        tok_copy(nxt, 0).start()
        dst_copy(dump_row, 0).start()

    def step(b):
        tok_copy(0, b).wait()
        dst_copy(0, b).wait()

        @pl.when(next_active)
        def _():
            tok_copy(jnp.minimum(g + 2, n_grid - 1), 1 - b).start()
            dst_copy(g, 1 - b).start()

        wait_gather(b)

        @pl.when(g >= 1)
        def _():
            wait_scatter(b)

        lo, hi = _unpack_pairs(_load_slabs(xbuf.at[b], tm))
        lo, hi = lo.astype(BF16), hi.astype(BF16)
        half = wg_ref.shape[0] // 2
        issue_gather(b, 1 - b)
        gate = _dot(lo, wg_ref[:half, :]) + _dot(hi, wg_ref[half:, :])
        up = _dot(lo, wu_ref[:half, :]) + _dot(hi, wu_ref[half:, :])
        issue_scatter(b, 1 - b)
        y = _dot((_silu(gate) * up).astype(BF16), wd_ref[...])
        _store_slabs(ybuf.at[b], _pack_pairs(_round_bf16(y)))

        @pl.when(last_active)
        def _():
            own = dst_copy(g, 1 - b)
            own.start()
            own.wait()
            issue_scatter(1 - b, b)
            wait_scatter(b)
            wait_scatter(1 - b)
            wait_gather(1 - b)

    for b in range(2):
        @pl.when(jnp.logical_and(active, lax.rem(g, 2) == b))
        def _(b=b):
            step(b)


def _moe(tile_e, tile_nv, tok_tab, dst_tab, h2s, wg, wu, wd, n_tok, tm):
    ne, d, f = wg.shape
    per = d // 2 // LANES
    n_tiles = tile_e.shape[0]
    n_rows = TOP_K * n_tok + 2 * tm
    buf = pltpu.VMEM((2, tm * per, LANES), jnp.uint32)
    idx = pltpu.SMEM((2, 1, tm), jnp.int32)
    return pl.pallas_call(
        _moe_kernel,
        grid_spec=pltpu.PrefetchScalarGridSpec(
            num_scalar_prefetch=2,
            grid=(n_tiles,),
            in_specs=[pl.BlockSpec(memory_space=pl.ANY),
                      pl.BlockSpec(memory_space=pl.ANY),
                      pl.BlockSpec(memory_space=pl.ANY),
                      pl.BlockSpec((None, d, f), lambda g, te, tn: (te[g], 0, 0)),
                      pl.BlockSpec((None, d, f), lambda g, te, tn: (te[g], 0, 0)),
                      pl.BlockSpec((None, f, d), lambda g, te, tn: (te[g], 0, 0))],
            out_specs=pl.BlockSpec(memory_space=pl.ANY),
            scratch_shapes=[buf, buf, idx, idx, pltpu.SemaphoreType.DMA((2,)), pltpu.SemaphoreType.DMA((2,)),
                            pltpu.SemaphoreType.DMA((2,))]),
        out_shape=jax.ShapeDtypeStruct((n_rows * per, LANES), jnp.uint32),
        compiler_params=_params(("arbitrary",)),
    )(tile_e, tile_nv, tok_tab, dst_tab, h2s, wg, wu, wd)


def _moe_tables(order, tile_row0, tile_nv, n_tok, tm, per):
    assert n_tok & (n_tok - 1) == 0, "token count must be a power of two (slot index packing)"
    p = order.shape[0]
    n_tiles = tile_row0.shape[0]
    r = jnp.arange(tm, dtype=jnp.int32)[None, :]
    slot = order[jnp.minimum(tile_row0[:, None] + r, p - 1)]
    parity = (jnp.arange(n_tiles, dtype=jnp.int32) % 2)[:, None]
    dump = p + parity * tm + r
    dst = jnp.where(r < tile_nv[:, None], slot, dump)
    dst = jnp.concatenate([dst, p + tm + r], axis=0)
    tok = slot & (n_tok - 1)
    return (tok * per)[:, None, :], (dst * per)[:, None, :]


def _moe_schedule(e_idx, ne, tm):
    k, t = e_idx.shape
    p = k * t
    keys = e_idx.reshape(p)
    _, order = lax.sort((keys, jnp.arange(p, dtype=jnp.int32)), num_keys=1, is_stable=True)
    experts = jnp.arange(ne, dtype=jnp.int32)
    counts = jnp.sum((keys[None, :] == experts[:, None]).astype(jnp.int32), axis=1)
    starts = jnp.cumsum(counts) - counts
    ntile = (counts + tm - 1) // tm
    cum = jnp.cumsum(ntile)
    n_tiles = p // tm + ne
    gidx = jnp.arange(n_tiles, dtype=jnp.int32)
    te = jnp.minimum(jnp.sum((gidx[:, None] >= cum[None, :]).astype(jnp.int32), axis=1), ne - 1)
    onehot = (te[:, None] == experts[None, :]).astype(jnp.int32)
    pick = lambda v: jnp.sum(onehot * v[None, :], axis=1)
    local = gidx - pick(cum - ntile)
    nv = jnp.where(gidx < cum[-1], jnp.clip(pick(counts) - local * tm, 0, tm), 0)
    row0 = jnp.where(nv > 0, pick(starts) + local * tm, 0)
    return order, te, row0.astype(jnp.int32), nv.astype(jnp.int32)


def _combine_kernel(*refs):
    y_refs = refs[:TOP_K]
    w_ref, a_ref, wsd_ref, x_ref, gate_ref, o_ref = refs[TOP_K:]
    tm = x_ref.shape[0]
    half = x_ref.shape[1] // 2
    w = w_ref[...]
    lo, hi = _unpack_pairs(_load_slabs(y_refs[0], tm))
    acc_lo, acc_hi = w[:, 0:1] * lo, w[:, 0:1] * hi
    for k in range(1, TOP_K):
        lo, hi = _unpack_pairs(_load_slabs(y_refs[k], tm))
        acc_lo = acc_lo + w[:, k:k + 1] * lo
        acc_hi = acc_hi + w[:, k:k + 1] * hi
    a = a_ref[...]
    gate = gate_ref[0]
    o_ref[:, :half] = x_ref[:, :half] + gate[:, :half] * (acc_lo + _dot(a, wsd_ref[:, :half]))
    o_ref[:, half:] = x_ref[:, half:] + gate[:, half:] * (acc_hi + _dot(a, wsd_ref[:, half:]))


def _combine(y, w_tk, a_sh, wsd, x2, gate, batch):
    t, d = x2.shape
    per = d // 2 // LANES
    kk = w_tk.shape[1]
    assert kk == TOP_K
    f = a_sh.shape[1]
    s = t // batch
    tm = _tile(s, 128)
    ns = s // tm
    nt = t // tm
    return pl.pallas_call(
        _combine_kernel,
        grid=(nt,),
        in_specs=[pl.BlockSpec((tm * per, LANES), lambda i, k=k: (k * nt + i, 0)) for k in range(TOP_K)] + [
                  pl.BlockSpec((tm, kk), lambda i: (i, 0)),
                  pl.BlockSpec((tm, f), lambda i: (i, 0)),
                  pl.BlockSpec((f, d), lambda i: (0, 0)),
                  pl.BlockSpec((tm, d), lambda i: (i, 0)),
                  pl.BlockSpec((1, 1, d), lambda i: (i // ns, 0, 0))],
        out_specs=pl.BlockSpec((tm, d), lambda i: (i, 0)),
        out_shape=jax.ShapeDtypeStruct((t, d), F32),
        compiler_params=_params(("arbitrary",)),
    )(*([y] * TOP_K), w_tk, a_sh, wsd, x2, gate)


def _round_up(v, m):
    return (v + m - 1) // m * m


def kernel(x, c, positions, w_ada, b_ada, norm1_g, w_in, q_a_norm_g, w_uq, kv_a_norm_g, w_ukv, q_norm_g, k_norm_g, w_o, norm2_g, w_router, router_bias, w_exp_gate, w_exp_up, w_exp_down, w_sh_gate, w_sh_up, w_sh_down):
    b, s, d = x.shape
    t = b * s
    depth = w_ada.shape[0]
    qr = q_a_norm_g.shape[1]
    kvr = kv_a_norm_g.shape[1]
    nh = w_uq.shape[2] // MLA_QK
    in_w = w_in.shape[2]
    ret_heads = (in_w - qr - kvr - MLA_ROPE) // (2 * RET_DK + 2 * RET_DV)
    ret_w = ret_heads * RET_DK
    ne = w_router.shape[2]
    moe_tm = 256

    pos = positions.reshape(t).astype(F32)
    inv_r = ROPE_BASE ** (-jnp.arange(0, RET_DK, 2, dtype=F32) / RET_DK)
    ang_r = pos[:, None] * inv_r
    cos_r, sin_r = jnp.cos(ang_r), jnp.sin(ang_r)
    inv_m = ROPE_BASE ** (-jnp.arange(0, MLA_ROPE, 2, dtype=F32) / MLA_ROPE)
    ang_m = pos[:, None] * inv_m
    cm, sm = jnp.cos(ang_m), jnp.sin(ang_m)
    z = jnp.zeros_like(cm)
    mla_tabs = (jnp.concatenate([cm, cm, z, z], axis=-1),
                jnp.concatenate([-sm, z, z, z], axis=-1),
                jnp.concatenate([z, sm, z, z], axis=-1))
    log_g = jnp.log(1.0 - 2.0 ** (-5.0 - jnp.arange(ret_heads, dtype=F32)))

    c_pad = jnp.zeros((8, d), F32).at[:b].set(c)
    x2 = x.reshape(t, d)
    wm = _round_up(qr + kvr + LANES, HEAD_PAD)
    ret_cols = 4 * ret_w

    for l in range(depth):
        mod = _ada(c_pad, w_ada[l], b_ada[l][None, :])[:b]
        shift_a, scale_a, gate_a, shift_m, scale_m, gate_m = (
            m.reshape(b, 1, d) for m in jnp.split(mod, 6, axis=-1))

        h1 = _norm_mod_call(x2, norm1_g[l][None, :], scale_a, shift_a, b, BF16)
        w_in_t = jnp.swapaxes(w_in, 1, 2)
        proj_ret = _mm_wcast(h1, w_in_t, l, 0, ret_cols, BF16)
        w_mla_t = jnp.pad(w_in_t[l, ret_cols:, :], ((0, wm - (in_w - ret_cols)), (0, 0))).astype(BF16)
        proj = _mm(h1, w_mla_t, BF16)

        ret = _retention(proj_ret, cos_r, sin_r, log_g, b, ret_heads, 0)

        wq = jnp.pad(w_uq[l].reshape(qr, nh, MLA_QK), ((0, 0), (0, 0), (0, HEAD_PAD - MLA_QK)))
        wq = wq.reshape(qr, nh * HEAD_PAD).astype(BF16)
        wkv = w_ukv[l].reshape(kvr, nh, MLA_NOPE + MLA_V)
        wk = wkv[:, :, :MLA_NOPE].reshape(kvr, nh * MLA_NOPE).astype(BF16)
        wv = wkv[:, :, MLA_NOPE:].reshape(kvr, nh * MLA_V).astype(BF16)
        gq = jnp.pad(q_norm_g[l], (0, HEAD_PAD - MLA_QK))[None, :]
        gk = jnp.pad(k_norm_g[l], (0, HEAD_PAD - MLA_QK))[None, :]
        q, k, v = _mla_prep(proj, wm, mla_tabs, q_a_norm_g[l][None, :], kv_a_norm_g[l][None, :],
                            gq, gk, wq, wk, wv, qr, kvr, nh)
        att = _flash(q, k, v, b, nh)

        x2 = _wo(ret, att, w_o[l].astype(BF16), x2, gate_a, b)

        wr_t = w_router[l].T
        wrh = wr_t.astype(BF16)
        wrl = (wr_t - wrh.astype(F32)).astype(BF16)
        h2p, a_sh, e_idx, w_kt = _router(x2, norm2_g[l][None, :], scale_m, shift_m, wrh, wrl,
                                         router_bias[l][:, None], w_sh_gate[l].astype(BF16),
                                         w_sh_up[l].astype(BF16), b)
        order, tile_e, tile_row0, tile_nv = _moe_schedule(e_idx, ne, moe_tm)
        tok_tab, dst_tab = _moe_tables(order, tile_row0, tile_nv, t, moe_tm, d // 2 // LANES)
        y = _moe(tile_e, tile_nv, tok_tab, dst_tab, h2p, w_exp_gate[l].astype(BF16),
                 w_exp_up[l].astype(BF16), w_exp_down[l].astype(BF16), t, moe_tm)
        x2 = _combine(y, w_kt.T, a_sh, w_sh_down[l].astype(BF16), x2, gate_m, b)

    return x2.reshape(b, s, d)
```

```python
import functools
import math

import jax
import jax.numpy as jnp
from jax import lax
from jax.experimental import pallas as pl
from jax.experimental.pallas import tpu as pltpu

F32 = jnp.float32
BF16 = jnp.bfloat16

EPS = 1e-6
ROPE_BASE = 10000.0
RET_DK = 256
RET_DV = 256
MLA_NOPE = 128
MLA_ROPE = 64
MLA_V = 128
MLA_QK = MLA_NOPE + MLA_ROPE
N_GROUPS = 8
TOPK_GROUPS = 4
TOP_K = 8
ROUTED_SCALE = 2.5

LANES = 128
SUBLANES = 8
HEAD_PAD = 2 * LANES
VMEM_LIMIT = 56 * 1024 * 1024
MOE_GU_ROWS = 512
MOE_DN_ROWS = 64


def _tile(dim, pref):
    t = min(dim, pref)
    while dim % t:
        t //= 2
    return t


def _params(sem):
    return pltpu.CompilerParams(dimension_semantics=sem, vmem_limit_bytes=VMEM_LIMIT)


def _silu(v):
    return v * jax.nn.sigmoid(v)


def _dot(a, b):
    return jnp.dot(a, b, preferred_element_type=F32)


def _dot_nt(a, b):
    return lax.dot_general(a, b, (((1,), (1,)), ((), ())), preferred_element_type=F32)


def _dot_tn(a, b):
    return lax.dot_general(a, b, (((0,), (0,)), ((), ())), preferred_element_type=F32)


HI_HALF = 0xFFFF0000


def _pack_pairs(v):
    half = v.shape[1] // 2
    bits = lax.bitcast_convert_type(v, jnp.uint32)
    return (bits[:, :half] >> 16) | (bits[:, half:] & jnp.uint32(HI_HALF))


def _unpack_pairs(w):
    lo = lax.bitcast_convert_type(w << 16, F32)
    hi = lax.bitcast_convert_type(w & jnp.uint32(HI_HALF), F32)
    return lo, hi


def _round_bf16(v):
    return v.astype(BF16).astype(F32)


def _store_slabs(ref, v):
    rows, n = v.shape
    per = n // LANES
    for s in range(per):
        ref[pl.ds(s, rows, stride=per), :] = v[:, s * LANES:(s + 1) * LANES]


def _load_slabs(ref, rows):
    per = ref.shape[0] // rows
    return jnp.concatenate([ref[pl.ds(s, rows, stride=per), :] for s in range(per)], axis=1)


def _ada_kernel(c_ref, w_ref, b_ref, o_ref):
    ca = _silu(c_ref[...]).astype(BF16)
    o_ref[...] = _dot(ca, w_ref[...].astype(BF16)) + b_ref[...]


def _ada(c_pad, w, b):
    m, d = c_pad.shape
    n = w.shape[1]
    tn = _tile(n, 512)
    return pl.pallas_call(
        _ada_kernel,
        grid=(n // tn,),
        in_specs=[pl.BlockSpec((m, d), lambda j: (0, 0)),
                  pl.BlockSpec((d, tn), lambda j: (0, j)),
                  pl.BlockSpec((1, tn), lambda j: (0, j))],
        out_specs=pl.BlockSpec((m, tn), lambda j: (0, j)),
        out_shape=jax.ShapeDtypeStruct((m, n), F32),
        compiler_params=_params(("arbitrary",)),
    )(c_pad, w, b)


def _norm_mod(x, g, scale, shift):
    xf = x.astype(F32)
    h = xf * lax.rsqrt(jnp.mean(xf * xf, axis=-1, keepdims=True) + EPS) * g
    return h * (1.0 + scale) + shift


def _norm_mod_kernel(x_ref, g_ref, sc_ref, sh_ref, o_ref):
    o_ref[...] = _norm_mod(x_ref[...], g_ref[...], sc_ref[0], sh_ref[0]).astype(o_ref.dtype)


def _norm_mod_call(x2, g, scale, shift, batch, out_dtype):
    t, d = x2.shape
    s = t // batch
    ts = _tile(s, 512)
    ns = s // ts
    return pl.pallas_call(
        _norm_mod_kernel,
        grid=(batch, ns),
        in_specs=[pl.BlockSpec((ts, d), lambda b, i: (b * ns + i, 0)),
                  pl.BlockSpec((1, d), lambda b, i: (0, 0)),
                  pl.BlockSpec((1, 1, d), lambda b, i: (b, 0, 0)),
                  pl.BlockSpec((1, 1, d), lambda b, i: (b, 0, 0))],
        out_specs=pl.BlockSpec((ts, d), lambda b, i: (b * ns + i, 0)),
        out_shape=jax.ShapeDtypeStruct((t, d), out_dtype),
        compiler_params=_params(("arbitrary", "arbitrary")),
    )(x2, g, scale, shift)


def _mm_kernel(a_ref, w_ref, o_ref):
    o_ref[...] = _dot_nt(a_ref[...], w_ref[...]).astype(o_ref.dtype)


def _mm(a, wt, out_dtype, tm_pref=1024, tn_pref=512):
    m, k = a.shape
    n = wt.shape[0]
    tm, tn = _tile(m, tm_pref), _tile(n, tn_pref)
    return pl.pallas_call(
        _mm_kernel,
        grid=(m // tm, n // tn),
        in_specs=[pl.BlockSpec((tm, k), lambda i, j: (i, 0)),
                  pl.BlockSpec((tn, k), lambda i, j: (j, 0))],
        out_specs=pl.BlockSpec((tm, tn), lambda i, j: (i, j)),
        out_shape=jax.ShapeDtypeStruct((m, n), out_dtype),
        compiler_params=_params(("arbitrary", "arbitrary")),
    )(a, wt)


def _mm_wcast_kernel(a_ref, w_ref, o_ref, wb_ref):
    @pl.when(pl.program_id(1) == 0)
    def _():
        wb_ref[...] = w_ref[...].astype(wb_ref.dtype)

    o_ref[...] = _dot_nt(a_ref[...], wb_ref[...]).astype(o_ref.dtype)


def _mm_wcast(a, wt3, layer, col0, n, out_dtype, tm_pref=1024, tn_pref=512):
    m, k = a.shape
    tm, tn = _tile(m, tm_pref), _tile(n, tn_pref)
    assert col0 % tn == 0
    j0 = col0 // tn
    return pl.pallas_call(
        _mm_wcast_kernel,
        grid=(n // tn, m // tm),
        in_specs=[pl.BlockSpec((tm, k), lambda j, i: (i, 0)),
                  pl.BlockSpec((None, tn, k), lambda j, i: (layer, j0 + j, 0))],
        out_specs=pl.BlockSpec((tm, tn), lambda j, i: (i, j)),
        out_shape=jax.ShapeDtypeStruct((m, n), out_dtype),
        scratch_shapes=[pltpu.VMEM((tn, k), BF16)],
        compiler_params=_params(("arbitrary", "arbitrary")),
    )(a, wt3)


def _ret_kernel(lg_ref, q_ref, k_ref, v_ref, g_ref, cos_ref, sin_ref, o_ref, state_ref):
    h = pl.program_id(1)

    @pl.when(pl.program_id(2) == 0)
    def _():
        state_ref[...] = jnp.zeros_like(state_ref)

    lg = lg_ref[h]
    c = q_ref.shape[0]
    cos, sin = cos_ref[...], sin_ref[...]
    half = RET_DK // 2

    def rope(v):
        v1, v2 = v[:, :half], v[:, half:]
        return jnp.concatenate([v1 * cos - v2 * sin, v2 * cos + v1 * sin], axis=-1)

    q = rope(q_ref[...].astype(F32))
    k = rope(k_ref[...].astype(F32)) * (RET_DK ** -0.5)
    v = v_ref[...]
    ii = lax.broadcasted_iota(jnp.int32, (c, c), 0)
    jj = lax.broadcasted_iota(jnp.int32, (c, c), 1)
    diff = (ii - jj).astype(F32)
    inner_decay = jnp.where(diff >= 0, jnp.exp(diff * lg), 0.0)
    ri = lax.broadcasted_iota(jnp.int32, (c, 1), 0).astype(F32)
    q_decay = jnp.exp((ri + 1.0) * lg)
    k_decay = jnp.exp((c - 1.0 - ri) * lg)
    chunk_decay = jnp.exp(jnp.full((1, 1), c, F32) * lg)

    qb = q.astype(BF16)
    sc = _dot_nt(qb, k.astype(BF16)) * inner_decay
    inner = _dot(sc.astype(BF16), v)
    state = state_ref[...]
    cross = _dot(qb, state.astype(BF16)) * q_decay
    kv = _dot_tn((k * k_decay).astype(BF16), v)
    state_ref[...] = state * chunk_decay + kv
    out = inner + cross
    r = out * lax.rsqrt(jnp.mean(out * out, axis=-1, keepdims=True) + EPS)
    o_ref[...] = (r * _silu(g_ref[...].astype(F32))).astype(o_ref.dtype)


def _retention(proj, cos_r, sin_r, log_g, batch, n_heads, col0):
    t = proj.shape[0]
    s = t // batch
    c = _tile(s, 512)
    n = s // c
    hh = n_heads

    def blk(off):
        return pl.BlockSpec((c, RET_DK), lambda b, h, i, lg: (b * n + i, col0 + off + h))

    tab = pl.BlockSpec((c, RET_DK // 2), lambda b, h, i, lg: (b * n + i, 0))
    return pl.pallas_call(
        _ret_kernel,
        grid_spec=pltpu.PrefetchScalarGridSpec(
            num_scalar_prefetch=1,
            grid=(batch, hh, n),
            in_specs=[blk(0), blk(hh), blk(2 * hh), blk(3 * hh), tab, tab],
            out_specs=pl.BlockSpec((c, RET_DV), lambda b, h, i, lg: (b * n + i, h)),
            scratch_shapes=[pltpu.VMEM((RET_DK, RET_DV), F32)]),
        out_shape=jax.ShapeDtypeStruct((t, hh * RET_DV), BF16),
        compiler_params=_params(("arbitrary", "arbitrary", "arbitrary")),
    )(log_g, proj, proj, proj, proj, cos_r, sin_r)


def _mla_prep_kernel(p_ref, ct_ref, sa_ref, sb_ref, gqa_ref, gkva_ref, gq_ref, gk_ref,
                     wq_ref, wk_ref, wv_ref, q_ref, k_ref, v_ref, *, qr, kvr, nh):
    p = p_ref[...].astype(F32)
    cq = p[:, :qr]
    ckv = p[:, qr:qr + kvr]
    krp = p[:, qr + kvr:qr + kvr + LANES]

    def rms(v, n):
        return v * lax.rsqrt(jnp.sum(v * v, axis=-1, keepdims=True) * (1.0 / n) + EPS)

    cqn = (rms(cq, qr) * gqa_ref[...]).astype(BF16)
    ckvn = (rms(ckv, kvr) * gkva_ref[...]).astype(BF16)
    qf = _dot(cqn, wq_ref[...])
    kn = _dot(ckvn, wk_ref[...])
    vf = _dot(ckvn, wv_ref[...])
    ones_col = (lax.broadcasted_iota(jnp.int32, (p.shape[0], LANES), 1) == 0).astype(v_ref.dtype)
    for h in range(nh):
        v_ref[:, h * HEAD_PAD:h * HEAD_PAD + MLA_V] = vf[:, h * MLA_V:(h + 1) * MLA_V].astype(v_ref.dtype)
        v_ref[:, h * HEAD_PAD + MLA_V:(h + 1) * HEAD_PAD] = ones_col

    ct, sa, sb = ct_ref[...], sa_ref[...], sb_ref[...]

    def rope(v):
        return v * ct + pltpu.roll(v, LANES - MLA_ROPE // 2, 1) * sa + pltpu.roll(v, MLA_ROPE // 2, 1) * sb

    gq, gk = gq_ref[...], gk_ref[...]
    kr_rot = rope(krp * gk[:, LANES:])
    kr_ss = jnp.sum(krp * krp, axis=-1, keepdims=True)
    qscale = 1.0 / math.sqrt(MLA_QK)
    for h in range(nh):
        qh = qf[:, h * HEAD_PAD:(h + 1) * HEAD_PAD]
        rq = lax.rsqrt(jnp.sum(qh * qh, axis=-1, keepdims=True) * (1.0 / MLA_QK) + EPS) * qscale
        qn = qh * rq * gq
        q_ref[:, h * HEAD_PAD:h * HEAD_PAD + LANES] = qn[:, :LANES].astype(q_ref.dtype)
        q_ref[:, h * HEAD_PAD + LANES:(h + 1) * HEAD_PAD] = rope(qn[:, LANES:]).astype(q_ref.dtype)
        knh = kn[:, h * MLA_NOPE:(h + 1) * MLA_NOPE]
        rk = lax.rsqrt((jnp.sum(knh * knh, axis=-1, keepdims=True) + kr_ss) * (1.0 / MLA_QK) + EPS)
        k_ref[:, h * HEAD_PAD:h * HEAD_PAD + LANES] = (knh * rk * gk[:, :LANES]).astype(k_ref.dtype)
        k_ref[:, h * HEAD_PAD + LANES:(h + 1) * HEAD_PAD] = (kr_rot * rk).astype(k_ref.dtype)


def _mla_prep(proj, wm, tabs, gqa, gkva, gq, gk, wq, wk, wv, qr, kvr, nh):
    t = proj.shape[0]
    tm = _tile(t, 256)
    full = lambda a: pl.BlockSpec(a.shape, lambda i: (0, 0))
    tab = pl.BlockSpec((tm, LANES), lambda i: (i, 0))
    return pl.pallas_call(
        functools.partial(_mla_prep_kernel, qr=qr, kvr=kvr, nh=nh),
        grid=(t // tm,),
        in_specs=[pl.BlockSpec((tm, wm), lambda i: (i, 0)), tab, tab, tab,
                  full(gqa), full(gkva), full(gq), full(gk), full(wq), full(wk), full(wv)],
        out_specs=[pl.BlockSpec((tm, nh * HEAD_PAD), lambda i: (i, 0)),
                   pl.BlockSpec((tm, nh * HEAD_PAD), lambda i: (i, 0)),
                   pl.BlockSpec((tm, nh * HEAD_PAD), lambda i: (i, 0))],
        out_shape=[jax.ShapeDtypeStruct((t, nh * HEAD_PAD), BF16),
                   jax.ShapeDtypeStruct((t, nh * HEAD_PAD), BF16),
                   jax.ShapeDtypeStruct((t, nh * HEAD_PAD), BF16)],
        compiler_params=_params(("arbitrary",)),
    )(proj, *tabs, gqa, gkva, gq, gk, wq, wk, wv)


def _flash_kernel(qi_ref, ki_ref, q_ref, k_ref, v_ref, o_ref, m_ref, acc_ref):
    p = pl.program_id(2)
    qi, ki = qi_ref[p], ki_ref[p]

    @pl.when(ki == 0)
    def _():
        m_ref[...] = jnp.full_like(m_ref, -jnp.inf)
        acc_ref[...] = jnp.zeros_like(acc_ref)

    def update(masked):
        blk = q_ref.shape[0]
        sub = min(blk, HEAD_PAD)
        for i in range(blk // sub):
            rows = slice(i * sub, (i + 1) * sub)
            nk = (i + 1) * sub if masked else blk
            s = _dot_nt(q_ref[rows, :], k_ref[:nk, :])
            if masked:
                row = lax.broadcasted_iota(jnp.int32, s.shape, 0) + i * sub
                col = lax.broadcasted_iota(jnp.int32, s.shape, 1)
                s = jnp.where(col <= row, s, -jnp.inf)
            m_prev = m_ref[rows, :]
            m_new = jnp.maximum(m_prev, jnp.max(s, axis=-1, keepdims=True))
            alpha = jnp.exp(m_prev - m_new)
            pe = jnp.exp((s - m_new).astype(BF16))
            acc_ref[rows, :] = alpha * acc_ref[rows, :] + _dot(pe, v_ref[:nk, :])
            m_ref[rows, :] = m_new

    @pl.when(ki < qi)
    def _():
        update(False)

    @pl.when(ki == qi)
    def _():
        update(True)
        acc = acc_ref[...]
        o_ref[...] = (acc[:, :MLA_V] / acc[:, MLA_V:MLA_V + 1]).astype(o_ref.dtype)


def _flash(q, k, v, batch, nh):
    t = q.shape[0]
    s = t // batch
    blk = _tile(s, 1024)
    nq = s // blk
    pairs = [(a, b) for a in range(nq) for b in range(a + 1)]
    qi = jnp.asarray([a for a, _ in pairs], jnp.int32)
    ki = jnp.asarray([b for _, b in pairs], jnp.int32)
    return pl.pallas_call(
        _flash_kernel,
        grid_spec=pltpu.PrefetchScalarGridSpec(
            num_scalar_prefetch=2,
            grid=(batch, nh, len(pairs)),
            in_specs=[pl.BlockSpec((blk, HEAD_PAD), lambda b, h, p, qi, ki: (b * nq + qi[p], h)),
                      pl.BlockSpec((blk, HEAD_PAD), lambda b, h, p, qi, ki: (b * nq + ki[p], h)),
                      pl.BlockSpec((blk, HEAD_PAD), lambda b, h, p, qi, ki: (b * nq + ki[p], h))],
            out_specs=pl.BlockSpec((blk, MLA_V), lambda b, h, p, qi, ki: (b * nq + qi[p], h)),
            scratch_shapes=[pltpu.VMEM((blk, 1), F32), pltpu.VMEM((blk, HEAD_PAD), F32)]),
        out_shape=jax.ShapeDtypeStruct((t, nh * MLA_V), BF16),
        compiler_params=_params(("arbitrary", "arbitrary", "arbitrary")),
    )(qi, ki, q, k, v)


def _wo_kernel(ret_ref, att_ref, w1_ref, w2_ref, x_ref, gate_ref, o_ref):
    mix = _dot(ret_ref[...], w1_ref[...]) + _dot(att_ref[...], w2_ref[...])
    o_ref[...] = x_ref[...] + gate_ref[0] * mix


def _wo(ret, att, w_o, x2, gate, batch):
    t, wr = ret.shape
    wa = att.shape[1]
    assert wr == wa
    d = w_o.shape[1]
    s = t // batch
    tm = _tile(s, 1024)
    tn = _tile(d, 512)
    ns = s // tm
    return pl.pallas_call(
        _wo_kernel,
        grid=(t // tm, d // tn),
        in_specs=[pl.BlockSpec((tm, wr), lambda i, j: (i, 0)),
                  pl.BlockSpec((tm, wa), lambda i, j: (i, 0)),
                  pl.BlockSpec((wr, tn), lambda i, j: (0, j)),
                  pl.BlockSpec((wa, tn), lambda i, j: (1, j)),
                  pl.BlockSpec((tm, tn), lambda i, j: (i, j)),
                  pl.BlockSpec((1, 1, tn), lambda i, j: (i // ns, 0, j))],
        out_specs=pl.BlockSpec((tm, tn), lambda i, j: (i, j)),
        out_shape=jax.ShapeDtypeStruct((t, d), F32),
        compiler_params=_params(("arbitrary", "arbitrary")),
    )(ret, att, w_o, w_o, x2, gate)


def _router_kernel(x_ref, g_ref, sc_ref, sh_ref, wrh_ref, wrl_ref, bias_ref, wsg_ref, wsu_ref,
                   h_ref, a_ref, e_ref, w_ref):
    h = _norm_mod(x_ref[...], g_ref[...], sc_ref[0], sh_ref[0])
    hb = h.astype(BF16)
    hbf = hb.astype(F32)
    _store_slabs(h_ref, _pack_pairs(hbf))
    a_ref[...] = (_silu(_dot(hb, wsg_ref[...])) * _dot(hb, wsu_ref[...])).astype(a_ref.dtype)

    hl = (h - hbf).astype(BF16)
    wrh = wrh_ref[...]
    logits = _dot_nt(wrh, hb) + _dot_nt(wrh, hl) + _dot_nt(wrl_ref[...], hb)
    scores = jax.nn.sigmoid(logits)
    sel = scores + bias_ref[...]
    ne, tm = sel.shape
    per_group = ne // N_GROUPS
    neg = -jnp.inf

    def first_argmax(v, iota, size):
        m = jnp.max(v, axis=0, keepdims=True)
        idx = jnp.min(jnp.where(v == m, iota, size), axis=0, keepdims=True)
        return m, idx

    iota_pg = lax.broadcasted_iota(jnp.int32, (per_group, tm), 0)
    gs = []
    for g in range(N_GROUPS):
        sg = sel[g * per_group:(g + 1) * per_group, :]
        m1, i1 = first_argmax(sg, iota_pg, per_group)
        m2 = jnp.max(jnp.where(iota_pg == i1, neg, sg), axis=0, keepdims=True)
        gs.append(m1 + m2)
    gsc = jnp.concatenate(gs, axis=0)
    iota_g = lax.broadcasted_iota(jnp.int32, (N_GROUPS, tm), 0)
    gmask = jnp.zeros((N_GROUPS, tm), jnp.bool_)
    for _ in range(TOPK_GROUPS):
        _, ig = first_argmax(gsc, iota_g, N_GROUPS)
        hit = iota_g == ig
        gmask = jnp.logical_or(gmask, hit)
        gsc = jnp.where(hit, neg, gsc)
    emask = jnp.concatenate(
        [jnp.broadcast_to(gmask[g:g + 1, :], (per_group, tm)) for g in range(N_GROUPS)], axis=0)
    cand = jnp.where(emask, sel, neg)
    iota_e = lax.broadcasted_iota(jnp.int32, (ne, tm), 0)
    es, ws = [], []
    for _ in range(TOP_K):
        _, ie = first_argmax(cand, iota_e, ne)
        hit = iota_e == ie
        es.append(ie)
        ws.append(jnp.sum(jnp.where(hit, scores, 0.0), axis=0, keepdims=True))
        cand = jnp.where(hit, neg, cand)
    wsum = ws[0]
    for wk in ws[1:]:
        wsum = wsum + wk
    e_ref[...] = jnp.concatenate(es, axis=0)
    w_ref[...] = jnp.concatenate(ws, axis=0) / wsum * ROUTED_SCALE


def _router(x2, g, scale, shift, wrh, wrl, bias, wsg, wsu, batch):
    t, d = x2.shape
    s = t // batch
    tm = _tile(s, 256)
    ns = s // tm
    ne = wrh.shape[0]
    f = wsg.shape[1]
    full = lambda a: pl.BlockSpec(a.shape, lambda i: (0,) * a.ndim)
    bat = pl.BlockSpec((1, 1, d), lambda i: (i // ns, 0, 0))
    return pl.pallas_call(
        _router_kernel,
        grid=(t // tm,),
        in_specs=[pl.BlockSpec((tm, d), lambda i: (i, 0)), full(g), bat, bat,
                  full(wrh), full(wrl), full(bias), full(wsg), full(wsu)],
        out_specs=[pl.BlockSpec((tm * (d // 2 // LANES), LANES), lambda i: (i, 0)),
                   pl.BlockSpec((tm, f), lambda i: (i, 0)),
                   pl.BlockSpec((TOP_K, tm), lambda i: (0, i)),
                   pl.BlockSpec((TOP_K, tm), lambda i: (0, i))],
        out_shape=[jax.ShapeDtypeStruct((t * (d // 2 // LANES), LANES), jnp.uint32),
                   jax.ShapeDtypeStruct((t, f), BF16),
                   jax.ShapeDtypeStruct((TOP_K, t), jnp.int32),
                   jax.ShapeDtypeStruct((TOP_K, t), F32)],
        compiler_params=_params(("arbitrary",)),
    )(x2, g, scale, shift, wrh, wrl, bias, wsg, wsu)


def _moe_kernel(tn_ref, ws_ref, nx_ref, ulo_ref, uhi_ref, tok_tab, dst_tab, h_hbm, wg_hbm, wu_hbm, wd_hbm, y_hbm,
                xbuf, ybuf, wgb, wub, wdb, stg_a, stg_d, tok_idx, dst_idx, gsem, ssem, isem, wsem):
    g = pl.program_id(0)
    n_grid = pl.num_programs(0)
    tm = tok_idx.shape[2]
    per = xbuf.shape[1] // tm
    dump_row = dst_tab.shape[0] - 1
    nxt = jnp.minimum(g + 1, n_grid - 1)
    active = tn_ref[g] > 0
    next_active = jnp.logical_and(g + 1 < n_grid, tn_ref[nxt] > 0)
    last_active = jnp.logical_and(active, jnp.logical_not(next_active))

    def issue_gather(ib, b):
        for r in range(tm):
            src = pl.multiple_of(tok_idx[ib, 0, r], per)
            pltpu.make_async_copy(h_hbm.at[pl.ds(src, per), :], xbuf.at[b, pl.ds(r * per, per), :],
                                  gsem.at[b]).start()

    def issue_scatter(ib, b):
        for r in range(tm):
            dst = pl.multiple_of(dst_idx[ib, 0, r], per)
            pltpu.make_async_copy(ybuf.at[b, pl.ds(r * per, per), :], y_hbm.at[pl.ds(dst, per), :],
                                  ssem.at[b]).start()

    def wait_gather(b):
        pltpu.make_async_copy(h_hbm.at[pl.ds(0, tm * per), :], xbuf.at[b], gsem.at[b]).wait()

    def wait_scatter(b):
        pltpu.make_async_copy(ybuf.at[b], y_hbm.at[pl.ds(0, tm * per), :], ssem.at[b]).wait()

    def tok_copy(row, b):
        return pltpu.make_async_copy(tok_tab.at[row], tok_idx.at[b], isem.at[b])

    def dst_copy(row, b):
        return pltpu.make_async_copy(dst_tab.at[row], dst_idx.at[b], isem.at[b])

    ca, cd = stg_a.shape[1], stg_d.shape[1]
    n_gu = wg_hbm.shape[1] // ca
    n_units = 2 * n_gu + wd_hbm.shape[1] // cd

    def unit_parts(e, u, slot):
        sb = lax.rem(u, 2)
        rg = pl.multiple_of(u * ca, ca)
        ru = pl.multiple_of((u - n_gu) * ca, ca)
        rd = pl.multiple_of((u - 2 * n_gu) * cd, cd)
        return sb, (
            (u < n_gu, wg_hbm.at[e, pl.ds(rg, ca), :], stg_a, wgb.at[slot, pl.ds(rg, ca), :]),
            (jnp.logical_and(u >= n_gu, u < 2 * n_gu), wu_hbm.at[e, pl.ds(ru, ca), :], stg_a,
             wub.at[slot, pl.ds(ru, ca), :]),
            (u >= 2 * n_gu, wd_hbm.at[e, pl.ds(rd, cd), :], stg_d, wdb.at[slot, pl.ds(rd, cd), :]))

    def unit_dma(e, u, start):
        sb, parts = unit_parts(e, u, 0)
        for pred, src, stg, _ in parts:
            @pl.when(pred)
            def _(src=src, stg=stg):
                copy = pltpu.make_async_copy(src, stg.at[sb], wsem.at[sb])
                copy.start() if start else copy.wait()

    def unit_cast(e, u, slot):
        sb, parts = unit_parts(e, u, slot)
        for pred, _, stg, dst in parts:
            @pl.when(pred)
            def _(stg=stg, dst=dst):
                dst[...] = stg[sb].astype(dst.dtype)

    def start_units(e, lo, hi):
        for k in range(2):
            @pl.when(lo + k < hi)
            def _(k=k):
                unit_dma(e, lo + k, True)

    def finish_units(e, slot, lo, hi):
        def body(u, carry):
            unit_dma(e, u, False)
            unit_cast(e, u, slot)

            @pl.when(u + 2 < hi)
            def _():
                unit_dma(e, u + 2, True)

            return carry

        lax.fori_loop(lo, hi, body, 0)

    @pl.when(g == 0)
    def _():
        start_units(nx_ref[n_grid], 0, n_units)
        finish_units(nx_ref[n_grid], ws_ref[0], 0, n_units)
        ybuf[...] = jnp.zeros_like(ybuf)
        for b in range(2):
            dump = pltpu.make_async_copy(
                ybuf.at[b], y_hbm.at[pl.ds(y_hbm.shape[0] - (2 - b) * tm * per, tm * per), :], ssem.at[b])
            dump.start()
            dump.wait()
        first = tok_copy(0, 1)
        first.start()
        first.wait()
        issue_gather(1, 0)
        tok_copy(nxt, 0).start()
        dst_copy(dump_row, 0).start()

    def step(b):
        tok_copy(0, b).wait()
        dst_copy(0, b).wait()

        @pl.when(next_active)
        def _():
            tok_copy(jnp.minimum(g + 2, n_grid - 1), 1 - b).start()
            dst_copy(g, 1 - b).start()

        wait_gather(b)

        @pl.when(g >= 1)
        def _():
            wait_scatter(b)

        lo, hi = _unpack_pairs(_load_slabs(xbuf.at[b], tm))
        lo, hi = lo.astype(BF16), hi.astype(BF16)
        ws = ws_ref[g]
        wg_ref, wu_ref, wd_ref = wgb.at[ws], wub.at[ws], wdb.at[ws]
        half = wg_ref.shape[0] // 2
        issue_gather(b, 1 - b)
        gate = _dot(lo, wg_ref[:half, :]) + _dot(hi, wg_ref[half:, :])
        up = _dot(lo, wu_ref[:half, :]) + _dot(hi, wu_ref[half:, :])
        issue_scatter(b, 1 - b)
        y = _dot((_silu(gate) * up).astype(BF16), wd_ref[...])
        _store_slabs(ybuf.at[b], _pack_pairs(_round_bf16(y)))

        @pl.when(last_active)
        def _():
            own = dst_copy(g, 1 - b)
            own.start()
            own.wait()
            issue_scatter(1 - b, b)
            wait_scatter(b)
            wait_scatter(1 - b)
            wait_gather(1 - b)

    @pl.when(active)
    def _():
        start_units(nx_ref[g], ulo_ref[g], uhi_ref[g])

    for b in range(2):
        @pl.when(jnp.logical_and(active, lax.rem(g, 2) == b))
        def _(b=b):
            step(b)

    @pl.when(active)
    def _():
        finish_units(nx_ref[g], 1 - ws_ref[g], ulo_ref[g], uhi_ref[g])


def _moe(sched, tok_tab, dst_tab, h2s, wg, wu, wd, n_tok, tm):
    ne, d, f = wg.shape
    per = d // 2 // LANES
    n_tiles = sched[0].shape[0]
    n_rows = TOP_K * n_tok + 2 * tm
    ca, cd = _tile(d, MOE_GU_ROWS), _tile(f, MOE_DN_ROWS)
    buf = pltpu.VMEM((2, tm * per, LANES), jnp.uint32)
    idx = pltpu.SMEM((2, 1, tm), jnp.int32)
    sems = pltpu.SemaphoreType.DMA((2,))
    hbm = pl.BlockSpec(memory_space=pl.ANY)
    return pl.pallas_call(
        _moe_kernel,
        grid_spec=pltpu.PrefetchScalarGridSpec(
            num_scalar_prefetch=len(sched),
            grid=(n_tiles,),
            in_specs=[hbm] * 6,
            out_specs=hbm,
            scratch_shapes=[buf, buf,
                            pltpu.VMEM((2, d, f), BF16), pltpu.VMEM((2, d, f), BF16), pltpu.VMEM((2, f, d), BF16),
                            pltpu.VMEM((2, ca, f), F32), pltpu.VMEM((2, cd, d), F32),
                            idx, idx, sems, sems, sems, sems]),
        out_shape=jax.ShapeDtypeStruct((n_rows * per, LANES), jnp.uint32),
        compiler_params=_params(("arbitrary",)),
    )(*sched, tok_tab, dst_tab, h2s, wg, wu, wd)


def _moe_tables(order, tile_row0, tile_nv, n_tok, tm, per):
    assert n_tok & (n_tok - 1) == 0, "token count must be a power of two (slot index packing)"
    p = order.shape[0]
    n_tiles = tile_row0.shape[0]
    r = jnp.arange(tm, dtype=jnp.int32)[None, :]
    slot = order[jnp.minimum(tile_row0[:, None] + r, p - 1)]
    parity = (jnp.arange(n_tiles, dtype=jnp.int32) % 2)[:, None]
    dump = p + parity * tm + r
    dst = jnp.where(r < tile_nv[:, None], slot, dump)
    dst = jnp.concatenate([dst, p + tm + r], axis=0)
    tok = slot & (n_tok - 1)
    return (tok * per)[:, None, :], (dst * per)[:, None, :]


def _moe_schedule(e_idx, ne, tm, n_units):
    k, t = e_idx.shape
    p = k * t
    keys = e_idx.reshape(p)
    _, order = lax.sort((keys, jnp.arange(p, dtype=jnp.int32)), num_keys=1, is_stable=True)
    experts = jnp.arange(ne, dtype=jnp.int32)
    counts = jnp.sum((keys[None, :] == experts[:, None]).astype(jnp.int32), axis=1)
    starts = jnp.cumsum(counts) - counts
    ntile = (counts + tm - 1) // tm
    cum = jnp.cumsum(ntile)
    n_tiles = p // tm + ne
    gidx = jnp.arange(n_tiles, dtype=jnp.int32)
    te = jnp.minimum(jnp.sum((gidx[:, None] >= cum[None, :]).astype(jnp.int32), axis=1), ne - 1)
    onehot = (te[:, None] == experts[None, :]).astype(jnp.int32)
    pick = lambda v: jnp.sum(onehot * v[None, :], axis=1)
    local = gidx - pick(cum - ntile)
    nv = jnp.where(gidx < cum[-1], jnp.clip(pick(counts) - local * tm, 0, tm), 0).astype(jnp.int32)
    row0 = jnp.where(nv > 0, pick(starts) + local * tm, 0).astype(jnp.int32)
    nonempty = ntile > 0
    slot_e = (jnp.cumsum(nonempty.astype(jnp.int32)) - 1) % 2
    masked_id = jnp.where(nonempty, experts, ne)
    next_e = jnp.concatenate([lax.cummin(masked_id[::-1])[::-1][1:], jnp.full((1,), ne, jnp.int32)])
    nt, nx = pick(ntile), pick(next_e)
    has_next = jnp.logical_and(nv > 0, nx < ne)
    ulo = jnp.where(has_next, (n_units * local) // jnp.maximum(nt, 1), 0).astype(jnp.int32)
    uhi = jnp.where(has_next, (n_units * (local + 1)) // jnp.maximum(nt, 1), 0).astype(jnp.int32)
    nx = jnp.concatenate([jnp.minimum(nx, ne - 1), te[:1]]).astype(jnp.int32)
    return order, row0, nv, (nv, pick(slot_e).astype(jnp.int32), nx, ulo, uhi)


def _combine_kernel(*refs):
    y_refs = refs[:TOP_K]
    w_ref, a_ref, wsd_ref, x_ref, gate_ref, o_ref = refs[TOP_K:]
    tm = x_ref.shape[0]
    half = x_ref.shape[1] // 2
    w = w_ref[...]
    lo, hi = _unpack_pairs(_load_slabs(y_refs[0], tm))
    acc_lo, acc_hi = w[:, 0:1] * lo, w[:, 0:1] * hi
    for k in range(1, TOP_K):
        lo, hi = _unpack_pairs(_load_slabs(y_refs[k], tm))
        acc_lo = acc_lo + w[:, k:k + 1] * lo
        acc_hi = acc_hi + w[:, k:k + 1] * hi
    a = a_ref[...]
    gate = gate_ref[0]
    o_ref[:, :half] = x_ref[:, :half] + gate[:, :half] * (acc_lo + _dot(a, wsd_ref[:, :half]))
    o_ref[:, half:] = x_ref[:, half:] + gate[:, half:] * (acc_hi + _dot(a, wsd_ref[:, half:]))


def _combine(y, w_tk, a_sh, wsd, x2, gate, batch):
    t, d = x2.shape
    per = d // 2 // LANES
    kk = w_tk.shape[1]
    assert kk == TOP_K
    f = a_sh.shape[1]
    s = t // batch
    tm = _tile(s, 128)
    ns = s // tm
    nt = t // tm
    return pl.pallas_call(
        _combine_kernel,
        grid=(nt,),
        in_specs=[pl.BlockSpec((tm * per, LANES), lambda i, k=k: (k * nt + i, 0)) for k in range(TOP_K)] + [
                  pl.BlockSpec((tm, kk), lambda i: (i, 0)),
                  pl.BlockSpec((tm, f), lambda i: (i, 0)),
                  pl.BlockSpec((f, d), lambda i: (0, 0)),
                  pl.BlockSpec((tm, d), lambda i: (i, 0)),
                  pl.BlockSpec((1, 1, d), lambda i: (i // ns, 0, 0))],
        out_specs=pl.BlockSpec((tm, d), lambda i: (i, 0)),
        out_shape=jax.ShapeDtypeStruct((t, d), F32),
        compiler_params=_params(("arbitrary",)),
    )(*([y] * TOP_K), w_tk, a_sh, wsd, x2, gate)


def _round_up(v, m):
    return (v + m - 1) // m * m


def kernel(x, c, positions, w_ada, b_ada, norm1_g, w_in, q_a_norm_g, w_uq, kv_a_norm_g, w_ukv, q_norm_g, k_norm_g, w_o, norm2_g, w_router, router_bias, w_exp_gate, w_exp_up, w_exp_down, w_sh_gate, w_sh_up, w_sh_down):
    b, s, d = x.shape
    t = b * s
    depth = w_ada.shape[0]
    qr = q_a_norm_g.shape[1]
    kvr = kv_a_norm_g.shape[1]
    nh = w_uq.shape[2] // MLA_QK
    in_w = w_in.shape[2]
    ret_heads = (in_w - qr - kvr - MLA_ROPE) // (2 * RET_DK + 2 * RET_DV)
    ret_w = ret_heads * RET_DK
    ne = w_router.shape[2]
    moe_tm = 256

    pos = positions.reshape(t).astype(F32)
    inv_r = ROPE_BASE ** (-jnp.arange(0, RET_DK, 2, dtype=F32) / RET_DK)
    ang_r = pos[:, None] * inv_r
    cos_r, sin_r = jnp.cos(ang_r), jnp.sin(ang_r)
    inv_m = ROPE_BASE ** (-jnp.arange(0, MLA_ROPE, 2, dtype=F32) / MLA_ROPE)
    ang_m = pos[:, None] * inv_m
    cm, sm = jnp.cos(ang_m), jnp.sin(ang_m)
    z = jnp.zeros_like(cm)
    mla_tabs = (jnp.concatenate([cm, cm, z, z], axis=-1),
                jnp.concatenate([-sm, z, z, z], axis=-1),
                jnp.concatenate([z, sm, z, z], axis=-1))
    log_g = jnp.log(1.0 - 2.0 ** (-5.0 - jnp.arange(ret_heads, dtype=F32)))

    c_pad = jnp.zeros((8, d), F32).at[:b].set(c)
    x2 = x.reshape(t, d)
    wm = _round_up(qr + kvr + LANES, HEAD_PAD)
    ret_cols = 4 * ret_w

    for l in range(depth):
        mod = _ada(c_pad, w_ada[l], b_ada[l][None, :])[:b]
        shift_a, scale_a, gate_a, shift_m, scale_m, gate_m = (
            m.reshape(b, 1, d) for m in jnp.split(mod, 6, axis=-1))

        h1 = _norm_mod_call(x2, norm1_g[l][None, :], scale_a, shift_a, b, BF16)
        w_in_t = jnp.swapaxes(w_in, 1, 2)
        proj_ret = _mm_wcast(h1, w_in_t, l, 0, ret_cols, BF16)
        w_mla_t = jnp.pad(w_in_t[l, ret_cols:, :], ((0, wm - (in_w - ret_cols)), (0, 0))).astype(BF16)
        proj = _mm(h1, w_mla_t, BF16)

        ret = _retention(proj_ret, cos_r, sin_r, log_g, b, ret_heads, 0)

        wq = jnp.pad(w_uq[l].reshape(qr, nh, MLA_QK), ((0, 0), (0, 0), (0, HEAD_PAD - MLA_QK)))
        wq = wq.reshape(qr, nh * HEAD_PAD).astype(BF16)
        wkv = w_ukv[l].reshape(kvr, nh, MLA_NOPE + MLA_V)
        wk = wkv[:, :, :MLA_NOPE].reshape(kvr, nh * MLA_NOPE).astype(BF16)
        wv = wkv[:, :, MLA_NOPE:].reshape(kvr, nh * MLA_V).astype(BF16)
        gq = jnp.pad(q_norm_g[l], (0, HEAD_PAD - MLA_QK))[None, :]
        gk = jnp.pad(k_norm_g[l], (0, HEAD_PAD - MLA_QK))[None, :]
        q, k, v = _mla_prep(proj, wm, mla_tabs, q_a_norm_g[l][None, :], kv_a_norm_g[l][None, :],
                            gq, gk, wq, wk, wv, qr, kvr, nh)
        att = _flash(q, k, v, b, nh)

        x2 = _wo(ret, att, w_o[l].astype(BF16), x2, gate_a, b)

        wr_t = w_router[l].T
        wrh = wr_t.astype(BF16)
        wrl = (wr_t - wrh.astype(F32)).astype(BF16)
        h2p, a_sh, e_idx, w_kt = _router(x2, norm2_g[l][None, :], scale_m, shift_m, wrh, wrl,
                                         router_bias[l][:, None], w_sh_gate[l].astype(BF16),
                                         w_sh_up[l].astype(BF16), b)
        ff = w_exp_gate.shape[3]
        n_units = 2 * (d // _tile(d, MOE_GU_ROWS)) + ff // _tile(ff, MOE_DN_ROWS)
        order, tile_row0, tile_nv, sched = _moe_schedule(e_idx, ne, moe_tm, n_units)
        tok_tab, dst_tab = _moe_tables(order, tile_row0, tile_nv, t, moe_tm, d // 2 // LANES)
        y = _moe(sched, tok_tab, dst_tab, h2p, w_exp_gate[l], w_exp_up[l], w_exp_down[l], t, moe_tm)
        x2 = _combine(y, w_kt.T, a_sh, w_sh_down[l].astype(BF16), x2, gate_m, b)

    return x2.reshape(b, s, d)
```

```python
import functools
import math

import jax
import jax.numpy as jnp
from jax import lax
from jax.experimental import pallas as pl
from jax.experimental.pallas import tpu as pltpu

F32 = jnp.float32
BF16 = jnp.bfloat16

EPS = 1e-6
ROPE_BASE = 10000.0
RET_DK = 256
RET_DV = 256
MLA_NOPE = 128
MLA_ROPE = 64
MLA_V = 128
MLA_QK = MLA_NOPE + MLA_ROPE
N_GROUPS = 8
TOPK_GROUPS = 4
TOP_K = 8
ROUTED_SCALE = 2.5

LANES = 128
SUBLANES = 8
HEAD_PAD = 2 * LANES
VMEM_LIMIT = 56 * 1024 * 1024
MOE_GU_ROWS = 512
MOE_DN_ROWS = 64
MOE_STAGE_DEPTH = 4


def _tile(dim, pref):
    t = min(dim, pref)
    while dim % t:
        t //= 2
    return t


def _params(sem):
    return pltpu.CompilerParams(dimension_semantics=sem, vmem_limit_bytes=VMEM_LIMIT)


def _silu(v):
    return v * jax.nn.sigmoid(v)


def _dot(a, b):
    return jnp.dot(a, b, preferred_element_type=F32)


def _dot_nt(a, b):
    return lax.dot_general(a, b, (((1,), (1,)), ((), ())), preferred_element_type=F32)


def _dot_tn(a, b):
    return lax.dot_general(a, b, (((0,), (0,)), ((), ())), preferred_element_type=F32)


HI_HALF = 0xFFFF0000


def _pack_pairs(v):
    half = v.shape[1] // 2
    bits = lax.bitcast_convert_type(v, jnp.uint32)
    return (bits[:, :half] >> 16) | (bits[:, half:] & jnp.uint32(HI_HALF))


def _unpack_pairs(w):
    lo = lax.bitcast_convert_type(w << 16, F32)
    hi = lax.bitcast_convert_type(w & jnp.uint32(HI_HALF), F32)
    return lo, hi


def _round_bf16(v):
    return v.astype(BF16).astype(F32)


def _store_slabs(ref, v):
    rows, n = v.shape
    per = n // LANES
    for s in range(per):
        ref[pl.ds(s, rows, stride=per), :] = v[:, s * LANES:(s + 1) * LANES]


def _load_slabs(ref, rows):
    per = ref.shape[0] // rows
    return jnp.concatenate([ref[pl.ds(s, rows, stride=per), :] for s in range(per)], axis=1)


def _ada_kernel(c_ref, w_ref, b_ref, o_ref):
    ca = _silu(c_ref[...]).astype(BF16)
    o_ref[...] = _dot(ca, w_ref[...].astype(BF16)) + b_ref[...]


def _ada(c_pad, w, b):
    m, d = c_pad.shape
    n = w.shape[1]
    tn = _tile(n, 512)
    return pl.pallas_call(
        _ada_kernel,
        grid=(n // tn,),
        in_specs=[pl.BlockSpec((m, d), lambda j: (0, 0)),
                  pl.BlockSpec((d, tn), lambda j: (0, j)),
                  pl.BlockSpec((1, tn), lambda j: (0, j))],
        out_specs=pl.BlockSpec((m, tn), lambda j: (0, j)),
        out_shape=jax.ShapeDtypeStruct((m, n), F32),
        compiler_params=_params(("arbitrary",)),
    )(c_pad, w, b)


def _norm_mod(x, g, scale, shift):
    xf = x.astype(F32)
    h = xf * lax.rsqrt(jnp.mean(xf * xf, axis=-1, keepdims=True) + EPS) * g
    return h * (1.0 + scale) + shift


def _norm_mod_kernel(x_ref, g_ref, sc_ref, sh_ref, o_ref):
    o_ref[...] = _norm_mod(x_ref[...], g_ref[...], sc_ref[0], sh_ref[0]).astype(o_ref.dtype)


def _norm_mod_call(x2, g, scale, shift, batch, out_dtype):
    t, d = x2.shape
    s = t // batch
    ts = _tile(s, 512)
    ns = s // ts
    return pl.pallas_call(
        _norm_mod_kernel,
        grid=(batch, ns),
        in_specs=[pl.BlockSpec((ts, d), lambda b, i: (b * ns + i, 0)),
                  pl.BlockSpec((1, d), lambda b, i: (0, 0)),
                  pl.BlockSpec((1, 1, d), lambda b, i: (b, 0, 0)),
                  pl.BlockSpec((1, 1, d), lambda b, i: (b, 0, 0))],
        out_specs=pl.BlockSpec((ts, d), lambda b, i: (b * ns + i, 0)),
        out_shape=jax.ShapeDtypeStruct((t, d), out_dtype),
        compiler_params=_params(("arbitrary", "arbitrary")),
    )(x2, g, scale, shift)


def _mm_kernel(a_ref, w_ref, o_ref):
    o_ref[...] = _dot_nt(a_ref[...], w_ref[...]).astype(o_ref.dtype)


def _mm(a, wt, out_dtype, tm_pref=1024, tn_pref=512):
    m, k = a.shape
    n = wt.shape[0]
    tm, tn = _tile(m, tm_pref), _tile(n, tn_pref)
    return pl.pallas_call(
        _mm_kernel,
        grid=(m // tm, n // tn),
        in_specs=[pl.BlockSpec((tm, k), lambda i, j: (i, 0)),
                  pl.BlockSpec((tn, k), lambda i, j: (j, 0))],
        out_specs=pl.BlockSpec((tm, tn), lambda i, j: (i, j)),
        out_shape=jax.ShapeDtypeStruct((m, n), out_dtype),
        compiler_params=_params(("arbitrary", "arbitrary")),
    )(a, wt)


def _mm_wcast_kernel(a_ref, w_ref, o_ref, wb_ref):
    @pl.when(pl.program_id(1) == 0)
    def _():
        wb_ref[...] = w_ref[...].astype(wb_ref.dtype)

    o_ref[...] = _dot_nt(a_ref[...], wb_ref[...]).astype(o_ref.dtype)


def _mm_wcast(a, wt3, layer, col0, n, out_dtype, tm_pref=1024, tn_pref=512):
    m, k = a.shape
    tm, tn = _tile(m, tm_pref), _tile(n, tn_pref)
    assert col0 % tn == 0
    j0 = col0 // tn
    return pl.pallas_call(
        _mm_wcast_kernel,
        grid=(n // tn, m // tm),
        in_specs=[pl.BlockSpec((tm, k), lambda j, i: (i, 0)),
                  pl.BlockSpec((None, tn, k), lambda j, i: (layer, j0 + j, 0))],
        out_specs=pl.BlockSpec((tm, tn), lambda j, i: (i, j)),
        out_shape=jax.ShapeDtypeStruct((m, n), out_dtype),
        scratch_shapes=[pltpu.VMEM((tn, k), BF16)],
        compiler_params=_params(("arbitrary", "arbitrary")),
    )(a, wt3)


def _ret_kernel(lg_ref, q_ref, k_ref, v_ref, g_ref, cos_ref, sin_ref, o_ref, state_ref):
    h = pl.program_id(1)

    @pl.when(pl.program_id(2) == 0)
    def _():
        state_ref[...] = jnp.zeros_like(state_ref)

    lg = lg_ref[h]
    c = q_ref.shape[0]
    cos, sin = cos_ref[...], sin_ref[...]
    half = RET_DK // 2

    def rope(v):
        v1, v2 = v[:, :half], v[:, half:]
        return jnp.concatenate([v1 * cos - v2 * sin, v2 * cos + v1 * sin], axis=-1)

    q = rope(q_ref[...].astype(F32))
    k = rope(k_ref[...].astype(F32)) * (RET_DK ** -0.5)
    v = v_ref[...]
    ii = lax.broadcasted_iota(jnp.int32, (c, c), 0)
    jj = lax.broadcasted_iota(jnp.int32, (c, c), 1)
    diff = (ii - jj).astype(F32)
    inner_decay = jnp.where(diff >= 0, jnp.exp(diff * lg), 0.0)
    ri = lax.broadcasted_iota(jnp.int32, (c, 1), 0).astype(F32)
    q_decay = jnp.exp((ri + 1.0) * lg)
    k_decay = jnp.exp((c - 1.0 - ri) * lg)
    chunk_decay = jnp.exp(jnp.full((1, 1), c, F32) * lg)

    qb = q.astype(BF16)
    sc = _dot_nt(qb, k.astype(BF16)) * inner_decay
    inner = _dot(sc.astype(BF16), v)
    state = state_ref[...]
    cross = _dot(qb, state.astype(BF16)) * q_decay
    kv = _dot_tn((k * k_decay).astype(BF16), v)
    state_ref[...] = state * chunk_decay + kv
    out = inner + cross
    r = out * lax.rsqrt(jnp.mean(out * out, axis=-1, keepdims=True) + EPS)
    o_ref[...] = (r * _silu(g_ref[...].astype(F32))).astype(o_ref.dtype)


def _retention(proj, cos_r, sin_r, log_g, batch, n_heads, col0):
    t = proj.shape[0]
    s = t // batch
    c = _tile(s, 512)
    n = s // c
    hh = n_heads

    def blk(off):
        return pl.BlockSpec((c, RET_DK), lambda b, h, i, lg: (b * n + i, col0 + off + h))

    tab = pl.BlockSpec((c, RET_DK // 2), lambda b, h, i, lg: (b * n + i, 0))
    return pl.pallas_call(
        _ret_kernel,
        grid_spec=pltpu.PrefetchScalarGridSpec(
            num_scalar_prefetch=1,
            grid=(batch, hh, n),
            in_specs=[blk(0), blk(hh), blk(2 * hh), blk(3 * hh), tab, tab],
            out_specs=pl.BlockSpec((c, RET_DV), lambda b, h, i, lg: (b * n + i, h)),
            scratch_shapes=[pltpu.VMEM((RET_DK, RET_DV), F32)]),
        out_shape=jax.ShapeDtypeStruct((t, hh * RET_DV), BF16),
        compiler_params=_params(("arbitrary", "arbitrary", "arbitrary")),
    )(log_g, proj, proj, proj, proj, cos_r, sin_r)


def _mla_prep_kernel(p_ref, ct_ref, sa_ref, sb_ref, gqa_ref, gkva_ref, gq_ref, gk_ref,
                     wq_ref, wk_ref, wv_ref, q_ref, k_ref, v_ref, *, qr, kvr, nh):
    p = p_ref[...].astype(F32)
    cq = p[:, :qr]
    ckv = p[:, qr:qr + kvr]
    krp = p[:, qr + kvr:qr + kvr + LANES]

    def rms(v, n):
        return v * lax.rsqrt(jnp.sum(v * v, axis=-1, keepdims=True) * (1.0 / n) + EPS)

    cqn = (rms(cq, qr) * gqa_ref[...]).astype(BF16)
    ckvn = (rms(ckv, kvr) * gkva_ref[...]).astype(BF16)
    qf = _dot(cqn, wq_ref[...])
    kn = _dot(ckvn, wk_ref[...])
    vf = _dot(ckvn, wv_ref[...])
    ones_col = (lax.broadcasted_iota(jnp.int32, (p.shape[0], LANES), 1) == 0).astype(v_ref.dtype)
    for h in range(nh):
        v_ref[:, h * HEAD_PAD:h * HEAD_PAD + MLA_V] = vf[:, h * MLA_V:(h + 1) * MLA_V].astype(v_ref.dtype)
        v_ref[:, h * HEAD_PAD + MLA_V:(h + 1) * HEAD_PAD] = ones_col

    ct, sa, sb = ct_ref[...], sa_ref[...], sb_ref[...]

    def rope(v):
        return v * ct + pltpu.roll(v, LANES - MLA_ROPE // 2, 1) * sa + pltpu.roll(v, MLA_ROPE // 2, 1) * sb

    gq, gk = gq_ref[...], gk_ref[...]
    kr_rot = rope(krp * gk[:, LANES:])
    kr_ss = jnp.sum(krp * krp, axis=-1, keepdims=True)
    qscale = 1.0 / math.sqrt(MLA_QK)
    for h in range(nh):
        qh = qf[:, h * HEAD_PAD:(h + 1) * HEAD_PAD]
        rq = lax.rsqrt(jnp.sum(qh * qh, axis=-1, keepdims=True) * (1.0 / MLA_QK) + EPS) * qscale
        qn = qh * rq * gq
        q_ref[:, h * HEAD_PAD:h * HEAD_PAD + LANES] = qn[:, :LANES].astype(q_ref.dtype)
        q_ref[:, h * HEAD_PAD + LANES:(h + 1) * HEAD_PAD] = rope(qn[:, LANES:]).astype(q_ref.dtype)
        knh = kn[:, h * MLA_NOPE:(h + 1) * MLA_NOPE]
        rk = lax.rsqrt((jnp.sum(knh * knh, axis=-1, keepdims=True) + kr_ss) * (1.0 / MLA_QK) + EPS)
        k_ref[:, h * HEAD_PAD:h * HEAD_PAD + LANES] = (knh * rk * gk[:, :LANES]).astype(k_ref.dtype)
        k_ref[:, h * HEAD_PAD + LANES:(h + 1) * HEAD_PAD] = (kr_rot * rk).astype(k_ref.dtype)


def _mla_prep(proj, wm, tabs, gqa, gkva, gq, gk, wq, wk, wv, qr, kvr, nh):
    t = proj.shape[0]
    tm = _tile(t, 256)
    full = lambda a: pl.BlockSpec(a.shape, lambda i: (0, 0))
    tab = pl.BlockSpec((tm, LANES), lambda i: (i, 0))
    return pl.pallas_call(
        functools.partial(_mla_prep_kernel, qr=qr, kvr=kvr, nh=nh),
        grid=(t // tm,),
        in_specs=[pl.BlockSpec((tm, wm), lambda i: (i, 0)), tab, tab, tab,
                  full(gqa), full(gkva), full(gq), full(gk), full(wq), full(wk), full(wv)],
        out_specs=[pl.BlockSpec((tm, nh * HEAD_PAD), lambda i: (i, 0)),
                   pl.BlockSpec((tm, nh * HEAD_PAD), lambda i: (i, 0)),
                   pl.BlockSpec((tm, nh * HEAD_PAD), lambda i: (i, 0))],
        out_shape=[jax.ShapeDtypeStruct((t, nh * HEAD_PAD), BF16),
                   jax.ShapeDtypeStruct((t, nh * HEAD_PAD), BF16),
                   jax.ShapeDtypeStruct((t, nh * HEAD_PAD), BF16)],
        compiler_params=_params(("arbitrary",)),
    )(proj, *tabs, gqa, gkva, gq, gk, wq, wk, wv)


def _flash_kernel(qi_ref, ki_ref, q_ref, k_ref, v_ref, o_ref, m_ref, acc_ref):
    p = pl.program_id(2)
    qi, ki = qi_ref[p], ki_ref[p]

    @pl.when(ki == 0)
    def _():
        m_ref[...] = jnp.full_like(m_ref, -jnp.inf)
        acc_ref[...] = jnp.zeros_like(acc_ref)

    def update(masked):
        blk = q_ref.shape[0]
        sub = min(blk, HEAD_PAD)
        for i in range(blk // sub):
            rows = slice(i * sub, (i + 1) * sub)
            nk = (i + 1) * sub if masked else blk
            s = _dot_nt(q_ref[rows, :], k_ref[:nk, :])
            if masked:
                row = lax.broadcasted_iota(jnp.int32, s.shape, 0) + i * sub
                col = lax.broadcasted_iota(jnp.int32, s.shape, 1)
                s = jnp.where(col <= row, s, -jnp.inf)
            m_prev = m_ref[rows, :]
            m_new = jnp.maximum(m_prev, jnp.max(s, axis=-1, keepdims=True))
            alpha = jnp.exp(m_prev - m_new)
            pe = jnp.exp((s - m_new).astype(BF16))
            acc_ref[rows, :] = alpha * acc_ref[rows, :] + _dot(pe, v_ref[:nk, :])
            m_ref[rows, :] = m_new

    @pl.when(ki < qi)
    def _():
        update(False)

    @pl.when(ki == qi)
    def _():
        update(True)
        acc = acc_ref[...]
        o_ref[...] = (acc[:, :MLA_V] / acc[:, MLA_V:MLA_V + 1]).astype(o_ref.dtype)


def _flash(q, k, v, batch, nh):
    t = q.shape[0]
    s = t // batch
    blk = _tile(s, 1024)
    nq = s // blk
    pairs = [(a, b) for a in range(nq) for b in range(a + 1)]
    qi = jnp.asarray([a for a, _ in pairs], jnp.int32)
    ki = jnp.asarray([b for _, b in pairs], jnp.int32)
    return pl.pallas_call(
        _flash_kernel,
        grid_spec=pltpu.PrefetchScalarGridSpec(
            num_scalar_prefetch=2,
            grid=(batch, nh, len(pairs)),
            in_specs=[pl.BlockSpec((blk, HEAD_PAD), lambda b, h, p, qi, ki: (b * nq + qi[p], h)),
                      pl.BlockSpec((blk, HEAD_PAD), lambda b, h, p, qi, ki: (b * nq + ki[p], h)),
                      pl.BlockSpec((blk, HEAD_PAD), lambda b, h, p, qi, ki: (b * nq + ki[p], h))],
            out_specs=pl.BlockSpec((blk, MLA_V), lambda b, h, p, qi, ki: (b * nq + qi[p], h)),
            scratch_shapes=[pltpu.VMEM((blk, 1), F32), pltpu.VMEM((blk, HEAD_PAD), F32)]),
        out_shape=jax.ShapeDtypeStruct((t, nh * MLA_V), BF16),
        compiler_params=_params(("arbitrary", "arbitrary", "arbitrary")),
    )(qi, ki, q, k, v)


def _wo_kernel(ret_ref, att_ref, w1_ref, w2_ref, x_ref, gate_ref, o_ref):
    mix = _dot(ret_ref[...], w1_ref[...]) + _dot(att_ref[...], w2_ref[...])
    o_ref[...] = x_ref[...] + gate_ref[0] * mix


def _wo(ret, att, w_o, x2, gate, batch):
    t, wr = ret.shape
    wa = att.shape[1]
    assert wr == wa
    d = w_o.shape[1]
    s = t // batch
    tm = _tile(s, 1024)
    tn = _tile(d, 512)
    ns = s // tm
    return pl.pallas_call(
        _wo_kernel,
        grid=(t // tm, d // tn),
        in_specs=[pl.BlockSpec((tm, wr), lambda i, j: (i, 0)),
                  pl.BlockSpec((tm, wa), lambda i, j: (i, 0)),
                  pl.BlockSpec((wr, tn), lambda i, j: (0, j)),
                  pl.BlockSpec((wa, tn), lambda i, j: (1, j)),
                  pl.BlockSpec((tm, tn), lambda i, j: (i, j)),
                  pl.BlockSpec((1, 1, tn), lambda i, j: (i // ns, 0, j))],
        out_specs=pl.BlockSpec((tm, tn), lambda i, j: (i, j)),
        out_shape=jax.ShapeDtypeStruct((t, d), F32),
        compiler_params=_params(("arbitrary", "arbitrary")),
    )(ret, att, w_o, w_o, x2, gate)


def _router_kernel(x_ref, g_ref, sc_ref, sh_ref, wrh_ref, wrl_ref, bias_ref, wsg_ref, wsu_ref,
                   h_ref, a_ref, e_ref, w_ref):
    h = _norm_mod(x_ref[...], g_ref[...], sc_ref[0], sh_ref[0])
    hb = h.astype(BF16)
    hbf = hb.astype(F32)
    _store_slabs(h_ref, _pack_pairs(hbf))
    a_ref[...] = (_silu(_dot(hb, wsg_ref[...])) * _dot(hb, wsu_ref[...])).astype(a_ref.dtype)

    hl = (h - hbf).astype(BF16)
    wrh = wrh_ref[...]
    logits = _dot_nt(wrh, hb) + _dot_nt(wrh, hl) + _dot_nt(wrl_ref[...], hb)
    scores = jax.nn.sigmoid(logits)
    sel = scores + bias_ref[...]
    ne, tm = sel.shape
    per_group = ne // N_GROUPS
    neg = -jnp.inf

    def first_argmax(v, iota, size):
        m = jnp.max(v, axis=0, keepdims=True)
        idx = jnp.min(jnp.where(v == m, iota, size), axis=0, keepdims=True)
        return m, idx

    iota_pg = lax.broadcasted_iota(jnp.int32, (per_group, tm), 0)
    gs = []
    for g in range(N_GROUPS):
        sg = sel[g * per_group:(g + 1) * per_group, :]
        m1, i1 = first_argmax(sg, iota_pg, per_group)
        m2 = jnp.max(jnp.where(iota_pg == i1, neg, sg), axis=0, keepdims=True)
        gs.append(m1 + m2)
    gsc = jnp.concatenate(gs, axis=0)
    iota_g = lax.broadcasted_iota(jnp.int32, (N_GROUPS, tm), 0)
    gmask = jnp.zeros((N_GROUPS, tm), jnp.bool_)
    for _ in range(TOPK_GROUPS):
        _, ig = first_argmax(gsc, iota_g, N_GROUPS)
        hit = iota_g == ig
        gmask = jnp.logical_or(gmask, hit)
        gsc = jnp.where(hit, neg, gsc)
    emask = jnp.concatenate(
        [jnp.broadcast_to(gmask[g:g + 1, :], (per_group, tm)) for g in range(N_GROUPS)], axis=0)
    cand = jnp.where(emask, sel, neg)
    iota_e = lax.broadcasted_iota(jnp.int32, (ne, tm), 0)
    es, ws = [], []
    for _ in range(TOP_K):
        _, ie = first_argmax(cand, iota_e, ne)
        hit = iota_e == ie
        es.append(ie)
        ws.append(jnp.sum(jnp.where(hit, scores, 0.0), axis=0, keepdims=True))
        cand = jnp.where(hit, neg, cand)
    wsum = ws[0]
    for wk in ws[1:]:
        wsum = wsum + wk
    e_ref[...] = jnp.concatenate(es, axis=0)
    w_ref[...] = jnp.concatenate(ws, axis=0) / wsum * ROUTED_SCALE


def _router(x2, g, scale, shift, wrh, wrl, bias, wsg, wsu, batch):
    t, d = x2.shape
    s = t // batch
    tm = _tile(s, 256)
    ns = s // tm
    ne = wrh.shape[0]
    f = wsg.shape[1]
    full = lambda a: pl.BlockSpec(a.shape, lambda i: (0,) * a.ndim)
    bat = pl.BlockSpec((1, 1, d), lambda i: (i // ns, 0, 0))
    return pl.pallas_call(
        _router_kernel,
        grid=(t // tm,),
        in_specs=[pl.BlockSpec((tm, d), lambda i: (i, 0)), full(g), bat, bat,
                  full(wrh), full(wrl), full(bias), full(wsg), full(wsu)],
        out_specs=[pl.BlockSpec((tm * (d // 2 // LANES), LANES), lambda i: (i, 0)),
                   pl.BlockSpec((tm, f), lambda i: (i, 0)),
                   pl.BlockSpec((TOP_K, tm), lambda i: (0, i)),
                   pl.BlockSpec((TOP_K, tm), lambda i: (0, i))],
        out_shape=[jax.ShapeDtypeStruct((t * (d // 2 // LANES), LANES), jnp.uint32),
                   jax.ShapeDtypeStruct((t, f), BF16),
                   jax.ShapeDtypeStruct((TOP_K, t), jnp.int32),
                   jax.ShapeDtypeStruct((TOP_K, t), F32)],
        compiler_params=_params(("arbitrary",)),
    )(x2, g, scale, shift, wrh, wrl, bias, wsg, wsu)


def _moe_kernel(tn_ref, ws_ref, nx_ref, ulo_ref, uhi_ref, tok_tab, dst_tab, h_hbm, wg_hbm, wu_hbm, wd_hbm, y_hbm,
                xbuf, ybuf, wgb, wub, wdb, stg_a, stg_d, tok_idx, dst_idx, gsem, ssem, isem, wsem):
    g = pl.program_id(0)
    n_grid = pl.num_programs(0)
    tm = tok_idx.shape[2]
    per = xbuf.shape[1] // tm
    dump_row = dst_tab.shape[0] - 1
    nxt = jnp.minimum(g + 1, n_grid - 1)
    active = tn_ref[g] > 0
    next_active = jnp.logical_and(g + 1 < n_grid, tn_ref[nxt] > 0)
    last_active = jnp.logical_and(active, jnp.logical_not(next_active))

    def issue_gather(ib, b):
        for r in range(tm):
            src = pl.multiple_of(tok_idx[ib, 0, r], per)
            pltpu.make_async_copy(h_hbm.at[pl.ds(src, per), :], xbuf.at[b, pl.ds(r * per, per), :],
                                  gsem.at[b]).start()

    def issue_scatter(ib, b):
        for r in range(tm):
            dst = pl.multiple_of(dst_idx[ib, 0, r], per)
            pltpu.make_async_copy(ybuf.at[b, pl.ds(r * per, per), :], y_hbm.at[pl.ds(dst, per), :],
                                  ssem.at[b]).start()

    def wait_gather(b):
        pltpu.make_async_copy(h_hbm.at[pl.ds(0, tm * per), :], xbuf.at[b], gsem.at[b]).wait()

    def wait_scatter(b):
        pltpu.make_async_copy(ybuf.at[b], y_hbm.at[pl.ds(0, tm * per), :], ssem.at[b]).wait()

    def tok_copy(row, b):
        return pltpu.make_async_copy(tok_tab.at[row], tok_idx.at[b], isem.at[b])

    def dst_copy(row, b):
        return pltpu.make_async_copy(dst_tab.at[row], dst_idx.at[b], isem.at[b])

    depth = stg_a.shape[0]
    ca, cd = stg_a.shape[1], stg_d.shape[1]
    n_gu = wg_hbm.shape[1] // ca
    n_units = 2 * n_gu + wd_hbm.shape[1] // cd

    def unit_parts(e, u, slot):
        sb = lax.rem(u, depth)
        rg = pl.multiple_of(u * ca, ca)
        ru = pl.multiple_of((u - n_gu) * ca, ca)
        rd = pl.multiple_of((u - 2 * n_gu) * cd, cd)
        return sb, (
            (u < n_gu, wg_hbm.at[e, pl.ds(rg, ca), :], stg_a, wgb.at[slot, pl.ds(rg, ca), :]),
            (jnp.logical_and(u >= n_gu, u < 2 * n_gu), wu_hbm.at[e, pl.ds(ru, ca), :], stg_a,
             wub.at[slot, pl.ds(ru, ca), :]),
            (u >= 2 * n_gu, wd_hbm.at[e, pl.ds(rd, cd), :], stg_d, wdb.at[slot, pl.ds(rd, cd), :]))

    def unit_dma(e, u, start):
        sb, parts = unit_parts(e, u, 0)
        for pred, src, stg, _ in parts:
            @pl.when(pred)
            def _(src=src, stg=stg):
                copy = pltpu.make_async_copy(src, stg.at[sb], wsem.at[sb])
                copy.start() if start else copy.wait()

    def unit_cast(e, u, slot):
        sb, parts = unit_parts(e, u, slot)
        for pred, _, stg, dst in parts:
            @pl.when(pred)
            def _(stg=stg, dst=dst):
                dst[...] = stg[sb].astype(dst.dtype)

    def start_units(e, lo, hi):
        for k in range(depth):
            @pl.when(lo + k < hi)
            def _(k=k):
                unit_dma(e, lo + k, True)

    def finish_units(e, slot, lo, hi):
        def body(u, carry):
            unit_dma(e, u, False)
            unit_cast(e, u, slot)

            @pl.when(u + depth < hi)
            def _():
                unit_dma(e, u + depth, True)

            return carry

        lax.fori_loop(lo, hi, body, 0)

    @pl.when(g == 0)
    def _():
        start_units(nx_ref[n_grid], 0, n_units)
        finish_units(nx_ref[n_grid], ws_ref[0], 0, n_units)
        ybuf[...] = jnp.zeros_like(ybuf)
        for b in range(2):
            dump = pltpu.make_async_copy(
                ybuf.at[b], y_hbm.at[pl.ds(y_hbm.shape[0] - (2 - b) * tm * per, tm * per), :], ssem.at[b])
            dump.start()
            dump.wait()
        first = tok_copy(0, 1)
        first.start()
        first.wait()
        issue_gather(1, 0)
        tok_copy(nxt, 0).start()
        dst_copy(dump_row, 0).start()

    def step(b):
        tok_copy(0, b).wait()
        dst_copy(0, b).wait()

        @pl.when(next_active)
        def _():
            tok_copy(jnp.minimum(g + 2, n_grid - 1), 1 - b).start()
            dst_copy(g, 1 - b).start()

        wait_gather(b)

        @pl.when(g >= 1)
        def _():
            wait_scatter(b)

        lo, hi = _unpack_pairs(_load_slabs(xbuf.at[b], tm))
        lo, hi = lo.astype(BF16), hi.astype(BF16)
        ws = ws_ref[g]
        wg_ref, wu_ref, wd_ref = wgb.at[ws], wub.at[ws], wdb.at[ws]
        half = wg_ref.shape[0] // 2
        issue_gather(b, 1 - b)
        gate = _dot(lo, wg_ref[:half, :]) + _dot(hi, wg_ref[half:, :])
        up = _dot(lo, wu_ref[:half, :]) + _dot(hi, wu_ref[half:, :])
        issue_scatter(b, 1 - b)
        y = _dot((_silu(gate) * up).astype(BF16), wd_ref[...])
        _store_slabs(ybuf.at[b], _pack_pairs(_round_bf16(y)))

        @pl.when(last_active)
        def _():
            own = dst_copy(g, 1 - b)
            own.start()
            own.wait()
            issue_scatter(1 - b, b)
            wait_scatter(b)
            wait_scatter(1 - b)
            wait_gather(1 - b)

    @pl.when(active)
    def _():
        start_units(nx_ref[g], ulo_ref[g], uhi_ref[g])

    for b in range(2):
        @pl.when(jnp.logical_and(active, lax.rem(g, 2) == b))
        def _(b=b):
            step(b)

    @pl.when(active)
    def _():
        finish_units(nx_ref[g], 1 - ws_ref[g], ulo_ref[g], uhi_ref[g])


def _moe(sched, tok_tab, dst_tab, h2s, wg, wu, wd, n_tok, tm):
    ne, d, f = wg.shape
    per = d // 2 // LANES
    n_tiles = sched[0].shape[0]
    n_rows = TOP_K * n_tok + 2 * tm
    ca, cd = _tile(d, MOE_GU_ROWS), _tile(f, MOE_DN_ROWS)
    buf = pltpu.VMEM((2, tm * per, LANES), jnp.uint32)
    idx = pltpu.SMEM((2, 1, tm), jnp.int32)
    sems = pltpu.SemaphoreType.DMA((2,))
    hbm = pl.BlockSpec(memory_space=pl.ANY)
    return pl.pallas_call(
        _moe_kernel,
        grid_spec=pltpu.PrefetchScalarGridSpec(
            num_scalar_prefetch=len(sched),
            grid=(n_tiles,),
            in_specs=[hbm] * 6,
            out_specs=hbm,
            scratch_shapes=[buf, buf,
                            pltpu.VMEM((2, d, f), BF16), pltpu.VMEM((2, d, f), BF16), pltpu.VMEM((2, f, d), BF16),
                            pltpu.VMEM((MOE_STAGE_DEPTH, ca, f), F32), pltpu.VMEM((MOE_STAGE_DEPTH, cd, d), F32),
                            idx, idx, sems, sems, sems, pltpu.SemaphoreType.DMA((MOE_STAGE_DEPTH,))]),
        out_shape=jax.ShapeDtypeStruct((n_rows * per, LANES), jnp.uint32),
        compiler_params=_params(("arbitrary",)),
    )(*sched, tok_tab, dst_tab, h2s, wg, wu, wd)


def _moe_tables(order, tile_row0, tile_nv, n_tok, tm, per):
    assert n_tok & (n_tok - 1) == 0, "token count must be a power of two (slot index packing)"
    p = order.shape[0]
    n_tiles = tile_row0.shape[0]
    r = jnp.arange(tm, dtype=jnp.int32)[None, :]
    slot = order[jnp.minimum(tile_row0[:, None] + r, p - 1)]
    parity = (jnp.arange(n_tiles, dtype=jnp.int32) % 2)[:, None]
    dump = p + parity * tm + r
    dst = jnp.where(r < tile_nv[:, None], slot, dump)
    dst = jnp.concatenate([dst, p + tm + r], axis=0)
    tok = slot & (n_tok - 1)
    return (tok * per)[:, None, :], (dst * per)[:, None, :]


def _moe_schedule(e_idx, ne, tm, n_units):
    k, t = e_idx.shape
    p = k * t
    keys = e_idx.reshape(p)
    _, order = lax.sort((keys, jnp.arange(p, dtype=jnp.int32)), num_keys=1, is_stable=True)
    experts = jnp.arange(ne, dtype=jnp.int32)
    counts = jnp.sum((keys[None, :] == experts[:, None]).astype(jnp.int32), axis=1)
    starts = jnp.cumsum(counts) - counts
    ntile = (counts + tm - 1) // tm
    cum = jnp.cumsum(ntile)
    n_tiles = p // tm + ne
    gidx = jnp.arange(n_tiles, dtype=jnp.int32)
    te = jnp.minimum(jnp.sum((gidx[:, None] >= cum[None, :]).astype(jnp.int32), axis=1), ne - 1)
    onehot = (te[:, None] == experts[None, :]).astype(jnp.int32)
    pick = lambda v: jnp.sum(onehot * v[None, :], axis=1)
    local = gidx - pick(cum - ntile)
    nv = jnp.where(gidx < cum[-1], jnp.clip(pick(counts) - local * tm, 0, tm), 0).astype(jnp.int32)
    row0 = jnp.where(nv > 0, pick(starts) + local * tm, 0).astype(jnp.int32)
    nonempty = ntile > 0
    slot_e = (jnp.cumsum(nonempty.astype(jnp.int32)) - 1) % 2
    masked_id = jnp.where(nonempty, experts, ne)
    next_e = jnp.concatenate([lax.cummin(masked_id[::-1])[::-1][1:], jnp.full((1,), ne, jnp.int32)])
    nt, nx = pick(ntile), pick(next_e)
    has_next = jnp.logical_and(nv > 0, nx < ne)
    ulo = jnp.where(has_next, (n_units * local) // jnp.maximum(nt, 1), 0).astype(jnp.int32)
    uhi = jnp.where(has_next, (n_units * (local + 1)) // jnp.maximum(nt, 1), 0).astype(jnp.int32)
    nx = jnp.concatenate([jnp.minimum(nx, ne - 1), te[:1]]).astype(jnp.int32)
    return order, row0, nv, (nv, pick(slot_e).astype(jnp.int32), nx, ulo, uhi)


def _combine_kernel(*refs):
    y_refs = refs[:TOP_K]
    w_ref, a_ref, wsd_ref, x_ref, gate_ref, o_ref = refs[TOP_K:]
    tm = x_ref.shape[0]
    half = x_ref.shape[1] // 2
    w = w_ref[...]
    lo, hi = _unpack_pairs(_load_slabs(y_refs[0], tm))
    acc_lo, acc_hi = w[:, 0:1] * lo, w[:, 0:1] * hi
    for k in range(1, TOP_K):
        lo, hi = _unpack_pairs(_load_slabs(y_refs[k], tm))
        acc_lo = acc_lo + w[:, k:k + 1] * lo
        acc_hi = acc_hi + w[:, k:k + 1] * hi
    a = a_ref[...]
    gate = gate_ref[0]
    o_ref[:, :half] = x_ref[:, :half] + gate[:, :half] * (acc_lo + _dot(a, wsd_ref[:, :half]))
    o_ref[:, half:] = x_ref[:, half:] + gate[:, half:] * (acc_hi + _dot(a, wsd_ref[:, half:]))


def _combine(y, w_tk, a_sh, wsd, x2, gate, batch):
    t, d = x2.shape
    per = d // 2 // LANES
    kk = w_tk.shape[1]
    assert kk == TOP_K
    f = a_sh.shape[1]
    s = t // batch
    tm = _tile(s, 128)
    ns = s // tm
    nt = t // tm
    return pl.pallas_call(
        _combine_kernel,
        grid=(nt,),
        in_specs=[pl.BlockSpec((tm * per, LANES), lambda i, k=k: (k * nt + i, 0)) for k in range(TOP_K)] + [
                  pl.BlockSpec((tm, kk), lambda i: (i, 0)),
                  pl.BlockSpec((tm, f), lambda i: (i, 0)),
                  pl.BlockSpec((f, d), lambda i: (0, 0)),
                  pl.BlockSpec((tm, d), lambda i: (i, 0)),
                  pl.BlockSpec((1, 1, d), lambda i: (i // ns, 0, 0))],
        out_specs=pl.BlockSpec((tm, d), lambda i: (i, 0)),
        out_shape=jax.ShapeDtypeStruct((t, d), F32),
        compiler_params=_params(("arbitrary",)),
    )(*([y] * TOP_K), w_tk, a_sh, wsd, x2, gate)


def _round_up(v, m):
    return (v + m - 1) // m * m


def kernel(x, c, positions, w_ada, b_ada, norm1_g, w_in, q_a_norm_g, w_uq, kv_a_norm_g, w_ukv, q_norm_g, k_norm_g, w_o, norm2_g, w_router, router_bias, w_exp_gate, w_exp_up, w_exp_down, w_sh_gate, w_sh_up, w_sh_down):
    b, s, d = x.shape
    t = b * s
    depth = w_ada.shape[0]
    qr = q_a_norm_g.shape[1]
    kvr = kv_a_norm_g.shape[1]
    nh = w_uq.shape[2] // MLA_QK
    in_w = w_in.shape[2]
    ret_heads = (in_w - qr - kvr - MLA_ROPE) // (2 * RET_DK + 2 * RET_DV)
    ret_w = ret_heads * RET_DK
    ne = w_router.shape[2]
    moe_tm = 256

    pos = positions.reshape(t).astype(F32)
    inv_r = ROPE_BASE ** (-jnp.arange(0, RET_DK, 2, dtype=F32) / RET_DK)
    ang_r = pos[:, None] * inv_r
    cos_r, sin_r = jnp.cos(ang_r), jnp.sin(ang_r)
    inv_m = ROPE_BASE ** (-jnp.arange(0, MLA_ROPE, 2, dtype=F32) / MLA_ROPE)
    ang_m = pos[:, None] * inv_m
    cm, sm = jnp.cos(ang_m), jnp.sin(ang_m)
    z = jnp.zeros_like(cm)
    mla_tabs = (jnp.concatenate([cm, cm, z, z], axis=-1),
                jnp.concatenate([-sm, z, z, z], axis=-1),
                jnp.concatenate([z, sm, z, z], axis=-1))
    log_g = jnp.log(1.0 - 2.0 ** (-5.0 - jnp.arange(ret_heads, dtype=F32)))

    c_pad = jnp.zeros((8, d), F32).at[:b].set(c)
    x2 = x.reshape(t, d)
    wm = _round_up(qr + kvr + LANES, HEAD_PAD)
    ret_cols = 4 * ret_w

    for l in range(depth):
        mod = _ada(c_pad, w_ada[l], b_ada[l][None, :])[:b]
        shift_a, scale_a, gate_a, shift_m, scale_m, gate_m = (
            m.reshape(b, 1, d) for m in jnp.split(mod, 6, axis=-1))

        h1 = _norm_mod_call(x2, norm1_g[l][None, :], scale_a, shift_a, b, BF16)
        w_in_t = jnp.swapaxes(w_in, 1, 2)
        proj_ret = _mm_wcast(h1, w_in_t, l, 0, ret_cols, BF16)
        w_mla_t = jnp.pad(w_in_t[l, ret_cols:, :], ((0, wm - (in_w - ret_cols)), (0, 0))).astype(BF16)
        proj = _mm(h1, w_mla_t, BF16)

        ret = _retention(proj_ret, cos_r, sin_r, log_g, b, ret_heads, 0)

        wq = jnp.pad(w_uq[l].reshape(qr, nh, MLA_QK), ((0, 0), (0, 0), (0, HEAD_PAD - MLA_QK)))
        wq = wq.reshape(qr, nh * HEAD_PAD).astype(BF16)
        wkv = w_ukv[l].reshape(kvr, nh, MLA_NOPE + MLA_V)
        wk = wkv[:, :, :MLA_NOPE].reshape(kvr, nh * MLA_NOPE).astype(BF16)
        wv = wkv[:, :, MLA_NOPE:].reshape(kvr, nh * MLA_V).astype(BF16)
        gq = jnp.pad(q_norm_g[l], (0, HEAD_PAD - MLA_QK))[None, :]
        gk = jnp.pad(k_norm_g[l], (0, HEAD_PAD - MLA_QK))[None, :]
        q, k, v = _mla_prep(proj, wm, mla_tabs, q_a_norm_g[l][None, :], kv_a_norm_g[l][None, :],
                            gq, gk, wq, wk, wv, qr, kvr, nh)
        att = _flash(q, k, v, b, nh)

        x2 = _wo(ret, att, w_o[l].astype(BF16), x2, gate_a, b)

        wr_t = w_router[l].T
        wrh = wr_t.astype(BF16)
        wrl = (wr_t - wrh.astype(F32)).astype(BF16)
        h2p, a_sh, e_idx, w_kt = _router(x2, norm2_g[l][None, :], scale_m, shift_m, wrh, wrl,
                                         router_bias[l][:, None], w_sh_gate[l].astype(BF16),
                                         w_sh_up[l].astype(BF16), b)
        ff = w_exp_gate.shape[3]
        n_units = 2 * (d // _tile(d, MOE_GU_ROWS)) + ff // _tile(ff, MOE_DN_ROWS)
        order, tile_row0, tile_nv, sched = _moe_schedule(e_idx, ne, moe_tm, n_units)
        tok_tab, dst_tab = _moe_tables(order, tile_row0, tile_nv, t, moe_tm, d // 2 // LANES)
        y = _moe(sched, tok_tab, dst_tab, h2p, w_exp_gate[l], w_exp_up[l], w_exp_down[l], t, moe_tm)
        x2 = _combine(y, w_kt.T, a_sh, w_sh_down[l].astype(BF16), x2, gate_m, b)

    return x2.reshape(b, s, d)
```

```python
import functools
import math

import jax
import jax.numpy as jnp
from jax import lax
from jax.experimental import pallas as pl
from jax.experimental.pallas import tpu as pltpu

F32 = jnp.float32
BF16 = jnp.bfloat16

EPS = 1e-6
ROPE_BASE = 10000.0
RET_DK = 256
RET_DV = 256
MLA_NOPE = 128
MLA_ROPE = 64
MLA_V = 128
MLA_QK = MLA_NOPE + MLA_ROPE
N_GROUPS = 8
TOPK_GROUPS = 4
TOP_K = 8
ROUTED_SCALE = 2.5

LANES = 128
SUBLANES = 8
HEAD_PAD = 2 * LANES
VMEM_LIMIT = 56 * 1024 * 1024
MOE_CHUNKS = 8
MOE_STAGE_DEPTH = 4


def _tile(dim, pref):
    t = min(dim, pref)
    while dim % t:
        t //= 2
    return t


def _params(sem):
    return pltpu.CompilerParams(dimension_semantics=sem, vmem_limit_bytes=VMEM_LIMIT)


def _silu(v):
    return v * jax.nn.sigmoid(v)


def _dot(a, b):
    return jnp.dot(a, b, preferred_element_type=F32)


def _dot_nt(a, b):
    return lax.dot_general(a, b, (((1,), (1,)), ((), ())), preferred_element_type=F32)


def _dot_tn(a, b):
    return lax.dot_general(a, b, (((0,), (0,)), ((), ())), preferred_element_type=F32)


HI_HALF = 0xFFFF0000


def _pack_pairs(v):
    half = v.shape[1] // 2
    bits = lax.bitcast_convert_type(v, jnp.uint32)
    return (bits[:, :half] >> 16) | (bits[:, half:] & jnp.uint32(HI_HALF))


def _unpack_pairs(w):
    lo = lax.bitcast_convert_type(w << 16, F32)
    hi = lax.bitcast_convert_type(w & jnp.uint32(HI_HALF), F32)
    return lo, hi


def _round_bf16(v):
    return v.astype(BF16).astype(F32)


def _store_slabs(ref, v):
    rows, n = v.shape
    per = n // LANES
    for s in range(per):
        ref[pl.ds(s, rows, stride=per), :] = v[:, s * LANES:(s + 1) * LANES]


def _load_slabs(ref, rows):
    per = ref.shape[0] // rows
    return jnp.concatenate([ref[pl.ds(s, rows, stride=per), :] for s in range(per)], axis=1)


def _ada_kernel(c_ref, w_ref, b_ref, o_ref):
    ca = _silu(c_ref[...]).astype(BF16)
    o_ref[...] = _dot(ca, w_ref[...].astype(BF16)) + b_ref[...]


def _ada(c_pad, w, b):
    m, d = c_pad.shape
    n = w.shape[1]
    tn = _tile(n, 512)
    return pl.pallas_call(
        _ada_kernel,
        grid=(n // tn,),
        in_specs=[pl.BlockSpec((m, d), lambda j: (0, 0)),
                  pl.BlockSpec((d, tn), lambda j: (0, j)),
                  pl.BlockSpec((1, tn), lambda j: (0, j))],
        out_specs=pl.BlockSpec((m, tn), lambda j: (0, j)),
        out_shape=jax.ShapeDtypeStruct((m, n), F32),
        compiler_params=_params(("arbitrary",)),
    )(c_pad, w, b)


def _norm_mod(x, g, scale, shift):
    xf = x.astype(F32)
    h = xf * lax.rsqrt(jnp.mean(xf * xf, axis=-1, keepdims=True) + EPS) * g
    return h * (1.0 + scale) + shift


def _norm_mod_kernel(x_ref, g_ref, sc_ref, sh_ref, o_ref):
    o_ref[...] = _norm_mod(x_ref[...], g_ref[...], sc_ref[0], sh_ref[0]).astype(o_ref.dtype)


def _norm_mod_call(x2, g, scale, shift, batch, out_dtype):
    t, d = x2.shape
    s = t // batch
    ts = _tile(s, 512)
    ns = s // ts
    return pl.pallas_call(
        _norm_mod_kernel,
        grid=(batch, ns),
        in_specs=[pl.BlockSpec((ts, d), lambda b, i: (b * ns + i, 0)),
                  pl.BlockSpec((1, d), lambda b, i: (0, 0)),
                  pl.BlockSpec((1, 1, d), lambda b, i: (b, 0, 0)),
                  pl.BlockSpec((1, 1, d), lambda b, i: (b, 0, 0))],
        out_specs=pl.BlockSpec((ts, d), lambda b, i: (b * ns + i, 0)),
        out_shape=jax.ShapeDtypeStruct((t, d), out_dtype),
        compiler_params=_params(("arbitrary", "arbitrary")),
    )(x2, g, scale, shift)


def _mm_kernel(a_ref, w_ref, o_ref):
    o_ref[...] = _dot_nt(a_ref[...], w_ref[...]).astype(o_ref.dtype)


def _mm(a, wt, out_dtype, tm_pref=1024, tn_pref=512):
    m, k = a.shape
    n = wt.shape[0]
    tm, tn = _tile(m, tm_pref), _tile(n, tn_pref)
    return pl.pallas_call(
        _mm_kernel,
        grid=(m // tm, n // tn),
        in_specs=[pl.BlockSpec((tm, k), lambda i, j: (i, 0)),
                  pl.BlockSpec((tn, k), lambda i, j: (j, 0))],
        out_specs=pl.BlockSpec((tm, tn), lambda i, j: (i, j)),
        out_shape=jax.ShapeDtypeStruct((m, n), out_dtype),
        compiler_params=_params(("arbitrary", "arbitrary")),
    )(a, wt)


def _mm_wcast_kernel(a_ref, w_ref, o_ref, wb_ref):
    @pl.when(pl.program_id(1) == 0)
    def _():
        wb_ref[...] = w_ref[...].astype(wb_ref.dtype)

    o_ref[...] = _dot_nt(a_ref[...], wb_ref[...]).astype(o_ref.dtype)


def _mm_wcast(a, wt3, layer, col0, n, out_dtype, tm_pref=1024, tn_pref=512):
    m, k = a.shape
    tm, tn = _tile(m, tm_pref), _tile(n, tn_pref)
    assert col0 % tn == 0
    j0 = col0 // tn
    return pl.pallas_call(
        _mm_wcast_kernel,
        grid=(n // tn, m // tm),
        in_specs=[pl.BlockSpec((tm, k), lambda j, i: (i, 0)),
                  pl.BlockSpec((None, tn, k), lambda j, i: (layer, j0 + j, 0))],
        out_specs=pl.BlockSpec((tm, tn), lambda j, i: (i, j)),
        out_shape=jax.ShapeDtypeStruct((m, n), out_dtype),
        scratch_shapes=[pltpu.VMEM((tn, k), BF16)],
        compiler_params=_params(("arbitrary", "arbitrary")),
    )(a, wt3)


def _ret_kernel(lg_ref, q_ref, k_ref, v_ref, g_ref, cos_ref, sin_ref, o_ref, state_ref):
    h = pl.program_id(1)

    @pl.when(pl.program_id(2) == 0)
    def _():
        state_ref[...] = jnp.zeros_like(state_ref)

    lg = lg_ref[h]
    c = q_ref.shape[0]
    cos, sin = cos_ref[...], sin_ref[...]
    half = RET_DK // 2

    def rope(v):
        v1, v2 = v[:, :half], v[:, half:]
        return jnp.concatenate([v1 * cos - v2 * sin, v2 * cos + v1 * sin], axis=-1)

    q = rope(q_ref[...].astype(F32))
    k = rope(k_ref[...].astype(F32)) * (RET_DK ** -0.5)
    v = v_ref[...]
    ii = lax.broadcasted_iota(jnp.int32, (c, c), 0)
    jj = lax.broadcasted_iota(jnp.int32, (c, c), 1)
    diff = (ii - jj).astype(F32)
    inner_decay = jnp.where(diff >= 0, jnp.exp(diff * lg), 0.0)
    ri = lax.broadcasted_iota(jnp.int32, (c, 1), 0).astype(F32)
    q_decay = jnp.exp((ri + 1.0) * lg)
    k_decay = jnp.exp((c - 1.0 - ri) * lg)
    chunk_decay = jnp.exp(jnp.full((1, 1), c, F32) * lg)

    qb = q.astype(BF16)
    sc = _dot_nt(qb, k.astype(BF16)) * inner_decay
    inner = _dot(sc.astype(BF16), v)
    state = state_ref[...]
    cross = _dot(qb, state.astype(BF16)) * q_decay
    kv = _dot_tn((k * k_decay).astype(BF16), v)
    state_ref[...] = state * chunk_decay + kv
    out = inner + cross
    r = out * lax.rsqrt(jnp.mean(out * out, axis=-1, keepdims=True) + EPS)
    o_ref[...] = (r * _silu(g_ref[...].astype(F32))).astype(o_ref.dtype)


def _retention(proj, cos_r, sin_r, log_g, batch, n_heads, col0):
    t = proj.shape[0]
    s = t // batch
    c = _tile(s, 512)
    n = s // c
    hh = n_heads

    def blk(off):
        return pl.BlockSpec((c, RET_DK), lambda b, h, i, lg: (b * n + i, col0 + off + h))

    tab = pl.BlockSpec((c, RET_DK // 2), lambda b, h, i, lg: (b * n + i, 0))
    return pl.pallas_call(
        _ret_kernel,
        grid_spec=pltpu.PrefetchScalarGridSpec(
            num_scalar_prefetch=1,
            grid=(batch, hh, n),
            in_specs=[blk(0), blk(hh), blk(2 * hh), blk(3 * hh), tab, tab],
            out_specs=pl.BlockSpec((c, RET_DV), lambda b, h, i, lg: (b * n + i, h)),
            scratch_shapes=[pltpu.VMEM((RET_DK, RET_DV), F32)]),
        out_shape=jax.ShapeDtypeStruct((t, hh * RET_DV), BF16),
        compiler_params=_params(("arbitrary", "arbitrary", "arbitrary")),
    )(log_g, proj, proj, proj, proj, cos_r, sin_r)


def _mla_prep_kernel(p_ref, ct_ref, sa_ref, sb_ref, gqa_ref, gkva_ref, gq_ref, gk_ref,
                     wq_ref, wk_ref, wv_ref, q_ref, k_ref, v_ref, *, qr, kvr, nh):
    p = p_ref[...].astype(F32)
    cq = p[:, :qr]
    ckv = p[:, qr:qr + kvr]
    krp = p[:, qr + kvr:qr + kvr + LANES]

    def rms(v, n):
        return v * lax.rsqrt(jnp.sum(v * v, axis=-1, keepdims=True) * (1.0 / n) + EPS)

    cqn = (rms(cq, qr) * gqa_ref[...]).astype(BF16)
    ckvn = (rms(ckv, kvr) * gkva_ref[...]).astype(BF16)
    qf = _dot(cqn, wq_ref[...])
    kn = _dot(ckvn, wk_ref[...])
    vf = _dot(ckvn, wv_ref[...])
    ones_col = (lax.broadcasted_iota(jnp.int32, (p.shape[0], LANES), 1) == 0).astype(v_ref.dtype)
    for h in range(nh):
        v_ref[:, h * HEAD_PAD:h * HEAD_PAD + MLA_V] = vf[:, h * MLA_V:(h + 1) * MLA_V].astype(v_ref.dtype)
        v_ref[:, h * HEAD_PAD + MLA_V:(h + 1) * HEAD_PAD] = ones_col

    ct, sa, sb = ct_ref[...], sa_ref[...], sb_ref[...]

    def rope(v):
        return v * ct + pltpu.roll(v, LANES - MLA_ROPE // 2, 1) * sa + pltpu.roll(v, MLA_ROPE // 2, 1) * sb

    gq, gk = gq_ref[...], gk_ref[...]
    kr_rot = rope(krp * gk[:, LANES:])
    kr_ss = jnp.sum(krp * krp, axis=-1, keepdims=True)
    qscale = 1.0 / math.sqrt(MLA_QK)
    for h in range(nh):
        qh = qf[:, h * HEAD_PAD:(h + 1) * HEAD_PAD]
        rq = lax.rsqrt(jnp.sum(qh * qh, axis=-1, keepdims=True) * (1.0 / MLA_QK) + EPS) * qscale
        qn = qh * rq * gq
        q_ref[:, h * HEAD_PAD:h * HEAD_PAD + LANES] = qn[:, :LANES].astype(q_ref.dtype)
        q_ref[:, h * HEAD_PAD + LANES:(h + 1) * HEAD_PAD] = rope(qn[:, LANES:]).astype(q_ref.dtype)
        knh = kn[:, h * MLA_NOPE:(h + 1) * MLA_NOPE]
        rk = lax.rsqrt((jnp.sum(knh * knh, axis=-1, keepdims=True) + kr_ss) * (1.0 / MLA_QK) + EPS)
        k_ref[:, h * HEAD_PAD:h * HEAD_PAD + LANES] = (knh * rk * gk[:, :LANES]).astype(k_ref.dtype)
        k_ref[:, h * HEAD_PAD + LANES:(h + 1) * HEAD_PAD] = (kr_rot * rk).astype(k_ref.dtype)


def _mla_prep(proj, wm, tabs, gqa, gkva, gq, gk, wq, wk, wv, qr, kvr, nh):
    t = proj.shape[0]
    tm = _tile(t, 256)
    full = lambda a: pl.BlockSpec(a.shape, lambda i: (0, 0))
    tab = pl.BlockSpec((tm, LANES), lambda i: (i, 0))
    return pl.pallas_call(
        functools.partial(_mla_prep_kernel, qr=qr, kvr=kvr, nh=nh),
        grid=(t // tm,),
        in_specs=[pl.BlockSpec((tm, wm), lambda i: (i, 0)), tab, tab, tab,
                  full(gqa), full(gkva), full(gq), full(gk), full(wq), full(wk), full(wv)],
        out_specs=[pl.BlockSpec((tm, nh * HEAD_PAD), lambda i: (i, 0)),
                   pl.BlockSpec((tm, nh * HEAD_PAD), lambda i: (i, 0)),
                   pl.BlockSpec((tm, nh * HEAD_PAD), lambda i: (i, 0))],
        out_shape=[jax.ShapeDtypeStruct((t, nh * HEAD_PAD), BF16),
                   jax.ShapeDtypeStruct((t, nh * HEAD_PAD), BF16),
                   jax.ShapeDtypeStruct((t, nh * HEAD_PAD), BF16)],
        compiler_params=_params(("arbitrary",)),
    )(proj, *tabs, gqa, gkva, gq, gk, wq, wk, wv)


def _flash_kernel(qi_ref, ki_ref, q_ref, k_ref, v_ref, o_ref, m_ref, acc_ref):
    p = pl.program_id(2)
    qi, ki = qi_ref[p], ki_ref[p]

    @pl.when(ki == 0)
    def _():
        m_ref[...] = jnp.full_like(m_ref, -jnp.inf)
        acc_ref[...] = jnp.zeros_like(acc_ref)

    def update(masked):
        blk = q_ref.shape[0]
        sub = min(blk, HEAD_PAD)
        for i in range(blk // sub):
            rows = slice(i * sub, (i + 1) * sub)
            nk = (i + 1) * sub if masked else blk
            s = _dot_nt(q_ref[rows, :], k_ref[:nk, :])
            if masked:
                row = lax.broadcasted_iota(jnp.int32, s.shape, 0) + i * sub
                col = lax.broadcasted_iota(jnp.int32, s.shape, 1)
                s = jnp.where(col <= row, s, -jnp.inf)
            m_prev = m_ref[rows, :]
            m_new = jnp.maximum(m_prev, jnp.max(s, axis=-1, keepdims=True))
            alpha = jnp.exp(m_prev - m_new)
            pe = jnp.exp((s - m_new).astype(BF16))
            acc_ref[rows, :] = alpha * acc_ref[rows, :] + _dot(pe, v_ref[:nk, :])
            m_ref[rows, :] = m_new

    @pl.when(ki < qi)
    def _():
        update(False)

    @pl.when(ki == qi)
    def _():
        update(True)
        acc = acc_ref[...]
        o_ref[...] = (acc[:, :MLA_V] / acc[:, MLA_V:MLA_V + 1]).astype(o_ref.dtype)


def _flash(q, k, v, batch, nh):
    t = q.shape[0]
    s = t // batch
    blk = _tile(s, 1024)
    nq = s // blk
    pairs = [(a, b) for a in range(nq) for b in range(a + 1)]
    qi = jnp.asarray([a for a, _ in pairs], jnp.int32)
    ki = jnp.asarray([b for _, b in pairs], jnp.int32)
    return pl.pallas_call(
        _flash_kernel,
        grid_spec=pltpu.PrefetchScalarGridSpec(
            num_scalar_prefetch=2,
            grid=(batch, nh, len(pairs)),
            in_specs=[pl.BlockSpec((blk, HEAD_PAD), lambda b, h, p, qi, ki: (b * nq + qi[p], h)),
                      pl.BlockSpec((blk, HEAD_PAD), lambda b, h, p, qi, ki: (b * nq + ki[p], h)),
                      pl.BlockSpec((blk, HEAD_PAD), lambda b, h, p, qi, ki: (b * nq + ki[p], h))],
            out_specs=pl.BlockSpec((blk, MLA_V), lambda b, h, p, qi, ki: (b * nq + qi[p], h)),
            scratch_shapes=[pltpu.VMEM((blk, 1), F32), pltpu.VMEM((blk, HEAD_PAD), F32)]),
        out_shape=jax.ShapeDtypeStruct((t, nh * MLA_V), BF16),
        compiler_params=_params(("arbitrary", "arbitrary", "arbitrary")),
    )(qi, ki, q, k, v)


def _wo_kernel(ret_ref, att_ref, w1_ref, w2_ref, x_ref, gate_ref, o_ref):
    mix = _dot(ret_ref[...], w1_ref[...]) + _dot(att_ref[...], w2_ref[...])
    o_ref[...] = x_ref[...] + gate_ref[0] * mix


def _wo(ret, att, w_o, x2, gate, batch):
    t, wr = ret.shape
    wa = att.shape[1]
    assert wr == wa
    d = w_o.shape[1]
    s = t // batch
    tm = _tile(s, 1024)
    tn = _tile(d, 512)
    ns = s // tm
    return pl.pallas_call(
        _wo_kernel,
        grid=(t // tm, d // tn),
        in_specs=[pl.BlockSpec((tm, wr), lambda i, j: (i, 0)),
                  pl.BlockSpec((tm, wa), lambda i, j: (i, 0)),
                  pl.BlockSpec((wr, tn), lambda i, j: (0, j)),
                  pl.BlockSpec((wa, tn), lambda i, j: (1, j)),
                  pl.BlockSpec((tm, tn), lambda i, j: (i, j)),
                  pl.BlockSpec((1, 1, tn), lambda i, j: (i // ns, 0, j))],
        out_specs=pl.BlockSpec((tm, tn), lambda i, j: (i, j)),
        out_shape=jax.ShapeDtypeStruct((t, d), F32),
        compiler_params=_params(("arbitrary", "arbitrary")),
    )(ret, att, w_o, w_o, x2, gate)


def _router_kernel(x_ref, g_ref, sc_ref, sh_ref, wrh_ref, wrl_ref, bias_ref, wsg_ref, wsu_ref,
                   h_ref, a_ref, e_ref, w_ref):
    h = _norm_mod(x_ref[...], g_ref[...], sc_ref[0], sh_ref[0])
    hb = h.astype(BF16)
    hbf = hb.astype(F32)
    _store_slabs(h_ref, _pack_pairs(hbf))
    a_ref[...] = (_silu(_dot(hb, wsg_ref[...])) * _dot(hb, wsu_ref[...])).astype(a_ref.dtype)

    hl = (h - hbf).astype(BF16)
    wrh = wrh_ref[...]
    logits = _dot_nt(wrh, hb) + _dot_nt(wrh, hl) + _dot_nt(wrl_ref[...], hb)
    scores = jax.nn.sigmoid(logits)
    sel = scores + bias_ref[...]
    ne, tm = sel.shape
    per_group = ne // N_GROUPS
    neg = -jnp.inf

    def first_argmax(v, iota, size):
        m = jnp.max(v, axis=0, keepdims=True)
        idx = jnp.min(jnp.where(v == m, iota, size), axis=0, keepdims=True)
        return m, idx

    iota_pg = lax.broadcasted_iota(jnp.int32, (per_group, tm), 0)
    gs = []
    for g in range(N_GROUPS):
        sg = sel[g * per_group:(g + 1) * per_group, :]
        m1, i1 = first_argmax(sg, iota_pg, per_group)
        m2 = jnp.max(jnp.where(iota_pg == i1, neg, sg), axis=0, keepdims=True)
        gs.append(m1 + m2)
    gsc = jnp.concatenate(gs, axis=0)
    iota_g = lax.broadcasted_iota(jnp.int32, (N_GROUPS, tm), 0)
    gmask = jnp.zeros((N_GROUPS, tm), jnp.bool_)
    for _ in range(TOPK_GROUPS):
        _, ig = first_argmax(gsc, iota_g, N_GROUPS)
        hit = iota_g == ig
        gmask = jnp.logical_or(gmask, hit)
        gsc = jnp.where(hit, neg, gsc)
    emask = jnp.concatenate(
        [jnp.broadcast_to(gmask[g:g + 1, :], (per_group, tm)) for g in range(N_GROUPS)], axis=0)
    cand = jnp.where(emask, sel, neg)
    iota_e = lax.broadcasted_iota(jnp.int32, (ne, tm), 0)
    es, ws = [], []
    for _ in range(TOP_K):
        _, ie = first_argmax(cand, iota_e, ne)
        hit = iota_e == ie
        es.append(ie)
        ws.append(jnp.sum(jnp.where(hit, scores, 0.0), axis=0, keepdims=True))
        cand = jnp.where(hit, neg, cand)
    wsum = ws[0]
    for wk in ws[1:]:
        wsum = wsum + wk
    e_ref[...] = jnp.concatenate(es, axis=0)
    w_ref[...] = jnp.concatenate(ws, axis=0) / wsum * ROUTED_SCALE


def _router(x2, g, scale, shift, wrh, wrl, bias, wsg, wsu, batch):
    t, d = x2.shape
    s = t // batch
    tm = _tile(s, 256)
    ns = s // tm
    ne = wrh.shape[0]
    f = wsg.shape[1]
    full = lambda a: pl.BlockSpec(a.shape, lambda i: (0,) * a.ndim)
    bat = pl.BlockSpec((1, 1, d), lambda i: (i // ns, 0, 0))
    return pl.pallas_call(
        _router_kernel,
        grid=(t // tm,),
        in_specs=[pl.BlockSpec((tm, d), lambda i: (i, 0)), full(g), bat, bat,
                  full(wrh), full(wrl), full(bias), full(wsg), full(wsu)],
        out_specs=[pl.BlockSpec((tm * (d // 2 // LANES), LANES), lambda i: (i, 0)),
                   pl.BlockSpec((tm, f), lambda i: (i, 0)),
                   pl.BlockSpec((TOP_K, tm), lambda i: (0, i)),
                   pl.BlockSpec((TOP_K, tm), lambda i: (0, i))],
        out_shape=[jax.ShapeDtypeStruct((t * (d // 2 // LANES), LANES), jnp.uint32),
                   jax.ShapeDtypeStruct((t, f), BF16),
                   jax.ShapeDtypeStruct((TOP_K, t), jnp.int32),
                   jax.ShapeDtypeStruct((TOP_K, t), F32)],
        compiler_params=_params(("arbitrary",)),
    )(x2, g, scale, shift, wrh, wrl, bias, wsg, wsu)


def _moe_kernel(tn_ref, ws_ref, nx_ref, cst_ref, ulo_ref, uhi_ref, tok_tab, dst_tab, h_hbm, wg_hbm, wu_hbm, wd_hbm, y_hbm,
                xbuf, ybuf, wgb, wub, wdb, stg_a, stg_d, tok_idx, dst_idx, gsem, ssem, isem, wsem):
    g = pl.program_id(0)
    n_grid = pl.num_programs(0)
    tm = tok_idx.shape[2]
    per = xbuf.shape[1] // tm
    dump_row = dst_tab.shape[0] - 1
    nxt = jnp.minimum(g + 1, n_grid - 1)
    active = tn_ref[g] > 0
    next_active = jnp.logical_and(g + 1 < n_grid, tn_ref[nxt] > 0)
    last_active = jnp.logical_and(active, jnp.logical_not(next_active))

    def issue_gather(ib, b):
        for r in range(tm):
            src = pl.multiple_of(tok_idx[ib, 0, r], per)
            pltpu.make_async_copy(h_hbm.at[pl.ds(src, per), :], xbuf.at[b, pl.ds(r * per, per), :],
                                  gsem.at[b]).start()

    def issue_scatter(ib, b):
        for r in range(tm):
            dst = pl.multiple_of(dst_idx[ib, 0, r], per)
            pltpu.make_async_copy(ybuf.at[b, pl.ds(r * per, per), :], y_hbm.at[pl.ds(dst, per), :],
                                  ssem.at[b]).start()

    def wait_gather(b):
        pltpu.make_async_copy(h_hbm.at[pl.ds(0, tm * per), :], xbuf.at[b], gsem.at[b]).wait()

    def wait_scatter(b):
        pltpu.make_async_copy(ybuf.at[b], y_hbm.at[pl.ds(0, tm * per), :], ssem.at[b]).wait()

    def tok_copy(row, b):
        return pltpu.make_async_copy(tok_tab.at[row], tok_idx.at[b], isem.at[b])

    def dst_copy(row, b):
        return pltpu.make_async_copy(dst_tab.at[row], dst_idx.at[b], isem.at[b])

    depth = stg_a.shape[0]
    ca, cd = stg_a.shape[1], stg_d.shape[1]
    n_units = 3 * MOE_CHUNKS

    def chunk_parts(e, c, slot):
        ra = pl.multiple_of(c * ca, ca)
        rd = pl.multiple_of(c * cd, cd)
        return ((wg_hbm.at[e, pl.ds(ra, ca), :], stg_a, wgb.at[slot, pl.ds(ra, ca), :]),
                (wu_hbm.at[e, pl.ds(ra, ca), :], stg_a, wub.at[slot, pl.ds(ra, ca), :]),
                (wd_hbm.at[e, pl.ds(rd, cd), :], stg_d, wdb.at[slot, pl.ds(rd, cd), :]))

    def chunk_copy(e, c, kind, sb):
        src, stg, _ = chunk_parts(e, c, 0)[kind]
        return pltpu.make_async_copy(src, stg.at[sb], wsem.at[sb])

    def chunk_cast(e, c, kind, sb, slot):
        _, stg, dst = chunk_parts(e, c, slot)[kind]
        dst[...] = stg[sb].astype(dst.dtype)

    def unit_dma(e, u, start):
        sb = lax.rem(u, depth)
        for kind in range(3):
            @pl.when(lax.rem(u, 3) == kind)
            def _(kind=kind):
                copy = chunk_copy(e, u // 3, kind, sb)
                copy.start() if start else copy.wait()

    def unit_cast(e, u, slot):
        sb = lax.rem(u, depth)
        for kind in range(3):
            @pl.when(lax.rem(u, 3) == kind)
            def _(kind=kind):
                chunk_cast(e, u // 3, kind, sb, slot)

    def start_units(e, lo, hi):
        for k in range(depth):
            @pl.when(lo + k < hi)
            def _(k=k):
                unit_dma(e, lo + k, True)

    def finish_units(e, slot, lo, hi):
        def body(u, carry):
            unit_dma(e, u, False)
            unit_cast(e, u, slot)

            @pl.when(u + depth < hi)
            def _():
                unit_dma(e, u + depth, True)

            return carry

        lax.fori_loop(lo, hi, body, 0)

    @pl.when(g == 0)
    def _():
        start_units(nx_ref[n_grid], 0, n_units)
        finish_units(nx_ref[n_grid], ws_ref[0], 0, n_units)
        ybuf[...] = jnp.zeros_like(ybuf)
        for b in range(2):
            dump = pltpu.make_async_copy(
                ybuf.at[b], y_hbm.at[pl.ds(y_hbm.shape[0] - (2 - b) * tm * per, tm * per), :], ssem.at[b])
            dump.start()
            dump.wait()
        first = tok_copy(0, 1)
        first.start()
        first.wait()
        issue_gather(1, 0)
        tok_copy(nxt, 0).start()
        dst_copy(dump_row, 0).start()

    def step(b):
        tok_copy(0, b).wait()
        dst_copy(0, b).wait()

        @pl.when(next_active)
        def _():
            tok_copy(jnp.minimum(g + 2, n_grid - 1), 1 - b).start()
            dst_copy(g, 1 - b).start()

        wait_gather(b)

        @pl.when(g >= 1)
        def _():
            wait_scatter(b)

        lo, hi = _unpack_pairs(_load_slabs(xbuf.at[b], tm))
        lo, hi = lo.astype(BF16), hi.astype(BF16)
        ws = ws_ref[g]
        wg_ref, wu_ref, wd_ref = wgb.at[ws], wub.at[ws], wdb.at[ws]
        half = wg_ref.shape[0] // 2
        issue_gather(b, 1 - b)
        gate = _dot(lo, wg_ref[:half, :]) + _dot(hi, wg_ref[half:, :])
        up = _dot(lo, wu_ref[:half, :]) + _dot(hi, wu_ref[half:, :])
        issue_scatter(b, 1 - b)
        for kind in range(3):
            chunk_copy(nx_ref[g], cst_ref[g], kind, kind).wait()
            chunk_cast(nx_ref[g], cst_ref[g], kind, kind, 1 - ws)
        y = _dot((_silu(gate) * up).astype(BF16), wd_ref[...])
        _store_slabs(ybuf.at[b], _pack_pairs(_round_bf16(y)))

        @pl.when(last_active)
        def _():
            own = dst_copy(g, 1 - b)
            own.start()
            own.wait()
            issue_scatter(1 - b, b)
            wait_scatter(b)
            wait_scatter(1 - b)
            wait_gather(1 - b)

    @pl.when(active)
    def _():
        for kind in range(3):
            chunk_copy(nx_ref[g], cst_ref[g], kind, kind).start()

    for b in range(2):
        @pl.when(jnp.logical_and(active, lax.rem(g, 2) == b))
        def _(b=b):
            step(b)

    @pl.when(active)
    def _():
        start_units(nx_ref[g], ulo_ref[g], uhi_ref[g])
        finish_units(nx_ref[g], 1 - ws_ref[g], ulo_ref[g], uhi_ref[g])


def _moe(sched, tok_tab, dst_tab, h2s, wg, wu, wd, n_tok, tm):
    ne, d, f = wg.shape
    per = d // 2 // LANES
    n_tiles = sched[0].shape[0]
    n_rows = TOP_K * n_tok + 2 * tm
    ca, cd = d // MOE_CHUNKS, f // MOE_CHUNKS
    buf = pltpu.VMEM((2, tm * per, LANES), jnp.uint32)
    idx = pltpu.SMEM((2, 1, tm), jnp.int32)
    sems = pltpu.SemaphoreType.DMA((2,))
    hbm = pl.BlockSpec(memory_space=pl.ANY)
    return pl.pallas_call(
        _moe_kernel,
        grid_spec=pltpu.PrefetchScalarGridSpec(
            num_scalar_prefetch=len(sched),
            grid=(n_tiles,),
            in_specs=[hbm] * 6,
            out_specs=hbm,
            scratch_shapes=[buf, buf,
                            pltpu.VMEM((2, d, f), BF16), pltpu.VMEM((2, d, f), BF16), pltpu.VMEM((2, f, d), BF16),
                            pltpu.VMEM((MOE_STAGE_DEPTH, ca, f), F32), pltpu.VMEM((MOE_STAGE_DEPTH, cd, d), F32),
                            idx, idx, sems, sems, sems, pltpu.SemaphoreType.DMA((MOE_STAGE_DEPTH,))]),
        out_shape=jax.ShapeDtypeStruct((n_rows * per, LANES), jnp.uint32),
        compiler_params=_params(("arbitrary",)),
    )(*sched, tok_tab, dst_tab, h2s, wg, wu, wd)


def _moe_tables(order, tile_row0, tile_nv, n_tok, tm, per):
    assert n_tok & (n_tok - 1) == 0, "token count must be a power of two (slot index packing)"
    p = order.shape[0]
    n_tiles = tile_row0.shape[0]
    r = jnp.arange(tm, dtype=jnp.int32)[None, :]
    slot = order[jnp.minimum(tile_row0[:, None] + r, p - 1)]
    parity = (jnp.arange(n_tiles, dtype=jnp.int32) % 2)[:, None]
    dump = p + parity * tm + r
    dst = jnp.where(r < tile_nv[:, None], slot, dump)
    dst = jnp.concatenate([dst, p + tm + r], axis=0)
    tok = slot & (n_tok - 1)
    return (tok * per)[:, None, :], (dst * per)[:, None, :]


def _moe_schedule(e_idx, ne, tm):
    k, t = e_idx.shape
    p = k * t
    keys = e_idx.reshape(p)
    _, order = lax.sort((keys, jnp.arange(p, dtype=jnp.int32)), num_keys=1, is_stable=True)
    experts = jnp.arange(ne, dtype=jnp.int32)
    counts = jnp.sum((keys[None, :] == experts[:, None]).astype(jnp.int32), axis=1)
    starts = jnp.cumsum(counts) - counts
    ntile = (counts + tm - 1) // tm
    cum = jnp.cumsum(ntile)
    n_tiles = p // tm + ne
    gidx = jnp.arange(n_tiles, dtype=jnp.int32)
    te = jnp.minimum(jnp.sum((gidx[:, None] >= cum[None, :]).astype(jnp.int32), axis=1), ne - 1)
    onehot = (te[:, None] == experts[None, :]).astype(jnp.int32)
    pick = lambda v: jnp.sum(onehot * v[None, :], axis=1)
    local = gidx - pick(cum - ntile)
    nv = jnp.where(gidx < cum[-1], jnp.clip(pick(counts) - local * tm, 0, tm), 0).astype(jnp.int32)
    row0 = jnp.where(nv > 0, pick(starts) + local * tm, 0).astype(jnp.int32)
    nonempty = ntile > 0
    slot_e = (jnp.cumsum(nonempty.astype(jnp.int32)) - 1) % 2
    masked_id = jnp.where(nonempty, experts, ne)
    next_e = jnp.concatenate([lax.cummin(masked_id[::-1])[::-1][1:], jnp.full((1,), ne, jnp.int32)])
    nt, nx = pick(ntile), pick(next_e)
    cst = jnp.minimum(local, MOE_CHUNKS - 1).astype(jnp.int32)
    rest = jnp.logical_and(jnp.logical_and(nv > 0, nx < ne), jnp.logical_and(local == nt - 1, nt < MOE_CHUNKS))
    ulo = jnp.where(rest, 3 * nt, 0).astype(jnp.int32)
    uhi = jnp.where(rest, 3 * MOE_CHUNKS, 0).astype(jnp.int32)
    nx = jnp.concatenate([jnp.minimum(nx, ne - 1), te[:1]]).astype(jnp.int32)
    return order, row0, nv, (nv, pick(slot_e).astype(jnp.int32), nx, cst, ulo, uhi)


def _combine_kernel(*refs):
    y_refs = refs[:TOP_K]
    w_ref, a_ref, wsd_ref, x_ref, gate_ref, o_ref = refs[TOP_K:]
    tm = x_ref.shape[0]
    half = x_ref.shape[1] // 2
    w = w_ref[...]
    lo, hi = _unpack_pairs(_load_slabs(y_refs[0], tm))
    acc_lo, acc_hi = w[:, 0:1] * lo, w[:, 0:1] * hi
    for k in range(1, TOP_K):
        lo, hi = _unpack_pairs(_load_slabs(y_refs[k], tm))
        acc_lo = acc_lo + w[:, k:k + 1] * lo
        acc_hi = acc_hi + w[:, k:k + 1] * hi
    a = a_ref[...]
    gate = gate_ref[0]
    o_ref[:, :half] = x_ref[:, :half] + gate[:, :half] * (acc_lo + _dot(a, wsd_ref[:, :half]))
    o_ref[:, half:] = x_ref[:, half:] + gate[:, half:] * (acc_hi + _dot(a, wsd_ref[:, half:]))


def _combine(y, w_tk, a_sh, wsd, x2, gate, batch):
    t, d = x2.shape
    per = d // 2 // LANES
    kk = w_tk.shape[1]
    assert kk == TOP_K
    f = a_sh.shape[1]
    s = t // batch
    tm = _tile(s, 128)
    ns = s // tm
    nt = t // tm
    return pl.pallas_call(
        _combine_kernel,
        grid=(nt,),
        in_specs=[pl.BlockSpec((tm * per, LANES), lambda i, k=k: (k * nt + i, 0)) for k in range(TOP_K)] + [
                  pl.BlockSpec((tm, kk), lambda i: (i, 0)),
                  pl.BlockSpec((tm, f), lambda i: (i, 0)),
                  pl.BlockSpec((f, d), lambda i: (0, 0)),
                  pl.BlockSpec((tm, d), lambda i: (i, 0)),
                  pl.BlockSpec((1, 1, d), lambda i: (i // ns, 0, 0))],
        out_specs=pl.BlockSpec((tm, d), lambda i: (i, 0)),
        out_shape=jax.ShapeDtypeStruct((t, d), F32),
        compiler_params=_params(("arbitrary",)),
    )(*([y] * TOP_K), w_tk, a_sh, wsd, x2, gate)


def _round_up(v, m):
    return (v + m - 1) // m * m


def kernel(x, c, positions, w_ada, b_ada, norm1_g, w_in, q_a_norm_g, w_uq, kv_a_norm_g, w_ukv, q_norm_g, k_norm_g, w_o, norm2_g, w_router, router_bias, w_exp_gate, w_exp_up, w_exp_down, w_sh_gate, w_sh_up, w_sh_down):
    b, s, d = x.shape
    t = b * s
    depth = w_ada.shape[0]
    qr = q_a_norm_g.shape[1]
    kvr = kv_a_norm_g.shape[1]
    nh = w_uq.shape[2] // MLA_QK
    in_w = w_in.shape[2]
    ret_heads = (in_w - qr - kvr - MLA_ROPE) // (2 * RET_DK + 2 * RET_DV)
    ret_w = ret_heads * RET_DK
    ne = w_router.shape[2]
    moe_tm = 256

    pos = positions.reshape(t).astype(F32)
    inv_r = ROPE_BASE ** (-jnp.arange(0, RET_DK, 2, dtype=F32) / RET_DK)
    ang_r = pos[:, None] * inv_r
    cos_r, sin_r = jnp.cos(ang_r), jnp.sin(ang_r)
    inv_m = ROPE_BASE ** (-jnp.arange(0, MLA_ROPE, 2, dtype=F32) / MLA_ROPE)
    ang_m = pos[:, None] * inv_m
    cm, sm = jnp.cos(ang_m), jnp.sin(ang_m)
    z = jnp.zeros_like(cm)
    mla_tabs = (jnp.concatenate([cm, cm, z, z], axis=-1),
                jnp.concatenate([-sm, z, z, z], axis=-1),
                jnp.concatenate([z, sm, z, z], axis=-1))
    log_g = jnp.log(1.0 - 2.0 ** (-5.0 - jnp.arange(ret_heads, dtype=F32)))

    c_pad = jnp.zeros((8, d), F32).at[:b].set(c)
    x2 = x.reshape(t, d)
    wm = _round_up(qr + kvr + LANES, HEAD_PAD)
    ret_cols = 4 * ret_w

    for l in range(depth):
        mod = _ada(c_pad, w_ada[l], b_ada[l][None, :])[:b]
        shift_a, scale_a, gate_a, shift_m, scale_m, gate_m = (
            m.reshape(b, 1, d) for m in jnp.split(mod, 6, axis=-1))

        h1 = _norm_mod_call(x2, norm1_g[l][None, :], scale_a, shift_a, b, BF16)
        w_in_t = jnp.swapaxes(w_in, 1, 2)
        proj_ret = _mm_wcast(h1, w_in_t, l, 0, ret_cols, BF16)
        w_mla_t = jnp.pad(w_in_t[l, ret_cols:, :], ((0, wm - (in_w - ret_cols)), (0, 0))).astype(BF16)
        proj = _mm(h1, w_mla_t, BF16)

        ret = _retention(proj_ret, cos_r, sin_r, log_g, b, ret_heads, 0)

        wq = jnp.pad(w_uq[l].reshape(qr, nh, MLA_QK), ((0, 0), (0, 0), (0, HEAD_PAD - MLA_QK)))
        wq = wq.reshape(qr, nh * HEAD_PAD).astype(BF16)
        wkv = w_ukv[l].reshape(kvr, nh, MLA_NOPE + MLA_V)
        wk = wkv[:, :, :MLA_NOPE].reshape(kvr, nh * MLA_NOPE).astype(BF16)
        wv = wkv[:, :, MLA_NOPE:].reshape(kvr, nh * MLA_V).astype(BF16)
        gq = jnp.pad(q_norm_g[l], (0, HEAD_PAD - MLA_QK))[None, :]
        gk = jnp.pad(k_norm_g[l], (0, HEAD_PAD - MLA_QK))[None, :]
        q, k, v = _mla_prep(proj, wm, mla_tabs, q_a_norm_g[l][None, :], kv_a_norm_g[l][None, :],
                            gq, gk, wq, wk, wv, qr, kvr, nh)
        att = _flash(q, k, v, b, nh)

        x2 = _wo(ret, att, w_o[l].astype(BF16), x2, gate_a, b)

        wr_t = w_router[l].T
        wrh = wr_t.astype(BF16)
        wrl = (wr_t - wrh.astype(F32)).astype(BF16)
        h2p, a_sh, e_idx, w_kt = _router(x2, norm2_g[l][None, :], scale_m, shift_m, wrh, wrl,
                                         router_bias[l][:, None], w_sh_gate[l].astype(BF16),
                                         w_sh_up[l].astype(BF16), b)
        order, tile_row0, tile_nv, sched = _moe_schedule(e_idx, ne, moe_tm)
        tok_tab, dst_tab = _moe_tables(order, tile_row0, tile_nv, t, moe_tm, d // 2 // LANES)
        y = _moe(sched, tok_tab, dst_tab, h2p, w_exp_gate[l], w_exp_up[l], w_exp_down[l], t, moe_tm)
        x2 = _combine(y, w_kt.T, a_sh, w_sh_down[l].astype(BF16), x2, gate_m, b)

    return x2.reshape(b, s, d)
```
